```python
import jax
import jax.numpy as jnp
from jax import lax
import numpy as np

D_MODEL = 4096
BATCH = 4
SEQ = 4096
DEPTH = 2

GRID_W = 64
CTX_LEN = 256
N_MOD = 6
EPS = 1e-6
ROPE_BASE = 10000.0

NA_HEADS = 16
NA_HEAD_DIM = 128
NA_WIN_ROWS = 8
NA_WIN_COLS = 16

GLA_HEADS = 4
GLA_DK = 128
GLA_DV = 256
GLA_LOWRANK = 16
GLA_NORMALIZER = 16.0
GLA_CHUNK = 64

LRU_WIDTH = 1024
LRU_BLOCKS = 8
LRU_BW = LRU_WIDTH // LRU_BLOCKS
LRU_CONV = 4
LRU_C = 8.0

D_FF_DENSE = 8192
N_EXPERTS = 8
TOP_K = 2
D_FF_EXPERT = 4096

NA_W = NA_HEADS * NA_HEAD_DIM
GLA_KW = GLA_HEADS * GLA_DK
GLA_VW = GLA_HEADS * GLA_DV
MIX_W = NA_W + GLA_VW + LRU_WIDTH
IN_SPLITS = (NA_W,) * 3 + (GLA_KW,) * 2 + (GLA_VW,) * 2 + (GLA_LOWRANK,) * 2 + (LRU_WIDTH,) * 2
IN_WIDTH = sum(IN_SPLITS)
IN_SPLIT_POINTS = tuple(int(p) for p in np.cumsum(IN_SPLITS)[:-1])

kernel_name = 'hybrid_na_gla_rglru_moe_dit_block'


def rmsnorm(x, w):
    xf = x.astype(jnp.float32)
    y = xf * lax.rsqrt(jnp.mean(xf * xf, axis=-1, keepdims=True) + EPS)
    return (y * w.astype(jnp.float32)).astype(x.dtype)


def to_heads(t, n_heads):
    b, s, _ = t.shape
    return t.reshape(b, s, n_heads, -1).transpose(0, 2, 1, 3)


def from_heads(t):
    b, h, s, d = t.shape
    return t.transpose(0, 2, 1, 3).reshape(b, s, h * d)


def axial_rope(t, row, col):
    half = t.shape[-1] // 2
    quarter = half // 2
    inv_freq = ROPE_BASE ** (-jnp.arange(quarter, dtype=jnp.float32) / quarter)

    def rotate(u, p):
        ang = p.astype(jnp.float32)[:, None] * inv_freq
        cos, sin = jnp.cos(ang).astype(u.dtype), jnp.sin(ang).astype(u.dtype)
        u1, u2 = u[..., :quarter], u[..., quarter:]
        return jnp.concatenate([u1 * cos - u2 * sin, u1 * sin + u2 * cos], axis=-1)

    return jnp.concatenate([rotate(t[..., :half], row), rotate(t[..., half:], col)], axis=-1)


def dense_attention(q, k, v):
    s = jnp.einsum('bhqd,bhkd->bhqk', q, k).astype(jnp.float32)
    p = jax.nn.softmax(s, axis=-1).astype(v.dtype)
    return jnp.einsum('bhqk,bhkd->bhqd', p, v)


def neighbourhood_attention(q, k, v, k_ctx, v_ctx, rpb):
    b, h, n, d = q.shape
    rows = n // GRID_W
    wr = min(NA_WIN_ROWS, rows)
    wc = NA_WIN_COLS
    qg = q.reshape(b, h, rows, GRID_W, d)
    kg = k.reshape(b, h, rows, GRID_W, d)
    vg = v.reshape(b, h, rows, GRID_W, d)
    cols = jnp.arange(GRID_W)
    col_start = jnp.clip(cols - wc // 2, 0, GRID_W - wc)
    col_idx = col_start[:, None] + jnp.arange(wc)[None, :]
    dc = col_idx - cols[:, None] + (NA_WIN_COLS - 1)

    def row_block(r):
        rs = jnp.clip(r - wr // 2, 0, rows - wr)
        q_r = lax.dynamic_index_in_dim(qg, r, axis=2, keepdims=False)
        k_rows = lax.dynamic_slice_in_dim(kg, rs, wr, axis=2)
        v_rows = lax.dynamic_slice_in_dim(vg, rs, wr, axis=2)
        k_win = k_rows[:, :, :, col_idx]
        v_win = v_rows[:, :, :, col_idx]
        dr = rs + jnp.arange(wr) - r + (NA_WIN_ROWS - 1)
        bias = rpb[:, dr[:, None, None], dc[None, :, :]]
        s_win = jnp.einsum('bhcd,bhicjd->bhcij', q_r, k_win).astype(jnp.float32)
        s_win = s_win + bias.transpose(0, 2, 1, 3).astype(jnp.float32)[None]
        s_ctx = jnp.einsum('bhcd,bhld->bhcl', q_r, k_ctx).astype(jnp.float32)
        s = jnp.concatenate([s_win.reshape(b, h, GRID_W, wr * wc), s_ctx], axis=-1)
        p = jax.nn.softmax(s, axis=-1).astype(v.dtype)
        p_win = p[..., :wr * wc].reshape(b, h, GRID_W, wr, wc)
        p_ctx = p[..., wr * wc:]
        return (jnp.einsum('bhcij,bhicjd->bhcd', p_win, v_win)
                + jnp.einsum('bhcl,bhld->bhcd', p_ctx, v_ctx))

    o = lax.map(row_block, jnp.arange(rows))
    return o.transpose(1, 2, 0, 3, 4).reshape(b, h, n, d)


def gla_inputs(q, k, v, lr_f, lr_b, wg, bg, pos):
    f32 = jnp.float32
    q = to_heads(q, GLA_HEADS).astype(f32) * GLA_DK ** -0.5
    k = to_heads(k, GLA_HEADS).astype(f32)
    v = to_heads(v, GLA_HEADS).astype(f32)
    if pos is not None:
        q = axial_rope(q, pos[0], pos[1])
        k = axial_rope(k, pos[0], pos[1])
    log_f = to_heads(jax.nn.log_sigmoid((lr_f @ wg[0] + bg[0]).astype(f32)) / GLA_NORMALIZER, GLA_HEADS)
    log_b = to_heads(jax.nn.log_sigmoid((lr_b @ wg[1] + bg[1]).astype(f32)) / GLA_NORMALIZER, GLA_HEADS)
    return q, k, v, log_f, log_b


def gla_chunked(q, k, v, logg, h0):
    b_, h_, t_, dk = q.shape
    dv = v.shape[-1]
    nc = t_ // GLA_CHUNK
    q = q.reshape(b_, h_, nc, GLA_CHUNK, dk)
    k = k.reshape(b_, h_, nc, GLA_CHUNK, dk)
    logg = logg.reshape(b_, h_, nc, GLA_CHUNK, dk)
    v = v.reshape(b_, h_, nc, GLA_CHUNK, dv)
    b = jnp.cumsum(logg, axis=3)
    b_end = b[:, :, :, -1:, :]
    q_dec = q * jnp.exp(b)
    k_inv = k * jnp.exp(-b)
    k_end = k * jnp.exp(b_end - b)
    lower = jnp.tril(jnp.ones((GLA_CHUNK, GLA_CHUNK), dtype=bool))
    att = jnp.where(lower, jnp.einsum('bhnik,bhnjk->bhnij', q_dec, k_inv), 0.0)
    o_intra = jnp.einsum('bhnij,bhnjv->bhniv', att, v)
    d_state = jnp.einsum('bhnjk,bhnjv->bhnkv', k_end, v)
    decay = jnp.exp(b_end[:, :, :, 0, :])

    def step(state, inp):
        dec, ds = inp
        return dec[..., None] * state + ds, state

    s_final, s_prev = lax.scan(step, h0, (jnp.moveaxis(decay, 2, 0), jnp.moveaxis(d_state, 2, 0)))
    o_inter = jnp.einsum('bhnik,bhnkv->bhniv', q_dec, jnp.moveaxis(s_prev, 0, 2))
    return (o_intra + o_inter).reshape(b_, h_, t_, dv), s_final


def gla_final_state(k, v, logg):
    b = jnp.cumsum(logg, axis=2)
    return jnp.einsum('bhtk,bhtv->bhkv', k * jnp.exp(b[:, :, -1:, :] - b), v)


def gla_output(o, g, norm_w):
    b, h, t, dv = o.shape
    o = rmsnorm(o.transpose(0, 2, 1, 3), norm_w).reshape(b, t, h * dv)
    return o.astype(g.dtype) * jax.nn.silu(g)


def conv_centred(x, w, b):
    kw = w.shape[0]
    y = lax.conv_general_dilated(x, w[:, None, :], window_strides=(1,),
                                 padding=[(kw // 2, kw - 1 - kw // 2)],
                                 dimension_numbers=('NWC', 'WIO', 'NWC'),
                                 feature_group_count=x.shape[-1])
    return y + b


def rglru_coeffs(xc, wa, ba, wx, bx, lam):
    b, t, _ = xc.shape
    xb = xc.reshape(b, t, LRU_BLOCKS, LRU_BW)
    r = jax.nn.sigmoid(jnp.einsum('btnc,ncd->btnd', xb, wa).reshape(b, t, LRU_WIDTH) + ba)
    i = jax.nn.sigmoid(jnp.einsum('btnc,ncd->btnd', xb, wx).reshape(b, t, LRU_WIDTH) + bx)
    log_a = -LRU_C * r * jax.nn.softplus(-lam)
    return jnp.exp(log_a), jnp.sqrt(-jnp.expm1(2.0 * log_a)) * (i * xc)


def rglru_directions(rx, conv_w, conv_b, wa, ba, wx, bx, lam):
    xc = conv_centred(rx, conv_w, conv_b).astype(jnp.float32)
    fwd = rglru_coeffs(xc, wa[0], ba[0], wx[0], bx[0], lam[0])
    bwd = rglru_coeffs(xc, wa[1], ba[1], wx[1], bx[1], lam[1])
    return fwd, bwd


def linear_scan(a, b, h0):
    def combine(left, right):
        return left[0] * right[0], right[0] * left[1] + right[1]
    acc_a, acc_b = lax.associative_scan(combine, (a, b), axis=1)
    return acc_a * h0[:, None, :] + acc_b


def token_mixer(h_l, h_c, row, col, w_in, rpb, gla_wg, gla_bg, gla_norm, conv_w, conv_b,
                lru_wa, lru_ba, lru_wx, lru_bx, lru_lam, w_out, with_ctx_out):
    (nq_l, nk_l, nv_l, gq_l, gk_l, gv_l, gg_l, glf_l, glb_l, rx_l, rg_l) = jnp.split(h_l @ w_in, IN_SPLIT_POINTS, axis=-1)
    (nq_c, nk_c, nv_c, gq_c, gk_c, gv_c, gg_c, glf_c, glb_c, rx_c, rg_c) = jnp.split(h_c @ w_in, IN_SPLIT_POINTS, axis=-1)
    flip2 = lambda t: jnp.flip(t, axis=2)
    flip1 = lambda t: jnp.flip(t, axis=1)

    na_scale = NA_HEAD_DIM ** -0.5
    nk_c, nv_c = to_heads(nk_c, NA_HEADS), to_heads(nv_c, NA_HEADS)
    na_l = from_heads(neighbourhood_attention(to_heads(nq_l, NA_HEADS) * na_scale, to_heads(nk_l, NA_HEADS),
                                              to_heads(nv_l, NA_HEADS), nk_c, nv_c, rpb))

    q_l, k_l, v_l, lf_l, lb_l = gla_inputs(gq_l, gk_l, gv_l, glf_l, glb_l, gla_wg, gla_bg, (row, col))
    q_c, k_c, v_c, lf_c, lb_c = gla_inputs(gq_c, gk_c, gv_c, glf_c, glb_c, gla_wg, gla_bg, None)
    if with_ctx_out:
        zero_s = jnp.zeros(k_c.shape[:2] + (GLA_DK, GLA_DV), jnp.float32)
        oc_f, s_f = gla_chunked(q_c, k_c, v_c, lf_c, zero_s)
        oc_b, s_b = gla_chunked(flip2(q_c), flip2(k_c), flip2(v_c), flip2(lb_c), zero_s)
        gla_c = gla_output(oc_f + flip2(oc_b), gg_c, gla_norm)
    else:
        s_f = gla_final_state(k_c, v_c, lf_c)
        s_b = gla_final_state(flip2(k_c), flip2(v_c), flip2(lb_c))
    ol_f, _ = gla_chunked(q_l, k_l, v_l, lf_l, s_f)
    ol_b, _ = gla_chunked(flip2(q_l), flip2(k_l), flip2(v_l), flip2(lb_l), s_b)
    gla_l = gla_output(ol_f + flip2(ol_b), gg_l, gla_norm)

    (ca_f, cb_f), (ca_b, cb_b) = rglru_directions(rx_c, conv_w, conv_b, lru_wa, lru_ba, lru_wx, lru_bx, lru_lam)
    zero_h = jnp.zeros((rx_c.shape[0], LRU_WIDTH), jnp.float32)
    hc_f = linear_scan(ca_f, cb_f, zero_h)
    hc_b = flip1(linear_scan(flip1(ca_b), flip1(cb_b), zero_h))
    (la_f, lb_f), (la_b, lb_b) = rglru_directions(rx_l, conv_w, conv_b, lru_wa, lru_ba, lru_wx, lru_bx, lru_lam)
    hl_f = linear_scan(la_f, lb_f, hc_f[:, -1])
    hl_b = flip1(linear_scan(flip1(la_b), flip1(lb_b), hc_b[:, 0]))
    lru_l = (hl_f + hl_b).astype(h_l.dtype) * jax.nn.gelu(rg_l)

    out_l = jnp.concatenate([na_l, gla_l, lru_l], axis=-1) @ w_out
    if not with_ctx_out:
        return out_l, None
    na_c = from_heads(dense_attention(to_heads(nq_c, NA_HEADS) * na_scale, nk_c, nv_c))
    lru_c = (hc_f + hc_b).astype(h_c.dtype) * jax.nn.gelu(rg_c)
    out_c = jnp.concatenate([na_c, gla_c, lru_c], axis=-1) @ w_out
    return out_l, out_c


def swiglu(h, w1, w3, w2):
    return (jax.nn.silu(h @ w1) * (h @ w3)) @ w2


def moe_swiglu(h, router, w1, w3, w2):
    probs = jax.nn.softmax((h @ router).astype(jnp.float32), axis=-1)
    top_p, top_i = lax.top_k(probs, TOP_K)
    top_p = top_p / jnp.sum(top_p, axis=-1, keepdims=True)
    gates = jnp.sum(jax.nn.one_hot(top_i, N_EXPERTS, dtype=jnp.float32) * top_p[..., None], axis=-2).astype(h.dtype)
    out = jnp.zeros_like(h)
    for e in range(N_EXPERTS):
        out = out + gates[..., e:e + 1] * swiglu(h, w1[e], w3[e], w2[e])
    return out


def channel_mixer(h, l, ffd_w1, ffd_w3, ffd_w2, router, moe_w1, moe_w3, moe_w2):
    j = l // 2
    if l % 2 == 0:
        return swiglu(h, ffd_w1[j], ffd_w3[j], ffd_w2[j])
    return moe_swiglu(h, router[j], moe_w1[j], moe_w3[j], moe_w2[j])


def setup_inputs(seed: int = 0) -> dict:
    key = jax.random.key(seed)
    ks = iter(jax.random.split(key, 40))
    f32 = jnp.float32

    def nrm(shape, scale):
        return jax.random.normal(next(ks), shape, f32) * scale

    def gain(shape):
        return 1.0 + nrm(shape, 0.02)

    n_dense = (DEPTH + 1) // 2
    n_moe = DEPTH // 2
    a_init = jax.random.uniform(next(ks), (DEPTH, 2, LRU_WIDTH), f32, 0.9, 0.999)
    return {
        'x': nrm((BATCH, SEQ, D_MODEL), 1.0),
        'c': nrm((BATCH, D_MODEL), 1.0),
        'ctx': nrm((BATCH, CTX_LEN, D_MODEL), 1.0),
        'c_ctx': nrm((D_MODEL,), 1.0),
        'w_mod': nrm((DEPTH, D_MODEL, N_MOD * D_MODEL), 0.5 * D_MODEL ** -0.5),
        'b_mod': nrm((DEPTH, N_MOD * D_MODEL), 0.02),
        'norm_mix': gain((DEPTH, D_MODEL)),
        'norm_ffn': gain((DEPTH, D_MODEL)),
        'w_in': nrm((DEPTH, D_MODEL, IN_WIDTH), D_MODEL ** -0.5),
        'na_rpb': nrm((DEPTH, NA_HEADS, 2 * NA_WIN_ROWS - 1, 2 * NA_WIN_COLS - 1), 0.1),
        'gla_wg': nrm((DEPTH, 2, GLA_LOWRANK, GLA_KW), GLA_LOWRANK ** -0.5),
        'gla_bg': nrm((DEPTH, 2, GLA_KW), 0.1),
        'gla_norm': gain((DEPTH, GLA_DV)),
        'conv_w': nrm((DEPTH, LRU_CONV, LRU_WIDTH), LRU_CONV ** -0.5),
        'conv_b': nrm((DEPTH, LRU_WIDTH), 0.02),
        'lru_wa': nrm((DEPTH, 2, LRU_BLOCKS, LRU_BW, LRU_BW), LRU_BW ** -0.5),
        'lru_ba': nrm((DEPTH, 2, LRU_WIDTH), 0.02),
        'lru_wx': nrm((DEPTH, 2, LRU_BLOCKS, LRU_BW, LRU_BW), LRU_BW ** -0.5),
        'lru_bx': nrm((DEPTH, 2, LRU_WIDTH), 0.02),
        'lru_lam': jnp.log(a_init) - jnp.log1p(-a_init),
        'w_out': nrm((DEPTH, MIX_W, D_MODEL), MIX_W ** -0.5),
        'ffd_w1': nrm((n_dense, D_MODEL, D_FF_DENSE), D_MODEL ** -0.5),
        'ffd_w3': nrm((n_dense, D_MODEL, D_FF_DENSE), D_MODEL ** -0.5),
        'ffd_w2': nrm((n_dense, D_FF_DENSE, D_MODEL), D_FF_DENSE ** -0.5),
        'router': nrm((n_moe, D_MODEL, N_EXPERTS), D_MODEL ** -0.5),
        'moe_w1': nrm((n_moe, N_EXPERTS, D_MODEL, D_FF_EXPERT), D_MODEL ** -0.5),
        'moe_w3': nrm((n_moe, N_EXPERTS, D_MODEL, D_FF_EXPERT), D_MODEL ** -0.5),
        'moe_w2': nrm((n_moe, N_EXPERTS, D_FF_EXPERT, D_MODEL), D_FF_EXPERT ** -0.5),
        'final_norm': gain((D_MODEL,)),
    }


def reference(x, c, ctx, c_ctx, w_mod, b_mod, norm_mix, norm_ffn, w_in, na_rpb,
              gla_wg, gla_bg, gla_norm, conv_w, conv_b, lru_wa, lru_ba, lru_wx, lru_bx,
              lru_lam, w_out, ffd_w1, ffd_w3, ffd_w2, router, moe_w1, moe_w3, moe_w2,
              final_norm):
    n_tok = x.shape[1]
    pos = jnp.arange(n_tok, dtype=jnp.int32)
    row, col = pos // GRID_W, pos % GRID_W
    silu_c = jax.nn.silu(c)
    silu_cc = jax.nn.silu(c_ctx)
    for l in range(DEPTH):
        last = l == DEPTH - 1
        mod_l = (silu_c @ w_mod[l] + b_mod[l]).reshape(x.shape[0], 1, N_MOD, D_MODEL)
        mod_c = (silu_cc @ w_mod[l] + b_mod[l]).reshape(1, 1, N_MOD, D_MODEL)
        sh1, sc1, g1, sh2, sc2, g2 = (mod_l[:, :, i] for i in range(N_MOD))
        csh1, csc1, cg1, csh2, csc2, cg2 = (mod_c[:, :, i] for i in range(N_MOD))

        h_l = rmsnorm(x, norm_mix[l]) * (1 + sc1) + sh1
        h_c = rmsnorm(ctx, norm_mix[l]) * (1 + csc1) + csh1
        o_l, o_c = token_mixer(h_l, h_c, row, col, w_in[l], na_rpb[l], gla_wg[l], gla_bg[l],
                               gla_norm[l], conv_w[l], conv_b[l], lru_wa[l], lru_ba[l],
                               lru_wx[l], lru_bx[l], lru_lam[l], w_out[l], not last)
        x = x + g1 * o_l
        h_l = rmsnorm(x, norm_ffn[l]) * (1 + sc2) + sh2
        x = x + g2 * channel_mixer(h_l, l, ffd_w1, ffd_w3, ffd_w2, router, moe_w1, moe_w3, moe_w2)
        if not last:
            ctx = ctx + cg1 * o_c
            h_c = rmsnorm(ctx, norm_ffn[l]) * (1 + csc2) + csh2
            ctx = ctx + cg2 * channel_mixer(h_c, l, ffd_w1, ffd_w3, ffd_w2, router, moe_w1, moe_w3, moe_w2)
    return rmsnorm(x, final_norm)
```

```python
import functools

import jax
import jax.numpy as jnp
from jax import lax
from jax.experimental import pallas as pl
from jax.experimental.pallas import tpu as pltpu

F32 = jnp.float32
BF16 = jnp.bfloat16

EPS = 1e-6
ROPE_BASE = 10000.0
GRID_W = 64
NA_WIN_ROWS = 8
NA_WIN_COLS = 16
GLA_CHUNK = 64
GLA_LOWRANK = 16
GLA_NORMALIZER = 16.0
LRU_C = 8.0
LRU_CONV = 4
N_MOD = 6
TOP_K = 2
LANES = 128
SUBLANES = 8
VMEM_LIMIT = 56 * 1024 * 1024
NEG_BIG = -1e30

_NT = (((1,), (1,)), ((), ()))


def _params(*sem):
    return pltpu.CompilerParams(dimension_semantics=sem, vmem_limit_bytes=VMEM_LIMIT)


def _dot(a, b):
    return jnp.dot(a, b, preferred_element_type=F32)


def _split_bf16(x):
    hi = x.astype(BF16)
    lo = (x - hi.astype(F32)).astype(BF16)
    return hi, lo


def _softplus(x):
    return jnp.maximum(x, 0.0) + jnp.log1p(jnp.exp(-jnp.abs(x)))


def _silu(x):
    return x * jax.nn.sigmoid(x)


def _gelu_tanh(x):
    return 0.5 * x * (1.0 + jnp.tanh(0.7978845608028654 * (x + 0.044715 * (x * x * x))))


def _mod_kernel(c_ref, w_ref, b_ref, o_ref):
    c = c_ref[...]
    a_hi, a_lo = _split_bf16(_silu(c))
    w_hi, w_lo = _split_bf16(w_ref[...])
    o_ref[...] = _dot(a_hi, w_hi) + _dot(a_lo, w_hi) + _dot(a_hi, w_lo) + b_ref[...]


def _modulation(c8, w_mod, b_mod, layer, tn=512):
    d = c8.shape[1]
    n = w_mod.shape[2]
    return pl.pallas_call(
        _mod_kernel,
        grid=(n // tn,),
        in_specs=[pl.BlockSpec((SUBLANES, d), lambda j: (0, 0)),
                  pl.BlockSpec((None, d, tn), lambda j: (layer, 0, j)),
                  pl.BlockSpec((None, 1, tn), lambda j: (layer, 0, j))],
        out_specs=pl.BlockSpec((SUBLANES, tn), lambda j: (0, j)),
        out_shape=jax.ShapeDtypeStruct((SUBLANES, n), F32),
        compiler_params=_params("parallel"),
        name="modulation",
    )(c8, w_mod, b_mod.reshape(b_mod.shape[0], 1, n))


def _mod_spec(which, tile_rows, seq, n_batch, d_block, col_axis=None):
    def row(i):
        return jnp.minimum((i * tile_rows) // seq, n_batch) * N_MOD + which
    if col_axis is None:
        return pl.BlockSpec((None, 1, d_block), lambda i, *_: (row(i), 0, 0))
    return pl.BlockSpec((None, 1, d_block), lambda *g: (row(g[0]), 0, g[col_axis]))


def _norm_mod_body(x_ref, nw_ref, sc_ref, sh_ref):
    x = x_ref[...]
    y = x * lax.rsqrt(jnp.mean(x * x, axis=-1, keepdims=True) + EPS) * nw_ref[...]
    return y * (1.0 + sc_ref[...]) + sh_ref[...]


def _norm_mod_kernel(x_ref, nw_ref, sc_ref, sh_ref, o_ref):
    o_ref[...] = _norm_mod_body(x_ref, nw_ref, sc_ref, sh_ref).astype(o_ref.dtype)


def _norm_route_kernel(x_ref, nw_ref, sc_ref, sh_ref, rhi_ref, rlo_ref, o_ref, g_ref, *, n_experts):
    h = _norm_mod_body(x_ref, nw_ref, sc_ref, sh_ref)
    o_ref[...] = h.astype(o_ref.dtype)
    h_hi, h_lo = _split_bf16(h)
    logits = _dot(h_hi, rhi_ref[...]) + _dot(h_lo, rhi_ref[...]) + _dot(h_hi, rlo_ref[...])
    lane = lax.broadcasted_iota(jnp.int32, logits.shape, 1)
    valid = lane < n_experts
    logits = jnp.where(valid, logits, NEG_BIG)
    e = jnp.exp(logits - jnp.max(logits, axis=-1, keepdims=True))
    probs = jnp.where(valid, e / jnp.sum(e, axis=-1, keepdims=True), -1.0)
    p1 = jnp.max(probs, axis=-1, keepdims=True)
    i1 = jnp.min(jnp.where(probs == p1, lane, LANES), axis=-1, keepdims=True)
    rest = jnp.where(lane == i1, -1.0, probs)
    p2 = jnp.max(rest, axis=-1, keepdims=True)
    i2 = jnp.min(jnp.where(rest == p2, lane, LANES), axis=-1, keepdims=True)
    den = p1 + p2
    g_ref[...] = jnp.where(lane == i1, p1 / den, 0.0) + jnp.where(lane == i2, p2 / den, 0.0)


def _norm_mod(x_all, norm_w, mod3, which_scale, which_shift, n_rows, seq, n_batch, tr=256,
              router=None):
    d = x_all.shape[1]
    in_specs = [pl.BlockSpec((tr, d), lambda i: (i, 0)),
                pl.BlockSpec((1, d), lambda i: (0, 0)),
                _mod_spec(which_scale, tr, seq, n_batch, d),
                _mod_spec(which_shift, tr, seq, n_batch, d)]
    h_spec = pl.BlockSpec((tr, d), lambda i: (i, 0))
    h_shape = jax.ShapeDtypeStruct((n_rows, d), BF16)
    if router is None:
        return pl.pallas_call(
            _norm_mod_kernel, grid=(n_rows // tr,), in_specs=in_specs, out_specs=h_spec,
            out_shape=h_shape, compiler_params=_params("parallel"), name="norm_mod",
        )(x_all, norm_w.reshape(1, d), mod3, mod3)
    n_experts = router.shape[1]
    r_pad = jnp.pad(router, ((0, 0), (0, LANES - n_experts)))
    r_hi = r_pad.astype(BF16)
    r_lo = (r_pad - r_hi.astype(F32)).astype(BF16)
    w_spec = pl.BlockSpec((d, LANES), lambda i: (0, 0))
    return pl.pallas_call(
        functools.partial(_norm_route_kernel, n_experts=n_experts),
        grid=(n_rows // tr,), in_specs=in_specs + [w_spec, w_spec],
        out_specs=[h_spec, pl.BlockSpec((tr, LANES), lambda i: (i, 0))],
        out_shape=[h_shape, jax.ShapeDtypeStruct((n_rows, LANES), F32)],
        compiler_params=_params("parallel"), name="norm_route",
    )(x_all, norm_w.reshape(1, d), mod3, mod3, r_hi, r_lo)


def _final_norm_kernel(x_ref, nw_ref, o_ref):
    x = x_ref[...]
    o_ref[...] = x * lax.rsqrt(jnp.mean(x * x, axis=-1, keepdims=True) + EPS) * nw_ref[...]


def _final_norm(x_all, norm_w, n_rows, tr=256):
    d = x_all.shape[1]
    return pl.pallas_call(
        _final_norm_kernel, grid=(n_rows // tr,),
        in_specs=[pl.BlockSpec((tr, d), lambda i: (i, 0)), pl.BlockSpec((1, d), lambda i: (0, 0))],
        out_specs=pl.BlockSpec((tr, d), lambda i: (i, 0)),
        out_shape=jax.ShapeDtypeStruct((n_rows, d), F32),
        compiler_params=_params("parallel"), name="final_norm",
    )(x_all, norm_w.reshape(1, d))


def _mm_kernel(a_ref, w_ref, o_ref):
    o_ref[...] = _dot(a_ref[...], w_ref[...]).astype(o_ref.dtype)


def _matmul(a, w, out_dtype, tm, tn):
    m, k = a.shape
    n = w.shape[1]
    return pl.pallas_call(
        _mm_kernel, grid=(m // tm, n // tn),
        in_specs=[pl.BlockSpec((tm, k), lambda i, j: (i, 0)),
                  pl.BlockSpec((k, tn), lambda i, j: (0, j))],
        out_specs=pl.BlockSpec((tm, tn), lambda i, j: (i, j)),
        out_shape=jax.ShapeDtypeStruct((m, n), out_dtype),
        compiler_params=_params("parallel", "parallel"), name="matmul",
    )(a, w)


def _mm_res_kernel(a_ref, w_ref, x_ref, g_ref, o_ref, acc_ref, *, nk):
    k = pl.program_id(2)

    @pl.when(k == 0)
    def _():
        acc_ref[...] = jnp.zeros_like(acc_ref)

    acc_ref[...] += _dot(a_ref[...], w_ref[...])

    @pl.when(k == nk - 1)
    def _():
        o_ref[...] = x_ref[...] + g_ref[...] * acc_ref[...]


def _matmul_residual(a, w, x_all, mod3, which_gate, n_rows, seq, n_batch, tm, tn, tk):
    k = a.shape[1]
    d = w.shape[1]
    nk = k // tk
    return pl.pallas_call(
        functools.partial(_mm_res_kernel, nk=nk), grid=(n_rows // tm, d // tn, nk),
        in_specs=[pl.BlockSpec((tm, tk), lambda i, j, kk: (i, kk)),
                  pl.BlockSpec((tk, tn), lambda i, j, kk: (kk, j)),
                  pl.BlockSpec((tm, tn), lambda i, j, kk: (i, j)),
                  _mod_spec(which_gate, tm, seq, n_batch, tn, col_axis=1)],
        out_specs=pl.BlockSpec((tm, tn), lambda i, j, kk: (i, j)),
        out_shape=jax.ShapeDtypeStruct((n_rows, d), F32),
        scratch_shapes=[pltpu.VMEM((tm, tn), F32)],
        compiler_params=_params("parallel", "parallel", "arbitrary"), name="matmul_residual",
    )(a, w, x_all, mod3)


def _swiglu_kernel(a_ref, w1_ref, w3_ref, o_ref):
    a = a_ref[...]
    o_ref[...] = (_silu(_dot(a, w1_ref[...])) * _dot(a, w3_ref[...])).astype(o_ref.dtype)


def _swiglu_gated_kernel(a_ref, w1_ref, w3_ref, g_ref, o_ref):
    a = a_ref[...]
    e = pl.program_id(1)
    g = g_ref[...]
    lane = lax.broadcasted_iota(jnp.int32, g.shape, 1)
    ge = jnp.sum(jnp.where(lane == e, g, 0.0), axis=-1, keepdims=True)
    o_ref[...] = (ge * (_silu(_dot(a, w1_ref[...])) * _dot(a, w3_ref[...]))).astype(o_ref.dtype)


def _swiglu_in(a, w1, w3, tm, tn, gates=None):
    m, k = a.shape
    n_e, _, f = w1.shape
    nj = f // tn
    in_specs = [pl.BlockSpec((tm, k), lambda i, e, j: (i, 0)),
                pl.BlockSpec((None, k, tn), lambda i, e, j: (e, 0, j)),
                pl.BlockSpec((None, k, tn), lambda i, e, j: (e, 0, j))]
    args = [a, w1, w3]
    body = _swiglu_kernel
    if gates is not None:
        in_specs.append(pl.BlockSpec((tm, LANES), lambda i, e, j: (i, 0)))
        args.append(gates)
        body = _swiglu_gated_kernel
    return pl.pallas_call(
        body, grid=(m // tm, n_e, nj), in_specs=in_specs,
        out_specs=pl.BlockSpec((tm, tn), lambda i, e, j: (i, e * nj + j)),
        out_shape=jax.ShapeDtypeStruct((m, n_e * f), BF16),
        compiler_params=_params("parallel", "parallel", "parallel"), name="swiglu_in",
    )(*args)


def _na_bias_table(rpb):
    n_heads = rpb.shape[0]
    cols = jnp.arange(GRID_W)
    col_start = jnp.clip(cols - NA_WIN_COLS // 2, 0, GRID_W - NA_WIN_COLS)
    in_win = (cols[None, :] >= col_start[:, None]) & (cols[None, :] < col_start[:, None] + NA_WIN_COLS)
    dc = jnp.clip(cols[None, :] - cols[:, None] + NA_WIN_COLS - 1, 0, 2 * NA_WIN_COLS - 2)
    off = jnp.arange(NA_WIN_ROWS)
    dr = off[None, :] - off[:, None] + NA_WIN_ROWS - 1
    t = rpb[:, dr[:, :, None, None], dc[None, None, :, :]]
    t = jnp.where(in_win[None, None, None], t, NEG_BIG)
    return t.transpose(0, 1, 3, 2, 4).reshape(n_heads, NA_WIN_ROWS, GRID_W, NA_WIN_ROWS * GRID_W)


def _na_kernel(q_ref, k_ref, v_ref, qc_ref, kc_ref, vc_ref, bias_ref, o_ref, oc_ref, *, rows, scale):
    kc = kc_ref[...]
    vc = vc_ref[...]
    win = NA_WIN_ROWS * GRID_W

    def row_block(r, carry):
        rs = jnp.clip(r - NA_WIN_ROWS // 2, 0, rows - NA_WIN_ROWS)
        q0 = pl.multiple_of(r * GRID_W, GRID_W)
        k0 = pl.multiple_of(rs * GRID_W, GRID_W)
        q = q_ref[pl.ds(q0, GRID_W), :]
        kw = k_ref[pl.ds(k0, win), :]
        vw = v_ref[pl.ds(k0, win), :]
        s_w = lax.dot_general(q, kw, _NT, preferred_element_type=F32) * scale + bias_ref[r - rs]
        s_c = lax.dot_general(q, kc, _NT, preferred_element_type=F32) * scale
        m = jnp.maximum(jnp.max(s_w, axis=-1, keepdims=True), jnp.max(s_c, axis=-1, keepdims=True))
        p_w = jnp.exp(s_w - m)
        p_c = jnp.exp(s_c - m)
        den = jnp.sum(p_w, axis=-1, keepdims=True) + jnp.sum(p_c, axis=-1, keepdims=True)
        o = _dot(p_w.astype(BF16), vw) + _dot(p_c.astype(BF16), vc)
        o_ref[pl.ds(q0, GRID_W), :] = (o / den).astype(o_ref.dtype)
        return carry

    lax.fori_loop(0, rows, row_block, 0)

    s = lax.dot_general(qc_ref[...], kc, _NT, preferred_element_type=F32) * scale
    p = jnp.exp(s - jnp.max(s, axis=-1, keepdims=True))
    o = _dot(p.astype(BF16), vc) / jnp.sum(p, axis=-1, keepdims=True)
    oc_ref[...] = o.astype(oc_ref.dtype)


def _neighbourhood_attention(proj, bias_tab, n_batch, seq, ctx_len, n_heads, col0):
    dh = LANES
    rows = seq // GRID_W
    cb = n_batch * seq // ctx_len
    lat = lambda part: pl.BlockSpec((seq, dh), lambda b, h: (b, col0 + part * n_heads + h))
    ctx = lambda part: pl.BlockSpec((ctx_len, dh), lambda b, h: (cb + b, col0 + part * n_heads + h))
    return pl.pallas_call(
        functools.partial(_na_kernel, rows=rows, scale=dh ** -0.5),
        grid=(n_batch, n_heads),
        in_specs=[lat(0), lat(1), lat(2), ctx(0), ctx(1), ctx(2),
                  pl.BlockSpec((None, NA_WIN_ROWS, GRID_W, NA_WIN_ROWS * GRID_W), lambda b, h: (h, 0, 0, 0))],
        out_specs=[pl.BlockSpec((seq, dh), lambda b, h: (b, h)),
                   pl.BlockSpec((ctx_len, dh), lambda b, h: (b, h))],
        out_shape=[jax.ShapeDtypeStruct((n_batch * seq, n_heads * dh), BF16),
                   jax.ShapeDtypeStruct((n_batch * ctx_len, n_heads * dh), BF16)],
        compiler_params=_params("parallel", "parallel"), name="neighbourhood_attention",
    )(proj, proj, proj, proj, proj, proj, bias_tab)


def _rope_tables(seq, dk):
    quarter = dk // 4
    inv_freq = ROPE_BASE ** (-jnp.arange(quarter, dtype=F32) / quarter)
    pos = jnp.arange(seq, dtype=jnp.int32)
    row = (pos // GRID_W).astype(F32)[:, None] * inv_freq
    col = (pos % GRID_W).astype(F32)[:, None] * inv_freq
    cos = jnp.concatenate([jnp.cos(row), jnp.cos(row), jnp.cos(col), jnp.cos(col)], axis=-1)
    sin = jnp.concatenate([-jnp.sin(row), jnp.sin(row), -jnp.sin(col), jnp.sin(col)], axis=-1)
    return cos, sin


def _gla_kernel(q_ref, k_ref, v_ref, g_ref, lr_ref, qc_ref, kc_ref, vc_ref, gc_ref, lrc_ref,
                wg_ref, bg_ref, nw_ref, cos_ref, sin_ref, tri_ref, o_ref, oc_ref,
                s_ref, acc_ref, accc_ref, *, n_lat, n_ctx):
    dk = q_ref.shape[1]
    c = GLA_CHUNK
    row_i = lax.broadcasted_iota(jnp.int32, (c, c), 0)
    col_i = lax.broadcasted_iota(jnp.int32, (c, c), 1)
    lane = lax.broadcasted_iota(jnp.int32, (c, dk), 1)
    first_quarter = (lane % (dk // 2)) < (dk // 4)

    def swap_quarters(t):
        return jnp.where(first_quarter, pltpu.roll(t, dk - dk // 4, 1), pltpu.roll(t, dk // 4, 1))

    def chunk(refs, n, d, rope):
        qr, kr, vr, lrr = refs
        r0 = pl.multiple_of(n * c, c)
        q = qr[pl.ds(r0, c), :].astype(F32) * (dk ** -0.5)
        k = kr[pl.ds(r0, c), :].astype(F32)
        if rope:
            cos = cos_ref[pl.ds(r0, c), :]
            sin = sin_ref[pl.ds(r0, c), :]
            q = q * cos + swap_quarters(q) * sin
            k = k * cos + swap_quarters(k) * sin
        v = vr[pl.ds(r0, c), :]
        z = _dot(lrr[pl.ds(r0, c), :].astype(BF16), wg_ref[d]) + bg_ref[d:d + 1, :]
        logg = -_softplus(-z) * (1.0 / GLA_NORMALIZER)
        tri = tri_ref[d]
        g1 = logg.astype(BF16)
        rem = logg - g1.astype(F32)
        g2 = rem.astype(BF16)
        g3 = (rem - g2.astype(F32)).astype(BF16)
        b = _dot(tri, g1) + _dot(tri, g2) + _dot(tri, g3)
        b_end = b[c - 1:c, :] if d == 0 else b[0:1, :]
        q_dec = (q * jnp.exp(b)).astype(BF16)
        k_inv = (k * jnp.exp(-b)).astype(BF16)
        k_end = k * jnp.exp(b_end - b)
        att = lax.dot_general(q_dec, k_inv, _NT, preferred_element_type=F32)
        keep = (row_i >= col_i) if d == 0 else (row_i <= col_i)
        att = jnp.where(keep, att, 0.0).astype(BF16)
        state = s_ref[...]
        o = _dot(att, v) + _dot(q_dec, state.astype(BF16))
        decay = jnp.broadcast_to(jnp.exp(b_end), (dk, dk)).T
        decay = jnp.concatenate([decay] * (state.shape[1] // dk), axis=1)
        s_ref[...] = decay * state + _dot(k_end.T.astype(BF16), v)
        return o

    def finish(o, g):
        y = o * lax.rsqrt(jnp.mean(o * o, axis=-1, keepdims=True) + EPS) * nw_ref[...]
        return (y * _silu(g.astype(F32))).astype(o_ref.dtype)

    lat = (q_ref, k_ref, v_ref, lr_ref)
    ctx = (qc_ref, kc_ref, vc_ref, lrc_ref)

    def fwd_ctx(n, carry):
        accc_ref[pl.ds(pl.multiple_of(n * c, c), c), :] = chunk(ctx, n, 0, False)
        return carry

    def fwd_lat(n, carry):
        acc_ref[pl.ds(pl.multiple_of(n * c, c), c), :] = chunk(lat, n, 0, True)
        return carry

    def bwd_ctx(i, carry):
        n = n_ctx - 1 - i
        rows = pl.ds(pl.multiple_of(n * c, c), c)
        oc_ref[rows, :] = finish(accc_ref[rows, :] + chunk(ctx, n, 1, False), gc_ref[rows, :])
        return carry

    def bwd_lat(i, carry):
        n = n_lat - 1 - i
        rows = pl.ds(pl.multiple_of(n * c, c), c)
        o_ref[rows, :] = finish(acc_ref[rows, :] + chunk(lat, n, 1, True), g_ref[rows, :])
        return carry

    s_ref[...] = jnp.zeros_like(s_ref)
    lax.fori_loop(0, n_ctx, fwd_ctx, 0)
    lax.fori_loop(0, n_lat, fwd_lat, 0)
    s_ref[...] = jnp.zeros_like(s_ref)
    lax.fori_loop(0, n_ctx, bwd_ctx, 0)
    lax.fori_loop(0, n_lat, bwd_lat, 0)


def _gla(proj, lr, wg_pad, bg, norm_w, n_batch, seq, ctx_len, n_heads, dk, dv, col_q):
    assert dk == LANES and dv % dk == 0
    cb = n_batch * seq // ctx_len
    vq = dv // dk
    col_v = (col_q + 2 * n_heads) // vq
    col_g = col_v + n_heads
    cos, sin = _rope_tables(seq, dk)
    ones = jnp.ones((GLA_CHUNK, GLA_CHUNK), F32)
    tri = jnp.stack([jnp.tril(ones), jnp.triu(ones)]).astype(BF16)

    def spec(rows, width, row_block, col):
        return pl.BlockSpec((rows, width), lambda b, h: (row_block(b), col(h)))
    lat_rb = lambda b: b
    ctx_rb = lambda b: cb + b
    in_specs = []
    for rows, rb in ((seq, lat_rb), (ctx_len, ctx_rb)):
        in_specs += [spec(rows, dk, rb, lambda h: col_q + h),
                     spec(rows, dk, rb, lambda h: col_q + n_heads + h),
                     spec(rows, dv, rb, lambda h: col_v + h),
                     spec(rows, dv, rb, lambda h: col_g + h),
                     spec(rows, LANES, rb, lambda h: 0)]
    in_specs += [pl.BlockSpec((2, LANES, dk), lambda b, h: (0, 0, h)),
                 pl.BlockSpec((2, dk), lambda b, h: (0, h)),
                 pl.BlockSpec((1, dv), lambda b, h: (0, 0)),
                 pl.BlockSpec((seq, dk), lambda b, h: (0, 0)),
                 pl.BlockSpec((seq, dk), lambda b, h: (0, 0)),
                 pl.BlockSpec((2, GLA_CHUNK, GLA_CHUNK), lambda b, h: (0, 0, 0))]
    return pl.pallas_call(
        functools.partial(_gla_kernel, n_lat=seq // GLA_CHUNK, n_ctx=ctx_len // GLA_CHUNK),
        grid=(n_batch, n_heads), in_specs=in_specs,
        out_specs=[pl.BlockSpec((seq, dv), lambda b, h: (b, h)),
                   pl.BlockSpec((ctx_len, dv), lambda b, h: (b, h))],
        out_shape=[jax.ShapeDtypeStruct((n_batch * seq, n_heads * dv), BF16),
                   jax.ShapeDtypeStruct((n_batch * ctx_len, n_heads * dv), BF16)],
        scratch_shapes=[pltpu.VMEM((dk, dv), F32), pltpu.VMEM((seq, dv), F32),
                        pltpu.VMEM((ctx_len, dv), F32)],
        compiler_params=_params("parallel", "parallel"), name="gla",
    )(proj, proj, proj, proj, lr, proj, proj, proj, proj, lr,
      wg_pad, bg, norm_w.reshape(1, dv), cos, sin, tri)


def _lru_kernel(x_ref, g_ref, xc_ref, gc_ref, cw_ref, cb_ref, w4_ref, b4_ref, lam_ref, o_ref, oc_ref,
                xpad_ref, af_ref, cf_ref, ab_ref, cbk_ref, *, seq, ctx_len):
    bw = x_ref.shape[1]
    pad = SUBLANES
    sp = _softplus(-lam_ref[...])
    cw = cw_ref[...]
    w4 = w4_ref[...]
    b4 = b4_ref[...]

    def coeffs(src_ref, n):
        xpad_ref[0:pad, :] = jnp.zeros((pad, bw), F32)
        xpad_ref[pad + n:pad + n + pad, :] = jnp.zeros((pad, bw), F32)
        tile = min(n, 512)
        for t0 in range(0, n, tile):
            xpad_ref[pad + t0:pad + t0 + tile, :] = src_ref[t0:t0 + tile, :].astype(F32)
        for t0 in range(0, n, tile):
            xc = cb_ref[...]
            for j in range(LRU_CONV):
                lo = pad + t0 + j - LRU_CONV // 2
                xc = xc + cw[j:j + 1, :] * xpad_ref[lo:lo + tile, :]
            z = _dot(xc.astype(BF16), w4) + b4
            for d, (a_ref, c_ref) in enumerate(((af_ref, cf_ref), (ab_ref, cbk_ref))):
                r = jax.nn.sigmoid(z[:, (2 * d) * bw:(2 * d + 1) * bw])
                i = jax.nn.sigmoid(z[:, (2 * d + 1) * bw:(2 * d + 2) * bw])
                log_a = (-LRU_C) * r * sp[d:d + 1, :]
                a = jnp.exp(log_a)
                a_ref[t0:t0 + tile, :] = a
                c_ref[t0:t0 + tile, :] = jnp.sqrt(1.0 - a * a) * (i * xc)

    def scan(n, h0_f, h0_b):
        seg = n // SUBLANES

        def step(g, carry):
            hf, pf, hb, pb = carry
            fwd = pl.ds(g, SUBLANES, stride=seg)
            bwd = pl.ds(seg - 1 - g, SUBLANES, stride=seg)
            a = af_ref[fwd, :]
            hf = a * hf + cf_ref[fwd, :]
            pf = a * pf
            cf_ref[fwd, :] = hf
            af_ref[fwd, :] = pf
            a = ab_ref[bwd, :]
            hb = a * hb + cbk_ref[bwd, :]
            pb = a * pb
            cbk_ref[bwd, :] = hb
            ab_ref[bwd, :] = pb
            return hf, pf, hb, pb

        zero = jnp.zeros((SUBLANES, bw), F32)
        one = jnp.ones((SUBLANES, bw), F32)
        hf, pf, hb, pb = lax.fori_loop(0, seg, step, (zero, one, zero, one))
        carry_f = [h0_f]
        for s in range(SUBLANES):
            carry_f.append(pf[s:s + 1, :] * carry_f[s] + hf[s:s + 1, :])
        carry_b = [h0_b]
        for s in range(SUBLANES - 1, -1, -1):
            carry_b.append(pb[s:s + 1, :] * carry_b[-1] + hb[s:s + 1, :])
        carry_b = carry_b[::-1]
        return carry_f, carry_b

    def emit(n, carry_f, carry_b, gate_ref, out_ref):
        seg = n // SUBLANES
        for s in range(SUBLANES):
            rows = slice(s * seg, (s + 1) * seg)
            h = (cf_ref[rows, :] + af_ref[rows, :] * carry_f[s]
                 + cbk_ref[rows, :] + ab_ref[rows, :] * carry_b[s + 1])
            out_ref[rows, :] = (h * _gelu_tanh(gate_ref[rows, :].astype(F32))).astype(out_ref.dtype)

    zero_h = jnp.zeros((1, bw), F32)
    coeffs(xc_ref, ctx_len)
    cf, cbw = scan(ctx_len, zero_h, zero_h)
    emit(ctx_len, cf, cbw, gc_ref, oc_ref)
    coeffs(x_ref, seq)
    lf, lb = scan(seq, cf[SUBLANES], cbw[0])
    emit(seq, lf, lb, g_ref, o_ref)


def _rglru(proj, conv_w, conv_b, w4, b4, lam, n_batch, seq, ctx_len, col_x):
    n_blk = w4.shape[0]
    bw = LANES
    cb = n_batch * seq // ctx_len
    width = n_blk * bw
    return pl.pallas_call(
        functools.partial(_lru_kernel, seq=seq, ctx_len=ctx_len),
        grid=(n_batch, n_blk),
        in_specs=[pl.BlockSpec((seq, bw), lambda b, j: (b, col_x + j)),
                  pl.BlockSpec((seq, bw), lambda b, j: (b, col_x + n_blk + j)),
                  pl.BlockSpec((ctx_len, bw), lambda b, j: (cb + b, col_x + j)),
                  pl.BlockSpec((ctx_len, bw), lambda b, j: (cb + b, col_x + n_blk + j)),
                  pl.BlockSpec((LRU_CONV, bw), lambda b, j: (0, j)),
                  pl.BlockSpec((1, bw), lambda b, j: (0, j)),
                  pl.BlockSpec((None, bw, 4 * bw), lambda b, j: (j, 0, 0)),
                  pl.BlockSpec((None, 1, 4 * bw), lambda b, j: (j, 0, 0)),
                  pl.BlockSpec((2, bw), lambda b, j: (0, j))],
        out_specs=[pl.BlockSpec((seq, bw), lambda b, j: (b, j)),
                   pl.BlockSpec((ctx_len, bw), lambda b, j: (b, j))],
        out_shape=[jax.ShapeDtypeStruct((n_batch * seq, width), BF16),
                   jax.ShapeDtypeStruct((n_batch * ctx_len, width), BF16)],
        scratch_shapes=[pltpu.VMEM((seq + 2 * SUBLANES, bw), F32)] + [pltpu.VMEM((seq, bw), F32)] * 4,
        compiler_params=_params("parallel", "parallel"), name="rglru",
    )(proj, proj, proj, proj, conv_w, conv_b.reshape(1, width), w4, b4, lam)


def _row_tile(n_rows, cap):
    t = cap
    while n_rows % t:
        t //= 2
    return t


def kernel(x, c, ctx, c_ctx, w_mod, b_mod, norm_mix, norm_ffn, w_in, na_rpb, gla_wg, gla_bg, gla_norm,
           conv_w, conv_b, lru_wa, lru_ba, lru_wx, lru_bx, lru_lam, w_out, ffd_w1, ffd_w3, ffd_w2,
           router, moe_w1, moe_w3, moe_w2, final_norm):
    n_batch, seq, d = x.shape
    ctx_len = ctx.shape[1]
    depth = w_in.shape[0]
    n_lat = n_batch * seq
    n_tot = n_lat + n_batch * ctx_len
    na_heads = na_rpb.shape[1]
    na_w = na_heads * LANES
    gla_kw = gla_wg.shape[3]
    gla_heads = gla_kw // LANES
    gla_dv = gla_norm.shape[1]
    gla_vw = gla_heads * gla_dv
    lru_w = conv_w.shape[2]
    n_blk = lru_wa.shape[2]
    assert n_batch < SUBLANES and lru_w == n_blk * LANES and seq % (SUBLANES * GRID_W) == 0

    p_lr = 3 * na_w + 2 * gla_kw + 2 * gla_vw
    col_gla = 3 * na_w // LANES
    col_lru = p_lr // LANES

    tm = _row_tile(n_tot, 1024)
    tm_lat = _row_tile(n_lat, 1024)
    tr = _row_tile(n_tot, 256)

    x_all = jnp.concatenate([x.reshape(n_lat, d), ctx.reshape(n_batch * ctx_len, d)], axis=0)
    c8 = jnp.zeros((SUBLANES, d), F32).at[:n_batch].set(c).at[n_batch].set(c_ctx)

    for l in range(depth):
        last = l == depth - 1
        rows_out = n_lat if last else n_tot
        tmo = tm_lat if last else tm
        mod3 = _modulation(c8, w_mod, b_mod, l).reshape(SUBLANES * N_MOD, 1, d)

        h = _norm_mod(x_all, norm_mix[l], mod3, 1, 0, n_tot, seq, n_batch, tr)
        w_l = w_in[l]
        w_main = jnp.concatenate([w_l[:, :p_lr], w_l[:, p_lr + 2 * GLA_LOWRANK:]], axis=1).astype(BF16)
        w_lr = jnp.pad(w_l[:, p_lr:p_lr + 2 * GLA_LOWRANK], ((0, 0), (0, LANES - 2 * GLA_LOWRANK))).astype(BF16)
        proj = _matmul(h, w_main, BF16, tm, _row_tile(w_main.shape[1], 512))
        lr = _matmul(h, w_lr, F32, tm, LANES)

        na_l, na_c = _neighbourhood_attention(proj, _na_bias_table(na_rpb[l]), n_batch, seq, ctx_len,
                                              na_heads, 0)
        wg = gla_wg[l]
        wg_pad = jnp.zeros((2, LANES, gla_kw), F32)
        wg_pad = wg_pad.at[0, :GLA_LOWRANK].set(wg[0]).at[1, GLA_LOWRANK:2 * GLA_LOWRANK].set(wg[1])
        gla_l, gla_c = _gla(proj, lr, wg_pad.astype(BF16), gla_bg[l], gla_norm[l], n_batch, seq, ctx_len,
                            gla_heads, LANES, gla_dv, col_gla)
        w4 = jnp.concatenate([lru_wa[l, 0], lru_wx[l, 0], lru_wa[l, 1], lru_wx[l, 1]], axis=-1).astype(BF16)
        b4 = jnp.concatenate([lru_ba[l, 0].reshape(n_blk, 1, LANES), lru_bx[l, 0].reshape(n_blk, 1, LANES),
                              lru_ba[l, 1].reshape(n_blk, 1, LANES), lru_bx[l, 1].reshape(n_blk, 1, LANES)],
                             axis=-1)
        lru_l, lru_c = _rglru(proj, conv_w[l], conv_b[l], w4, b4, lru_lam[l], n_batch, seq, ctx_len, col_lru)

        mix = jnp.concatenate([na_l, gla_l, lru_l], axis=1)
        if not last:
            mix = jnp.concatenate([mix, jnp.concatenate([na_c, gla_c, lru_c], axis=1)], axis=0)
        x_all = _matmul_residual(mix, w_out[l].astype(BF16), x_all, mod3, 2, rows_out, seq, n_batch,
                                 tmo, _row_tile(d, 1024), _row_tile(mix.shape[1], 2048))

        j = l // 2
        if l % 2 == 0:
            h = _norm_mod(x_all, norm_ffn[l], mod3, 4, 3, rows_out, seq, n_batch, tr)
            act = _swiglu_in(h, ffd_w1[j].astype(BF16)[None], ffd_w3[j].astype(BF16)[None], tmo,
                             _row_tile(ffd_w1.shape[2], 512))
            w2 = ffd_w2[j].astype(BF16)
        else:
            h, gates = _norm_mod(x_all, norm_ffn[l], mod3, 4, 3, rows_out, seq, n_batch, tr, router=router[j])
            act = _swiglu_in(h, moe_w1[j].astype(BF16), moe_w3[j].astype(BF16), tmo,
                             _row_tile(moe_w1.shape[3], 512), gates=gates)
            w2 = moe_w2[j].astype(BF16).reshape(-1, d)
        x_all = _matmul_residual(act, w2, x_all, mod3, 5, rows_out, seq, n_batch, tmo,
                                 _row_tile(d, 1024), _row_tile(act.shape[1], 2048))

    out = _final_norm(x_all, final_norm, n_lat, tr)
    return out.reshape(n_batch, seq, d)
```

```python
import functools

import jax
import jax.numpy as jnp
from jax import lax
from jax.experimental import pallas as pl
from jax.experimental.pallas import tpu as pltpu

F32 = jnp.float32
BF16 = jnp.bfloat16

EPS = 1e-6
ROPE_BASE = 10000.0
GRID_W = 64
NA_WIN_ROWS = 8
NA_WIN_COLS = 16
GLA_CHUNK = 64
GLA_LOWRANK = 16
GLA_NORMALIZER = 16.0
LRU_C = 8.0
LRU_CONV = 4
N_MOD = 6
TOP_K = 2
LANES = 128
SUBLANES = 8
VMEM_LIMIT = 56 * 1024 * 1024
NEG_BIG = -1e30

_NT = (((1,), (1,)), ((), ()))


def _params(*sem):
    return pltpu.CompilerParams(dimension_semantics=sem, vmem_limit_bytes=VMEM_LIMIT)


def _dot(a, b):
    return jnp.dot(a, b, preferred_element_type=F32)


def _split_bf16(x):
    hi = x.astype(BF16)
    lo = (x - hi.astype(F32)).astype(BF16)
    return hi, lo


def _softplus(x):
    return jnp.maximum(x, 0.0) + jnp.log1p(jnp.exp(-jnp.abs(x)))


def _silu(x):
    return x * jax.nn.sigmoid(x)


def _gelu_tanh(x):
    return 0.5 * x * (1.0 + jnp.tanh(0.7978845608028654 * (x + 0.044715 * (x * x * x))))


def _mod_kernel(c_ref, w_ref, b_ref, o_ref):
    c = c_ref[...]
    a_hi, a_lo = _split_bf16(_silu(c))
    w_hi, w_lo = _split_bf16(w_ref[...])
    o_ref[...] = _dot(a_hi, w_hi) + _dot(a_lo, w_hi) + _dot(a_hi, w_lo) + b_ref[...]


def _modulation(c8, w_mod, b_mod, layer, tn=512):
    d = c8.shape[1]
    n = w_mod.shape[2]
    return pl.pallas_call(
        _mod_kernel,
        grid=(n // tn,),
        in_specs=[pl.BlockSpec((SUBLANES, d), lambda j: (0, 0)),
                  pl.BlockSpec((None, d, tn), lambda j: (layer, 0, j)),
                  pl.BlockSpec((None, 1, tn), lambda j: (layer, 0, j))],
        out_specs=pl.BlockSpec((SUBLANES, tn), lambda j: (0, j)),
        out_shape=jax.ShapeDtypeStruct((SUBLANES, n), F32),
        compiler_params=_params("parallel"),
        name="modulation",
    )(c8, w_mod, b_mod.reshape(b_mod.shape[0], 1, n))


def _mod_spec(which, tile_rows, seq, n_batch, d_block, col_axis=None):
    def row(i):
        return jnp.minimum((i * tile_rows) // seq, n_batch) * N_MOD + which
    if col_axis is None:
        return pl.BlockSpec((None, 1, d_block), lambda i, *_: (row(i), 0, 0))
    return pl.BlockSpec((None, 1, d_block), lambda *g: (row(g[0]), 0, g[col_axis]))


def _norm_mod_body(x_ref, nw_ref, sc_ref, sh_ref):
    x = x_ref[...]
    y = x * lax.rsqrt(jnp.mean(x * x, axis=-1, keepdims=True) + EPS) * nw_ref[...]
    return y * (1.0 + sc_ref[...]) + sh_ref[...]


def _norm_mod_kernel(x_ref, nw_ref, sc_ref, sh_ref, o_ref):
    o_ref[...] = _norm_mod_body(x_ref, nw_ref, sc_ref, sh_ref).astype(o_ref.dtype)


def _norm_route_kernel(x_ref, nw_ref, sc_ref, sh_ref, rhi_ref, rlo_ref, o_ref, g_ref, *, n_experts):
    h = _norm_mod_body(x_ref, nw_ref, sc_ref, sh_ref)
    o_ref[...] = h.astype(o_ref.dtype)
    h_hi, h_lo = _split_bf16(h)
    logits = _dot(h_hi, rhi_ref[...]) + _dot(h_lo, rhi_ref[...]) + _dot(h_hi, rlo_ref[...])
    lane = lax.broadcasted_iota(jnp.int32, logits.shape, 1)
    valid = lane < n_experts
    logits = jnp.where(valid, logits, NEG_BIG)
    e = jnp.exp(logits - jnp.max(logits, axis=-1, keepdims=True))
    probs = jnp.where(valid, e / jnp.sum(e, axis=-1, keepdims=True), -1.0)
    p1 = jnp.max(probs, axis=-1, keepdims=True)
    i1 = jnp.min(jnp.where(probs == p1, lane, LANES), axis=-1, keepdims=True)
    rest = jnp.where(lane == i1, -1.0, probs)
    p2 = jnp.max(rest, axis=-1, keepdims=True)
    i2 = jnp.min(jnp.where(rest == p2, lane, LANES), axis=-1, keepdims=True)
    den = p1 + p2
    g_ref[...] = (jnp.where(lane == 0, p1 / den, 0.0) + jnp.where(lane == 1, p2 / den, 0.0)
                  + jnp.where(lane == 2, i1.astype(F32), 0.0) + jnp.where(lane == 3, i2.astype(F32), 0.0))


def _norm_mod(x_all, norm_w, mod3, which_scale, which_shift, n_rows, seq, n_batch, tr=256,
              router=None):
    d = x_all.shape[1]
    in_specs = [pl.BlockSpec((tr, d), lambda i: (i, 0)),
                pl.BlockSpec((1, d), lambda i: (0, 0)),
                _mod_spec(which_scale, tr, seq, n_batch, d),
                _mod_spec(which_shift, tr, seq, n_batch, d)]
    h_spec = pl.BlockSpec((tr, d), lambda i: (i, 0))
    h_shape = jax.ShapeDtypeStruct((n_rows, d), BF16)
    if router is None:
        return pl.pallas_call(
            _norm_mod_kernel, grid=(n_rows // tr,), in_specs=in_specs, out_specs=h_spec,
            out_shape=h_shape, compiler_params=_params("parallel"), name="norm_mod",
        )(x_all, norm_w.reshape(1, d), mod3, mod3)
    n_experts = router.shape[1]
    r_pad = jnp.pad(router, ((0, 0), (0, LANES - n_experts)))
    r_hi = r_pad.astype(BF16)
    r_lo = (r_pad - r_hi.astype(F32)).astype(BF16)
    w_spec = pl.BlockSpec((d, LANES), lambda i: (0, 0))
    return pl.pallas_call(
        functools.partial(_norm_route_kernel, n_experts=n_experts),
        grid=(n_rows // tr,), in_specs=in_specs + [w_spec, w_spec],
        out_specs=[h_spec, pl.BlockSpec((tr, LANES), lambda i: (i, 0))],
        out_shape=[jax.ShapeDtypeStruct((n_rows, d), F32), jax.ShapeDtypeStruct((n_rows, LANES), F32)],
        compiler_params=_params("parallel"), name="norm_route",
    )(x_all, norm_w.reshape(1, d), mod3, mod3, r_hi, r_lo)


def _final_norm_kernel(x_ref, nw_ref, o_ref):
    x = x_ref[...]
    o_ref[...] = x * lax.rsqrt(jnp.mean(x * x, axis=-1, keepdims=True) + EPS) * nw_ref[...]


def _final_norm(x_all, norm_w, n_rows, tr=256):
    d = x_all.shape[1]
    return pl.pallas_call(
        _final_norm_kernel, grid=(n_rows // tr,),
        in_specs=[pl.BlockSpec((tr, d), lambda i: (i, 0)), pl.BlockSpec((1, d), lambda i: (0, 0))],
        out_specs=pl.BlockSpec((tr, d), lambda i: (i, 0)),
        out_shape=jax.ShapeDtypeStruct((n_rows, d), F32),
        compiler_params=_params("parallel"), name="final_norm",
    )(x_all, norm_w.reshape(1, d))


def _mm_kernel(a_ref, w_ref, o_ref):
    o_ref[...] = _dot(a_ref[...], w_ref[...]).astype(o_ref.dtype)


def _matmul(a, w, out_dtype, tm, tn):
    m, k = a.shape
    n = w.shape[1]
    return pl.pallas_call(
        _mm_kernel, grid=(m // tm, n // tn),
        in_specs=[pl.BlockSpec((tm, k), lambda i, j: (i, 0)),
                  pl.BlockSpec((k, tn), lambda i, j: (0, j))],
        out_specs=pl.BlockSpec((tm, tn), lambda i, j: (i, j)),
        out_shape=jax.ShapeDtypeStruct((m, n), out_dtype),
        compiler_params=_params("parallel", "parallel"), name="matmul",
    )(a, w)


def _mm_res_kernel(a_ref, w_ref, x_ref, g_ref, o_ref, acc_ref, *, nk):
    k = pl.program_id(2)

    @pl.when(k == 0)
    def _():
        acc_ref[...] = jnp.zeros_like(acc_ref)

    acc_ref[...] += _dot(a_ref[...], w_ref[...])

    @pl.when(k == nk - 1)
    def _():
        o_ref[...] = x_ref[...] + g_ref[...] * acc_ref[...]


def _matmul_residual(a, w, x_all, mod3, which_gate, n_rows, seq, n_batch, tm, tn, tk):
    k = a.shape[1]
    d = w.shape[1]
    nk = k // tk
    return pl.pallas_call(
        functools.partial(_mm_res_kernel, nk=nk), grid=(n_rows // tm, d // tn, nk),
        in_specs=[pl.BlockSpec((tm, tk), lambda i, j, kk: (i, kk)),
                  pl.BlockSpec((tk, tn), lambda i, j, kk: (kk, j)),
                  pl.BlockSpec((tm, tn), lambda i, j, kk: (i, j)),
                  _mod_spec(which_gate, tm, seq, n_batch, tn, col_axis=1)],
        out_specs=pl.BlockSpec((tm, tn), lambda i, j, kk: (i, j)),
        out_shape=jax.ShapeDtypeStruct((n_rows, d), F32),
        scratch_shapes=[pltpu.VMEM((tm, tn), F32)],
        compiler_params=_params("parallel", "parallel", "arbitrary"), name="matmul_residual",
    )(a, w, x_all, mod3)


def _swiglu_kernel(a_ref, w1_ref, w3_ref, o_ref):
    a = a_ref[...]
    o_ref[...] = (_silu(_dot(a, w1_ref[...])) * _dot(a, w3_ref[...])).astype(o_ref.dtype)


def _swiglu_in(a, w1, w3, tm, tn):
    m, k = a.shape
    f = w1.shape[1]
    return pl.pallas_call(
        _swiglu_kernel, grid=(m // tm, f // tn),
        in_specs=[pl.BlockSpec((tm, k), lambda i, j: (i, 0)),
                  pl.BlockSpec((k, tn), lambda i, j: (0, j)),
                  pl.BlockSpec((k, tn), lambda i, j: (0, j))],
        out_specs=pl.BlockSpec((tm, tn), lambda i, j: (i, j)),
        out_shape=jax.ShapeDtypeStruct((m, f), BF16),
        compiler_params=_params("parallel", "parallel"), name="swiglu_in",
    )(a, w1, w3)


MOE_TILE = 512


def _moe_dispatch(route, n_experts, tile):
    n = route.shape[0]
    e = jnp.concatenate([route[:, 2], route[:, 3]]).astype(jnp.int32)
    onehot = (e[:, None] == jnp.arange(n_experts, dtype=jnp.int32)[None, :]).astype(jnp.int32)
    csum = jnp.cumsum(onehot, axis=0)
    counts = csum[-1]
    rank = jnp.sum((csum - onehot) * onehot, axis=1)
    padded = (counts + tile - 1) // tile * tile
    ends = jnp.cumsum(padded)
    pos = jnp.sum(onehot * (ends - padded)[None, :], axis=1) + rank
    n_tiles = (TOP_K * n) // tile + n_experts
    tok = jnp.tile(jnp.arange(n, dtype=jnp.int32), TOP_K)
    src = jnp.zeros((n_tiles * tile,), jnp.int32).at[pos].set(tok, unique_indices=True)
    tile_id = jnp.arange(n_tiles, dtype=jnp.int32)
    tile_expert = jnp.sum((tile_id[:, None] * tile >= ends[None, :]).astype(jnp.int32), axis=1)
    n_used = ends[-1] // tile
    tile_expert = jnp.minimum(tile_expert, jnp.max(jnp.where(tile_id < n_used, tile_expert, 0)))
    return src.reshape(n_tiles, 1, tile), pos, tile_expert.astype(jnp.int32), n_used.reshape(1).astype(jnp.int32)


def _moe_in_kernel(te_ref, nu_ref, src_ref, h_hbm, w1_ref, w3_ref, o_ref, rows_ref, a_ref, sem, *, tile):
    i = pl.program_id(0)
    j = pl.program_id(1)
    used = i < nu_ref[0]

    @pl.when(jnp.logical_and(used, j == 0))
    def _():
        def issue(p, carry):
            pltpu.make_async_copy(h_hbm.at[pl.ds(src_ref[0, p], 1), :], rows_ref.at[pl.ds(p, 1), :], sem).start()
            return carry
        lax.fori_loop(0, tile, issue, 0)
        pltpu.make_async_copy(h_hbm.at[pl.ds(0, tile), :], rows_ref, sem).wait()
        a_ref[...] = rows_ref[...].astype(BF16)

    @pl.when(used)
    def _():
        a = a_ref[...]
        o_ref[...] = (_silu(_dot(a, w1_ref[...])) * _dot(a, w3_ref[...])).astype(o_ref.dtype)

    @pl.when(jnp.logical_not(used))
    def _():
        o_ref[...] = jnp.zeros_like(o_ref)


def _moe_out_kernel(te_ref, nu_ref, a_ref, w_ref, o_ref):
    used = pl.program_id(0) < nu_ref[0]

    @pl.when(used)
    def _():
        o_ref[...] = _dot(a_ref[...], w_ref[...])

    @pl.when(jnp.logical_not(used))
    def _():
        o_ref[...] = jnp.zeros_like(o_ref)


def _moe_combine_kernel(pos_ref, y_hbm, x_ref, g_ref, route_ref, o_ref, rows_ref, sem, *, tc):
    def issue(t, carry):
        for s in range(TOP_K):
            pltpu.make_async_copy(y_hbm.at[pl.ds(pos_ref[0, s * tc + t], 1), :],
                                  rows_ref.at[s, pl.ds(t, 1), :], sem).start()
        return carry
    lax.fori_loop(0, tc, issue, 0)
    for s in range(TOP_K):
        pltpu.make_async_copy(y_hbm.at[pl.ds(0, tc), :], rows_ref.at[s], sem).wait()
    route = route_ref[...]
    moe = route[:, 0:1] * rows_ref[0] + route[:, 1:2] * rows_ref[1]
    o_ref[...] = x_ref[...] + g_ref[...] * moe


def _moe(h, route, w1, w3, w2, x_all, mod3, which_gate, n_rows, seq, n_batch, tn=512, tc=256):
    n, d = h.shape
    n_experts, _, f = w1.shape
    tile = MOE_TILE
    src, pos, tile_expert, n_used = _moe_dispatch(route, n_experts, tile)
    n_tiles = src.shape[0]
    smem = functools.partial(pl.BlockSpec, memory_space=pltpu.SMEM)
    act = pl.pallas_call(
        functools.partial(_moe_in_kernel, tile=tile),
        grid_spec=pltpu.PrefetchScalarGridSpec(
            num_scalar_prefetch=2, grid=(n_tiles, f // tn),
            in_specs=[smem((None, 1, tile), lambda i, j, te, nu: (i, 0, 0)),
                      pl.BlockSpec(memory_space=pl.ANY),
                      pl.BlockSpec((None, d, tn), lambda i, j, te, nu: (te[i], 0, j)),
                      pl.BlockSpec((None, d, tn), lambda i, j, te, nu: (te[i], 0, j))],
            out_specs=pl.BlockSpec((tile, tn), lambda i, j, te, nu: (i, j)),
            scratch_shapes=[pltpu.VMEM((tile, d), F32), pltpu.VMEM((tile, d), BF16),
                            pltpu.SemaphoreType.DMA(())]),
        out_shape=jax.ShapeDtypeStruct((n_tiles * tile, f), BF16),
        compiler_params=_params("arbitrary", "arbitrary"), name="moe_in",
    )(tile_expert, n_used, src, h, w1, w3)
    tn2 = _row_tile(d, 1024)
    y = pl.pallas_call(
        _moe_out_kernel,
        grid_spec=pltpu.PrefetchScalarGridSpec(
            num_scalar_prefetch=2, grid=(n_tiles, d // tn2),
            in_specs=[pl.BlockSpec((tile, f), lambda i, j, te, nu: (i, 0)),
                      pl.BlockSpec((None, f, tn2), lambda i, j, te, nu: (te[i], 0, j))],
            out_specs=pl.BlockSpec((tile, tn2), lambda i, j, te, nu: (i, j))),
        out_shape=jax.ShapeDtypeStruct((n_tiles * tile, d), F32),
        compiler_params=_params("arbitrary", "arbitrary"), name="moe_out",
    )(tile_expert, n_used, act, w2)
    pos2 = pos.reshape(TOP_K, n // tc, tc).transpose(1, 0, 2).reshape(n // tc, 1, TOP_K * tc)
    return pl.pallas_call(
        functools.partial(_moe_combine_kernel, tc=tc),
        grid=(n_rows // tc,),
        in_specs=[smem((None, 1, TOP_K * tc), lambda i: (i, 0, 0)),
                  pl.BlockSpec(memory_space=pl.ANY),
                  pl.BlockSpec((tc, d), lambda i: (i, 0)),
                  _mod_spec(which_gate, tc, seq, n_batch, d),
                  pl.BlockSpec((tc, LANES), lambda i: (i, 0))],
        out_specs=pl.BlockSpec((tc, d), lambda i: (i, 0)),
        out_shape=jax.ShapeDtypeStruct((n_rows, d), F32),
        scratch_shapes=[pltpu.VMEM((TOP_K, tc, d), F32), pltpu.SemaphoreType.DMA(())],
        compiler_params=_params("arbitrary"), name="moe_combine",
    )(pos2, y, x_all, mod3, route)


def _na_bias_table(rpb):
    n_heads = rpb.shape[0]
    n_dr = 2 * NA_WIN_ROWS - 1
    n_dc = 2 * NA_WIN_COLS - 1
    cols = jnp.arange(GRID_W)
    col_start = jnp.clip(cols - NA_WIN_COLS // 2, 0, GRID_W - NA_WIN_COLS)
    in_win = (cols[None, :] >= col_start[:, None]) & (cols[None, :] < col_start[:, None] + NA_WIN_COLS)
    dc = cols[None, :] - cols[:, None] + NA_WIN_COLS - 1
    expand = ((dc[None] == jnp.arange(n_dc)[:, None, None]) & in_win[None]).astype(F32)
    t = jnp.dot(rpb.reshape(-1, n_dc), expand.reshape(n_dc, GRID_W * GRID_W), precision=lax.Precision.HIGHEST)
    t = jnp.where(in_win.reshape(1, 1, -1), t.reshape(n_heads, n_dr, -1), NEG_BIG)
    t = t.reshape(n_heads, n_dr, GRID_W, GRID_W)
    tab = jnp.stack([t[:, NA_WIN_ROWS - 1 - o:n_dr - o] for o in range(NA_WIN_ROWS)], axis=1)
    return tab.transpose(0, 1, 3, 2, 4).reshape(n_heads, NA_WIN_ROWS, GRID_W, NA_WIN_ROWS * GRID_W)


def _na_kernel(q_ref, k_ref, v_ref, qc_ref, kc_ref, vc_ref, bias_ref, o_ref, oc_ref, *, rows, scale):
    kc = kc_ref[...]
    vc = vc_ref[...]
    win = NA_WIN_ROWS * GRID_W

    def row_block(r, carry):
        rs = jnp.clip(r - NA_WIN_ROWS // 2, 0, rows - NA_WIN_ROWS)
        q0 = pl.multiple_of(r * GRID_W, GRID_W)
        k0 = pl.multiple_of(rs * GRID_W, GRID_W)
        q = q_ref[pl.ds(q0, GRID_W), :]
        kw = k_ref[pl.ds(k0, win), :]
        vw = v_ref[pl.ds(k0, win), :]
        s_w = lax.dot_general(q, kw, _NT, preferred_element_type=F32) * scale + bias_ref[r - rs]
        s_c = lax.dot_general(q, kc, _NT, preferred_element_type=F32) * scale
        m = jnp.maximum(jnp.max(s_w, axis=-1, keepdims=True), jnp.max(s_c, axis=-1, keepdims=True))
        p_w = jnp.exp(s_w - m)
        p_c = jnp.exp(s_c - m)
        den = jnp.sum(p_w, axis=-1, keepdims=True) + jnp.sum(p_c, axis=-1, keepdims=True)
        o = _dot(p_w.astype(BF16), vw) + _dot(p_c.astype(BF16), vc)
        o_ref[pl.ds(q0, GRID_W), :] = (o / den).astype(o_ref.dtype)
        return carry

    lax.fori_loop(0, rows, row_block, 0, unroll=2)

    s = lax.dot_general(qc_ref[...], kc, _NT, preferred_element_type=F32) * scale
    p = jnp.exp(s - jnp.max(s, axis=-1, keepdims=True))
    o = _dot(p.astype(BF16), vc) / jnp.sum(p, axis=-1, keepdims=True)
    oc_ref[...] = o.astype(oc_ref.dtype)


def _neighbourhood_attention(proj, bias_tab, n_batch, seq, ctx_len, n_heads, col0):
    dh = LANES
    rows = seq // GRID_W
    cb = n_batch * seq // ctx_len
    lat = lambda part: pl.BlockSpec((seq, dh), lambda b, h: (b, col0 + part * n_heads + h))
    ctx = lambda part: pl.BlockSpec((ctx_len, dh), lambda b, h: (cb + b, col0 + part * n_heads + h))
    return pl.pallas_call(
        functools.partial(_na_kernel, rows=rows, scale=dh ** -0.5),
        grid=(n_batch, n_heads),
        in_specs=[lat(0), lat(1), lat(2), ctx(0), ctx(1), ctx(2),
                  pl.BlockSpec((None, NA_WIN_ROWS, GRID_W, NA_WIN_ROWS * GRID_W), lambda b, h: (h, 0, 0, 0))],
        out_specs=[pl.BlockSpec((seq, dh), lambda b, h: (b, h)),
                   pl.BlockSpec((ctx_len, dh), lambda b, h: (b, h))],
        out_shape=[jax.ShapeDtypeStruct((n_batch * seq, n_heads * dh), BF16),
                   jax.ShapeDtypeStruct((n_batch * ctx_len, n_heads * dh), BF16)],
        compiler_params=_params("parallel", "parallel"), name="neighbourhood_attention",
    )(proj, proj, proj, proj, proj, proj, bias_tab)


def _rope_tables(seq, dk):
    quarter = dk // 4
    inv_freq = ROPE_BASE ** (-jnp.arange(quarter, dtype=F32) / quarter)
    pos = jnp.arange(seq, dtype=jnp.int32)
    row = (pos // GRID_W).astype(F32)[:, None] * inv_freq
    col = (pos % GRID_W).astype(F32)[:, None] * inv_freq
    cos = jnp.concatenate([jnp.cos(row), jnp.cos(row), jnp.cos(col), jnp.cos(col)], axis=-1)
    sin = jnp.concatenate([-jnp.sin(row), jnp.sin(row), -jnp.sin(col), jnp.sin(col)], axis=-1)
    return cos, sin


def _gla_kernel(q_ref, k_ref, v_ref, g_ref, lr_ref, qc_ref, kc_ref, vc_ref, gc_ref, lrc_ref,
                wg_ref, bg_ref, nw_ref, cos_ref, sin_ref, tri_ref, o_ref, oc_ref,
                s_ref, acc_ref, accc_ref, *, n_lat, n_ctx):
    dk = q_ref.shape[1]
    c = GLA_CHUNK
    row_i = lax.broadcasted_iota(jnp.int32, (c, c), 0)
    col_i = lax.broadcasted_iota(jnp.int32, (c, c), 1)
    lane = lax.broadcasted_iota(jnp.int32, (c, dk), 1)
    first_quarter = (lane % (dk // 2)) < (dk // 4)

    def swap_quarters(t):
        return jnp.where(first_quarter, pltpu.roll(t, dk - dk // 4, 1), pltpu.roll(t, dk // 4, 1))

    def chunk(refs, n, d, rope):
        qr, kr, vr, lrr = refs
        r0 = pl.multiple_of(n * c, c)
        q = qr[pl.ds(r0, c), :].astype(F32) * (dk ** -0.5)
        k = kr[pl.ds(r0, c), :].astype(F32)
        if rope:
            cos = cos_ref[pl.ds(r0, c), :]
            sin = sin_ref[pl.ds(r0, c), :]
            q = q * cos + swap_quarters(q) * sin
            k = k * cos + swap_quarters(k) * sin
        v = vr[pl.ds(r0, c), :]
        z = _dot(lrr[pl.ds(r0, c), :].astype(BF16), wg_ref[d]) + bg_ref[d:d + 1, :]
        logg = -_softplus(-z) * (1.0 / GLA_NORMALIZER)
        tri = tri_ref[d]
        g1 = logg.astype(BF16)
        rem = logg - g1.astype(F32)
        g2 = rem.astype(BF16)
        g3 = (rem - g2.astype(F32)).astype(BF16)
        b = _dot(tri, g1) + _dot(tri, g2) + _dot(tri, g3)
        b_end = b[c - 1:c, :] if d == 0 else b[0:1, :]
        q_dec = (q * jnp.exp(b)).astype(BF16)
        k_inv = (k * jnp.exp(-b)).astype(BF16)
        k_end = k * jnp.exp(b_end - b)
        att = lax.dot_general(q_dec, k_inv, _NT, preferred_element_type=F32)
        keep = (row_i >= col_i) if d == 0 else (row_i <= col_i)
        att = jnp.where(keep, att, 0.0).astype(BF16)
        state = s_ref[...]
        o = _dot(att, v) + _dot(q_dec, state.astype(BF16))
        decay = jnp.broadcast_to(jnp.exp(b_end), (dk, dk)).T
        decay = jnp.concatenate([decay] * (state.shape[1] // dk), axis=1)
        s_ref[...] = decay * state + _dot(k_end.T.astype(BF16), v)
        return o

    def finish(o, g):
        y = o * lax.rsqrt(jnp.mean(o * o, axis=-1, keepdims=True) + EPS) * nw_ref[...]
        return (y * _silu(g.astype(F32))).astype(o_ref.dtype)

    lat = (q_ref, k_ref, v_ref, lr_ref)
    ctx = (qc_ref, kc_ref, vc_ref, lrc_ref)

    def fwd_ctx(n, carry):
        accc_ref[pl.ds(pl.multiple_of(n * c, c), c), :] = chunk(ctx, n, 0, False)
        return carry

    def fwd_lat(n, carry):
        acc_ref[pl.ds(pl.multiple_of(n * c, c), c), :] = chunk(lat, n, 0, True)
        return carry

    def bwd_ctx(i, carry):
        n = n_ctx - 1 - i
        rows = pl.ds(pl.multiple_of(n * c, c), c)
        oc_ref[rows, :] = finish(accc_ref[rows, :] + chunk(ctx, n, 1, False), gc_ref[rows, :])
        return carry

    def bwd_lat(i, carry):
        n = n_lat - 1 - i
        rows = pl.ds(pl.multiple_of(n * c, c), c)
        o_ref[rows, :] = finish(acc_ref[rows, :] + chunk(lat, n, 1, True), g_ref[rows, :])
        return carry

    s_ref[...] = jnp.zeros_like(s_ref)
    lax.fori_loop(0, n_ctx, fwd_ctx, 0, unroll=2)
    lax.fori_loop(0, n_lat, fwd_lat, 0, unroll=2)
    s_ref[...] = jnp.zeros_like(s_ref)
    lax.fori_loop(0, n_ctx, bwd_ctx, 0, unroll=2)
    lax.fori_loop(0, n_lat, bwd_lat, 0, unroll=2)


def _gla(proj, lr, wg_pad, bg, norm_w, n_batch, seq, ctx_len, n_heads, dk, dv, col_q):
    assert dk == LANES and dv % dk == 0
    cb = n_batch * seq // ctx_len
    vq = dv // dk
    col_v = (col_q + 2 * n_heads) // vq
    col_g = col_v + n_heads
    cos, sin = _rope_tables(seq, dk)
    ones = jnp.ones((GLA_CHUNK, GLA_CHUNK), F32)
    tri = jnp.stack([jnp.tril(ones), jnp.triu(ones)]).astype(BF16)

    def spec(rows, width, row_block, col):
        return pl.BlockSpec((rows, width), lambda b, h: (row_block(b), col(h)))
    lat_rb = lambda b: b
    ctx_rb = lambda b: cb + b
    in_specs = []
    for rows, rb in ((seq, lat_rb), (ctx_len, ctx_rb)):
        in_specs += [spec(rows, dk, rb, lambda h: col_q + h),
                     spec(rows, dk, rb, lambda h: col_q + n_heads + h),
                     spec(rows, dv, rb, lambda h: col_v + h),
                     spec(rows, dv, rb, lambda h: col_g + h),
                     spec(rows, LANES, rb, lambda h: 0)]
    in_specs += [pl.BlockSpec((2, LANES, dk), lambda b, h: (0, 0, h)),
                 pl.BlockSpec((2, dk), lambda b, h: (0, h)),
                 pl.BlockSpec((1, dv), lambda b, h: (0, 0)),
                 pl.BlockSpec((seq, dk), lambda b, h: (0, 0)),
                 pl.BlockSpec((seq, dk), lambda b, h: (0, 0)),
                 pl.BlockSpec((2, GLA_CHUNK, GLA_CHUNK), lambda b, h: (0, 0, 0))]
    return pl.pallas_call(
        functools.partial(_gla_kernel, n_lat=seq // GLA_CHUNK, n_ctx=ctx_len // GLA_CHUNK),
        grid=(n_batch, n_heads), in_specs=in_specs,
        out_specs=[pl.BlockSpec((seq, dv), lambda b, h: (b, h)),
                   pl.BlockSpec((ctx_len, dv), lambda b, h: (b, h))],
        out_shape=[jax.ShapeDtypeStruct((n_batch * seq, n_heads * dv), BF16),
                   jax.ShapeDtypeStruct((n_batch * ctx_len, n_heads * dv), BF16)],
        scratch_shapes=[pltpu.VMEM((dk, dv), F32), pltpu.VMEM((seq, dv), F32),
                        pltpu.VMEM((ctx_len, dv), F32)],
        compiler_params=_params("parallel", "parallel"), name="gla",
    )(proj, proj, proj, proj, lr, proj, proj, proj, proj, lr,
      wg_pad, bg, norm_w.reshape(1, dv), cos, sin, tri)


def _lru_kernel(x_ref, g_ref, xc_ref, gc_ref, cw_ref, cb_ref, w4_ref, b4_ref, lam_ref, o_ref, oc_ref,
                xpad_ref, af_ref, cf_ref, ab_ref, cbk_ref, *, seq, ctx_len):
    bw = x_ref.shape[1]
    pad = SUBLANES
    sp = _softplus(-lam_ref[...])
    cw = cw_ref[...]
    w4 = w4_ref[...]
    b4 = b4_ref[...]

    def coeffs(src_ref, n):
        xpad_ref[0:pad, :] = jnp.zeros((pad, bw), F32)
        xpad_ref[pad + n:pad + n + pad, :] = jnp.zeros((pad, bw), F32)
        tile = min(n, 512)
        for t0 in range(0, n, tile):
            xpad_ref[pad + t0:pad + t0 + tile, :] = src_ref[t0:t0 + tile, :].astype(F32)
        for t0 in range(0, n, tile):
            xc = cb_ref[...]
            for j in range(LRU_CONV):
                lo = pad + t0 + j - LRU_CONV // 2
                xc = xc + cw[j:j + 1, :] * xpad_ref[lo:lo + tile, :]
            z = _dot(xc.astype(BF16), w4) + b4
            for d, (a_ref, c_ref) in enumerate(((af_ref, cf_ref), (ab_ref, cbk_ref))):
                r = jax.nn.sigmoid(z[:, (2 * d) * bw:(2 * d + 1) * bw])
                i = jax.nn.sigmoid(z[:, (2 * d + 1) * bw:(2 * d + 2) * bw])
                log_a = (-LRU_C) * r * sp[d:d + 1, :]
                a = jnp.exp(log_a)
                a_ref[t0:t0 + tile, :] = a
                c_ref[t0:t0 + tile, :] = jnp.sqrt(1.0 - a * a) * (i * xc)

    def scan(n, h0_f, h0_b):
        seg = n // SUBLANES

        def step(g, carry):
            hf, pf, hb, pb = carry
            fwd = pl.ds(g, SUBLANES, stride=seg)
            bwd = pl.ds(seg - 1 - g, SUBLANES, stride=seg)
            a = af_ref[fwd, :]
            hf = a * hf + cf_ref[fwd, :]
            pf = a * pf
            cf_ref[fwd, :] = hf
            af_ref[fwd, :] = pf
            a = ab_ref[bwd, :]
            hb = a * hb + cbk_ref[bwd, :]
            pb = a * pb
            cbk_ref[bwd, :] = hb
            ab_ref[bwd, :] = pb
            return hf, pf, hb, pb

        zero = jnp.zeros((SUBLANES, bw), F32)
        one = jnp.ones((SUBLANES, bw), F32)
        hf, pf, hb, pb = lax.fori_loop(0, seg, step, (zero, one, zero, one))
        carry_f = [h0_f]
        for s in range(SUBLANES):
            carry_f.append(pf[s:s + 1, :] * carry_f[s] + hf[s:s + 1, :])
        carry_b = [h0_b]
        for s in range(SUBLANES - 1, -1, -1):
            carry_b.append(pb[s:s + 1, :] * carry_b[-1] + hb[s:s + 1, :])
        carry_b = carry_b[::-1]
        return carry_f, carry_b

    def emit(n, carry_f, carry_b, gate_ref, out_ref):
        seg = n // SUBLANES
        for s in range(SUBLANES):
            rows = slice(s * seg, (s + 1) * seg)
            h = (cf_ref[rows, :] + af_ref[rows, :] * carry_f[s]
                 + cbk_ref[rows, :] + ab_ref[rows, :] * carry_b[s + 1])
            out_ref[rows, :] = (h * _gelu_tanh(gate_ref[rows, :].astype(F32))).astype(out_ref.dtype)

    zero_h = jnp.zeros((1, bw), F32)
    coeffs(xc_ref, ctx_len)
    cf, cbw = scan(ctx_len, zero_h, zero_h)
    emit(ctx_len, cf, cbw, gc_ref, oc_ref)
    coeffs(x_ref, seq)
    lf, lb = scan(seq, cf[SUBLANES], cbw[0])
    emit(seq, lf, lb, g_ref, o_ref)


def _rglru(proj, conv_w, conv_b, w4, b4, lam, n_batch, seq, ctx_len, col_x):
    n_blk = w4.shape[0]
    bw = LANES
    cb = n_batch * seq // ctx_len
    width = n_blk * bw
    return pl.pallas_call(
        functools.partial(_lru_kernel, seq=seq, ctx_len=ctx_len),
        grid=(n_batch, n_blk),
        in_specs=[pl.BlockSpec((seq, bw), lambda b, j: (b, col_x + j)),
                  pl.BlockSpec((seq, bw), lambda b, j: (b, col_x + n_blk + j)),
                  pl.BlockSpec((ctx_len, bw), lambda b, j: (cb + b, col_x + j)),
                  pl.BlockSpec((ctx_len, bw), lambda b, j: (cb + b, col_x + n_blk + j)),
                  pl.BlockSpec((LRU_CONV, bw), lambda b, j: (0, j)),
                  pl.BlockSpec((1, bw), lambda b, j: (0, j)),
                  pl.BlockSpec((None, bw, 4 * bw), lambda b, j: (j, 0, 0)),
                  pl.BlockSpec((None, 1, 4 * bw), lambda b, j: (j, 0, 0)),
                  pl.BlockSpec((2, bw), lambda b, j: (0, j))],
        out_specs=[pl.BlockSpec((seq, bw), lambda b, j: (b, j)),
                   pl.BlockSpec((ctx_len, bw), lambda b, j: (b, j))],
        out_shape=[jax.ShapeDtypeStruct((n_batch * seq, width), BF16),
                   jax.ShapeDtypeStruct((n_batch * ctx_len, width), BF16)],
        scratch_shapes=[pltpu.VMEM((seq + 2 * SUBLANES, bw), F32)] + [pltpu.VMEM((seq, bw), F32)] * 4,
        compiler_params=_params("parallel", "parallel"), name="rglru",
    )(proj, proj, proj, proj, conv_w, conv_b.reshape(1, width), w4, b4, lam)


def _row_tile(n_rows, cap):
    t = cap
    while n_rows % t:
        t //= 2
    return t


def kernel(x, c, ctx, c_ctx, w_mod, b_mod, norm_mix, norm_ffn, w_in, na_rpb, gla_wg, gla_bg, gla_norm,
           conv_w, conv_b, lru_wa, lru_ba, lru_wx, lru_bx, lru_lam, w_out, ffd_w1, ffd_w3, ffd_w2,
           router, moe_w1, moe_w3, moe_w2, final_norm):
    n_batch, seq, d = x.shape
    ctx_len = ctx.shape[1]
    depth = w_in.shape[0]
    n_lat = n_batch * seq
    n_tot = n_lat + n_batch * ctx_len
    na_heads = na_rpb.shape[1]
    na_w = na_heads * LANES
    gla_kw = gla_wg.shape[3]
    gla_heads = gla_kw // LANES
    gla_dv = gla_norm.shape[1]
    gla_vw = gla_heads * gla_dv
    lru_w = conv_w.shape[2]
    n_blk = lru_wa.shape[2]
    assert n_batch < SUBLANES and lru_w == n_blk * LANES and seq % (SUBLANES * GRID_W) == 0

    p_lr = 3 * na_w + 2 * gla_kw + 2 * gla_vw
    col_gla = 3 * na_w // LANES
    col_lru = p_lr // LANES

    tm = _row_tile(n_tot, 1024)
    tm_lat = _row_tile(n_lat, 1024)
    tr = _row_tile(n_tot, 256)

    x_all = jnp.concatenate([x.reshape(n_lat, d), ctx.reshape(n_batch * ctx_len, d)], axis=0)
    c8 = jnp.zeros((SUBLANES, d), F32).at[:n_batch].set(c).at[n_batch].set(c_ctx)

    for l in range(depth):
        last = l == depth - 1
        rows_out = n_lat if last else n_tot
        tmo = tm_lat if last else tm
        mod3 = _modulation(c8, w_mod, b_mod, l).reshape(SUBLANES * N_MOD, 1, d)

        h = _norm_mod(x_all, norm_mix[l], mod3, 1, 0, n_tot, seq, n_batch, tr)
        w_l = w_in[l]
        w_main = jnp.concatenate([w_l[:, :p_lr], w_l[:, p_lr + 2 * GLA_LOWRANK:]], axis=1).astype(BF16)
        w_lr = jnp.pad(w_l[:, p_lr:p_lr + 2 * GLA_LOWRANK], ((0, 0), (0, LANES - 2 * GLA_LOWRANK))).astype(BF16)
        proj = _matmul(h, w_main, BF16, tm, _row_tile(w_main.shape[1], 512))
        lr = _matmul(h, w_lr, F32, tm, LANES)

        na_l, na_c = _neighbourhood_attention(proj, _na_bias_table(na_rpb[l]), n_batch, seq, ctx_len,
                                              na_heads, 0)
        wg = gla_wg[l]
        wg_pad = jnp.zeros((2, LANES, gla_kw), F32)
        wg_pad = wg_pad.at[0, :GLA_LOWRANK].set(wg[0]).at[1, GLA_LOWRANK:2 * GLA_LOWRANK].set(wg[1])
        gla_l, gla_c = _gla(proj, lr, wg_pad.astype(BF16), gla_bg[l], gla_norm[l], n_batch, seq, ctx_len,
                            gla_heads, LANES, gla_dv, col_gla)
        w4 = jnp.concatenate([lru_wa[l, 0], lru_wx[l, 0], lru_wa[l, 1], lru_wx[l, 1]], axis=-1).astype(BF16)
        b4 = jnp.concatenate([lru_ba[l, 0].reshape(n_blk, 1, LANES), lru_bx[l, 0].reshape(n_blk, 1, LANES),
                              lru_ba[l, 1].reshape(n_blk, 1, LANES), lru_bx[l, 1].reshape(n_blk, 1, LANES)],
                             axis=-1)
        lru_l, lru_c = _rglru(proj, conv_w[l], conv_b[l], w4, b4, lru_lam[l], n_batch, seq, ctx_len, col_lru)

        mix = jnp.concatenate([na_l, gla_l, lru_l], axis=1)
        if not last:
            mix = jnp.concatenate([mix, jnp.concatenate([na_c, gla_c, lru_c], axis=1)], axis=0)
        x_all = _matmul_residual(mix, w_out[l].astype(BF16), x_all, mod3, 2, rows_out, seq, n_batch,
                                 tmo, _row_tile(d, 1024), _row_tile(mix.shape[1], 2048))

        j = l // 2
        if l % 2 == 0:
            h = _norm_mod(x_all, norm_ffn[l], mod3, 4, 3, rows_out, seq, n_batch, tr)
            act = _swiglu_in(h, ffd_w1[j].astype(BF16), ffd_w3[j].astype(BF16), tmo,
                             _row_tile(ffd_w1.shape[2], 512))
            x_all = _matmul_residual(act, ffd_w2[j].astype(BF16), x_all, mod3, 5, rows_out, seq, n_batch, tmo,
                                     _row_tile(d, 1024), _row_tile(act.shape[1], 2048))
        else:
            h, route = _norm_mod(x_all, norm_ffn[l], mod3, 4, 3, rows_out, seq, n_batch, tr, router=router[j])
            x_all = _moe(h, route, moe_w1[j].astype(BF16), moe_w3[j].astype(BF16), moe_w2[j].astype(BF16),
                         x_all, mod3, 5, rows_out, seq, n_batch, tn=_row_tile(moe_w1.shape[3], 512),
                         tc=_row_tile(rows_out, 256))

    out = _final_norm(x_all, final_norm, n_lat, tr)
    return out.reshape(n_batch, seq, d)
```

```python
import functools

import jax
import jax.numpy as jnp
from jax import lax
from jax.experimental import pallas as pl
from jax.experimental.pallas import tpu as pltpu

F32 = jnp.float32
BF16 = jnp.bfloat16

EPS = 1e-6
ROPE_BASE = 10000.0
GRID_W = 64
NA_WIN_ROWS = 8
NA_WIN_COLS = 16
GLA_CHUNK = 64
GLA_LOWRANK = 16
GLA_NORMALIZER = 16.0
LRU_C = 8.0
LRU_CONV = 4
N_MOD = 6
TOP_K = 2
LANES = 128
SUBLANES = 8
VMEM_LIMIT = 56 * 1024 * 1024
NEG_BIG = -1e30

_NT = (((1,), (1,)), ((), ()))


def _params(*sem):
    return pltpu.CompilerParams(dimension_semantics=sem, vmem_limit_bytes=VMEM_LIMIT)


def _dot(a, b):
    return jnp.dot(a, b, preferred_element_type=F32)


def _split_bf16(x):
    hi = x.astype(BF16)
    lo = (x - hi.astype(F32)).astype(BF16)
    return hi, lo


def _softplus(x):
    return jnp.maximum(x, 0.0) + jnp.log1p(jnp.exp(-jnp.abs(x)))


def _silu(x):
    return x * jax.nn.sigmoid(x)


def _gelu_tanh(x):
    return 0.5 * x * (1.0 + jnp.tanh(0.7978845608028654 * (x + 0.044715 * (x * x * x))))


def _mod_kernel(c_ref, w_ref, b_ref, o_ref):
    c = c_ref[...]
    a_hi, a_lo = _split_bf16(_silu(c))
    w_hi, w_lo = _split_bf16(w_ref[...])
    o_ref[...] = _dot(a_hi, w_hi) + _dot(a_lo, w_hi) + _dot(a_hi, w_lo) + b_ref[...]


def _modulation(c8, w_mod, b_mod, layer, tn=512):
    d = c8.shape[1]
    n = w_mod.shape[2]
    return pl.pallas_call(
        _mod_kernel,
        grid=(n // tn,),
        in_specs=[pl.BlockSpec((SUBLANES, d), lambda j: (0, 0)),
                  pl.BlockSpec((None, d, tn), lambda j: (layer, 0, j)),
                  pl.BlockSpec((None, 1, tn), lambda j: (layer, 0, j))],
        out_specs=pl.BlockSpec((SUBLANES, tn), lambda j: (0, j)),
        out_shape=jax.ShapeDtypeStruct((SUBLANES, n), F32),
        compiler_params=_params("parallel"),
        name="modulation",
    )(c8, w_mod, b_mod.reshape(b_mod.shape[0], 1, n))


def _mod_spec(which, tile_rows, seq, n_batch, d_block, col_axis=None):
    def row(i):
        return jnp.minimum((i * tile_rows) // seq, n_batch) * N_MOD + which
    if col_axis is None:
        return pl.BlockSpec((None, 1, d_block), lambda i, *_: (row(i), 0, 0))
    return pl.BlockSpec((None, 1, d_block), lambda *g: (row(g[0]), 0, g[col_axis]))


def _norm_mod_body(x_ref, nw_ref, sc_ref, sh_ref):
    x = x_ref[...]
    y = x * lax.rsqrt(jnp.mean(x * x, axis=-1, keepdims=True) + EPS) * nw_ref[...]
    return y * (1.0 + sc_ref[...]) + sh_ref[...]


def _norm_mod_kernel(x_ref, nw_ref, sc_ref, sh_ref, o_ref):
    o_ref[...] = _norm_mod_body(x_ref, nw_ref, sc_ref, sh_ref).astype(o_ref.dtype)


def _norm_route_kernel(x_ref, nw_ref, sc_ref, sh_ref, rhi_ref, rlo_ref, o_ref, g_ref, *, n_experts):
    h = _norm_mod_body(x_ref, nw_ref, sc_ref, sh_ref)
    o_ref[...] = h.astype(o_ref.dtype)
    h_hi, h_lo = _split_bf16(h)
    logits = _dot(h_hi, rhi_ref[...]) + _dot(h_lo, rhi_ref[...]) + _dot(h_hi, rlo_ref[...])
    lane = lax.broadcasted_iota(jnp.int32, logits.shape, 1)
    valid = lane < n_experts
    logits = jnp.where(valid, logits, NEG_BIG)
    e = jnp.exp(logits - jnp.max(logits, axis=-1, keepdims=True))
    probs = jnp.where(valid, e / jnp.sum(e, axis=-1, keepdims=True), -1.0)
    p1 = jnp.max(probs, axis=-1, keepdims=True)
    i1 = jnp.min(jnp.where(probs == p1, lane, LANES), axis=-1, keepdims=True)
    rest = jnp.where(lane == i1, -1.0, probs)
    p2 = jnp.max(rest, axis=-1, keepdims=True)
    i2 = jnp.min(jnp.where(rest == p2, lane, LANES), axis=-1, keepdims=True)
    den = p1 + p2
    g_ref[...] = (jnp.where(lane == 0, p1 / den, 0.0) + jnp.where(lane == 1, p2 / den, 0.0)
                  + jnp.where(lane == 2, i1.astype(F32), 0.0) + jnp.where(lane == 3, i2.astype(F32), 0.0))


def _norm_mod(x_all, norm_w, mod3, which_scale, which_shift, n_rows, seq, n_batch, tr=256,
              router=None):
    d = x_all.shape[1]
    in_specs = [pl.BlockSpec((tr, d), lambda i: (i, 0)),
                pl.BlockSpec((1, d), lambda i: (0, 0)),
                _mod_spec(which_scale, tr, seq, n_batch, d),
                _mod_spec(which_shift, tr, seq, n_batch, d)]
    h_spec = pl.BlockSpec((tr, d), lambda i: (i, 0))
    h_shape = jax.ShapeDtypeStruct((n_rows, d), BF16)
    if router is None:
        return pl.pallas_call(
            _norm_mod_kernel, grid=(n_rows // tr,), in_specs=in_specs, out_specs=h_spec,
            out_shape=h_shape, compiler_params=_params("parallel"), name="norm_mod",
        )(x_all, norm_w.reshape(1, d), mod3, mod3)
    n_experts = router.shape[1]
    r_pad = jnp.pad(router, ((0, 0), (0, LANES - n_experts)))
    r_hi = r_pad.astype(BF16)
    r_lo = (r_pad - r_hi.astype(F32)).astype(BF16)
    w_spec = pl.BlockSpec((d, LANES), lambda i: (0, 0))
    return pl.pallas_call(
        functools.partial(_norm_route_kernel, n_experts=n_experts),
        grid=(n_rows // tr,), in_specs=in_specs + [w_spec, w_spec],
        out_specs=[h_spec, pl.BlockSpec((tr, LANES), lambda i: (i, 0))],
        out_shape=[jax.ShapeDtypeStruct((n_rows, d), F32), jax.ShapeDtypeStruct((n_rows, LANES), F32)],
        compiler_params=_params("parallel"), name="norm_route",
    )(x_all, norm_w.reshape(1, d), mod3, mod3, r_hi, r_lo)


def _final_norm_kernel(x_ref, nw_ref, o_ref):
    x = x_ref[...]
    o_ref[...] = x * lax.rsqrt(jnp.mean(x * x, axis=-1, keepdims=True) + EPS) * nw_ref[...]


def _final_norm(x_all, norm_w, n_rows, tr=256):
    d = x_all.shape[1]
    return pl.pallas_call(
        _final_norm_kernel, grid=(n_rows // tr,),
        in_specs=[pl.BlockSpec((tr, d), lambda i: (i, 0)), pl.BlockSpec((1, d), lambda i: (0, 0))],
        out_specs=pl.BlockSpec((tr, d), lambda i: (i, 0)),
        out_shape=jax.ShapeDtypeStruct((n_rows, d), F32),
        compiler_params=_params("parallel"), name="final_norm",
    )(x_all, norm_w.reshape(1, d))


def _mm_kernel(a_ref, w_ref, o_ref):
    o_ref[...] = _dot(a_ref[...], w_ref[...]).astype(o_ref.dtype)


def _matmul(a, w, out_dtype, tm, tn):
    m, k = a.shape
    n = w.shape[1]
    return pl.pallas_call(
        _mm_kernel, grid=(m // tm, n // tn),
        in_specs=[pl.BlockSpec((tm, k), lambda i, j: (i, 0)),
                  pl.BlockSpec((k, tn), lambda i, j: (0, j))],
        out_specs=pl.BlockSpec((tm, tn), lambda i, j: (i, j)),
        out_shape=jax.ShapeDtypeStruct((m, n), out_dtype),
        compiler_params=_params("parallel", "parallel"), name="matmul",
    )(a, w)


def _mm_res_kernel(a_ref, w_ref, x_ref, g_ref, o_ref, acc_ref, *, nk):
    k = pl.program_id(2)

    @pl.when(k == 0)
    def _():
        acc_ref[...] = jnp.zeros_like(acc_ref)

    acc_ref[...] += _dot(a_ref[...], w_ref[...])

    @pl.when(k == nk - 1)
    def _():
        o_ref[...] = x_ref[...] + g_ref[...] * acc_ref[...]


def _matmul_residual(a, w, x_all, mod3, which_gate, n_rows, seq, n_batch, tm, tn, tk):
    k = a.shape[1]
    d = w.shape[1]
    nk = k // tk
    return pl.pallas_call(
        functools.partial(_mm_res_kernel, nk=nk), grid=(n_rows // tm, d // tn, nk),
        in_specs=[pl.BlockSpec((tm, tk), lambda i, j, kk: (i, kk)),
                  pl.BlockSpec((tk, tn), lambda i, j, kk: (kk, j)),
                  pl.BlockSpec((tm, tn), lambda i, j, kk: (i, j)),
                  _mod_spec(which_gate, tm, seq, n_batch, tn, col_axis=1)],
        out_specs=pl.BlockSpec((tm, tn), lambda i, j, kk: (i, j)),
        out_shape=jax.ShapeDtypeStruct((n_rows, d), F32),
        scratch_shapes=[pltpu.VMEM((tm, tn), F32)],
        compiler_params=_params("parallel", "parallel", "arbitrary"), name="matmul_residual",
    )(a, w, x_all, mod3)


def _swiglu_kernel(a_ref, w1_ref, w3_ref, o_ref):
    a = a_ref[...]
    o_ref[...] = (_silu(_dot(a, w1_ref[...])) * _dot(a, w3_ref[...])).astype(o_ref.dtype)


def _swiglu_in(a, w1, w3, tm, tn):
    m, k = a.shape
    f = w1.shape[1]
    return pl.pallas_call(
        _swiglu_kernel, grid=(m // tm, f // tn),
        in_specs=[pl.BlockSpec((tm, k), lambda i, j: (i, 0)),
                  pl.BlockSpec((k, tn), lambda i, j: (0, j)),
                  pl.BlockSpec((k, tn), lambda i, j: (0, j))],
        out_specs=pl.BlockSpec((tm, tn), lambda i, j: (i, j)),
        out_shape=jax.ShapeDtypeStruct((m, f), BF16),
        compiler_params=_params("parallel", "parallel"), name="swiglu_in",
    )(a, w1, w3)


MOE_TILE = 512


def _moe_dispatch(route, n_experts, tile):
    n = route.shape[0]
    e = jnp.concatenate([route[:, 2], route[:, 3]]).astype(jnp.int32)
    onehot = (e[:, None] == jnp.arange(n_experts, dtype=jnp.int32)[None, :]).astype(jnp.int32)
    csum = jnp.cumsum(onehot, axis=0)
    counts = csum[-1]
    rank = jnp.sum((csum - onehot) * onehot, axis=1)
    padded = (counts + tile - 1) // tile * tile
    ends = jnp.cumsum(padded)
    pos = jnp.sum(onehot * (ends - padded)[None, :], axis=1) + rank
    n_tiles = (TOP_K * n) // tile + n_experts
    tok = jnp.tile(jnp.arange(n, dtype=jnp.int32), TOP_K)
    src = jnp.zeros((n_tiles * tile,), jnp.int32).at[pos].set(tok, unique_indices=True)
    tile_id = jnp.arange(n_tiles, dtype=jnp.int32)
    tile_expert = jnp.sum((tile_id[:, None] * tile >= ends[None, :]).astype(jnp.int32), axis=1)
    n_used = ends[-1] // tile
    tile_expert = jnp.minimum(tile_expert, jnp.max(jnp.where(tile_id < n_used, tile_expert, 0)))
    return src.reshape(n_tiles, 1, tile), pos, tile_expert.astype(jnp.int32), n_used.reshape(1).astype(jnp.int32)


def _moe_in_kernel(te_ref, nu_ref, src_ref, h_hbm, w1_ref, w3_ref, o_ref, rows_ref, a_ref, sem, *, tile):
    i = pl.program_id(0)
    j = pl.program_id(1)
    used = i < nu_ref[0]

    @pl.when(jnp.logical_and(used, j == 0))
    def _():
        def issue(p, carry):
            pltpu.make_async_copy(h_hbm.at[pl.ds(src_ref[0, p], 1), :], rows_ref.at[pl.ds(p, 1), :], sem).start()
            return carry
        lax.fori_loop(0, tile, issue, 0)
        pltpu.make_async_copy(h_hbm.at[pl.ds(0, tile), :], rows_ref, sem).wait()
        a_ref[...] = rows_ref[...].astype(BF16)

    @pl.when(used)
    def _():
        a = a_ref[...]
        o_ref[...] = (_silu(_dot(a, w1_ref[...])) * _dot(a, w3_ref[...])).astype(o_ref.dtype)

    @pl.when(jnp.logical_not(used))
    def _():
        o_ref[...] = jnp.zeros_like(o_ref)


def _moe_out_kernel(te_ref, nu_ref, a_ref, w_ref, o_ref):
    used = pl.program_id(0) < nu_ref[0]

    @pl.when(used)
    def _():
        o_ref[...] = _dot(a_ref[...], w_ref[...])

    @pl.when(jnp.logical_not(used))
    def _():
        o_ref[...] = jnp.zeros_like(o_ref)


def _moe_combine_kernel(pos_ref, y_hbm, x_ref, g_ref, route_ref, o_ref, rows_ref, sem, *, tc):
    def issue(t, carry):
        for s in range(TOP_K):
            pltpu.make_async_copy(y_hbm.at[pl.ds(pos_ref[0, s * tc + t], 1), :],
                                  rows_ref.at[s, pl.ds(t, 1), :], sem).start()
        return carry
    lax.fori_loop(0, tc, issue, 0)
    for s in range(TOP_K):
        pltpu.make_async_copy(y_hbm.at[pl.ds(0, tc), :], rows_ref.at[s], sem).wait()
    route = route_ref[...]
    moe = route[:, 0:1] * rows_ref[0] + route[:, 1:2] * rows_ref[1]
    o_ref[...] = x_ref[...] + g_ref[...] * moe


def _moe(h, route, w1, w3, w2, x_all, mod3, which_gate, n_rows, seq, n_batch, tn=512, tc=256):
    n, d = h.shape
    n_experts, _, f = w1.shape
    tile = MOE_TILE
    src, pos, tile_expert, n_used = _moe_dispatch(route, n_experts, tile)
    n_tiles = src.shape[0]
    smem = functools.partial(pl.BlockSpec, memory_space=pltpu.SMEM)
    act = pl.pallas_call(
        functools.partial(_moe_in_kernel, tile=tile),
        grid_spec=pltpu.PrefetchScalarGridSpec(
            num_scalar_prefetch=2, grid=(n_tiles, f // tn),
            in_specs=[smem((None, 1, tile), lambda i, j, te, nu: (i, 0, 0)),
                      pl.BlockSpec(memory_space=pl.ANY),
                      pl.BlockSpec((None, d, tn), lambda i, j, te, nu: (te[i], 0, j)),
                      pl.BlockSpec((None, d, tn), lambda i, j, te, nu: (te[i], 0, j))],
            out_specs=pl.BlockSpec((tile, tn), lambda i, j, te, nu: (i, j)),
            scratch_shapes=[pltpu.VMEM((tile, d), F32), pltpu.VMEM((tile, d), BF16),
                            pltpu.SemaphoreType.DMA(())]),
        out_shape=jax.ShapeDtypeStruct((n_tiles * tile, f), BF16),
        compiler_params=_params("arbitrary", "arbitrary"), name="moe_in",
    )(tile_expert, n_used, src, h, w1, w3)
    tn2 = _row_tile(d, 1024)
    y = pl.pallas_call(
        _moe_out_kernel,
        grid_spec=pltpu.PrefetchScalarGridSpec(
            num_scalar_prefetch=2, grid=(n_tiles, d // tn2),
            in_specs=[pl.BlockSpec((tile, f), lambda i, j, te, nu: (i, 0)),
                      pl.BlockSpec((None, f, tn2), lambda i, j, te, nu: (te[i], 0, j))],
            out_specs=pl.BlockSpec((tile, tn2), lambda i, j, te, nu: (i, j))),
        out_shape=jax.ShapeDtypeStruct((n_tiles * tile, d), F32),
        compiler_params=_params("arbitrary", "arbitrary"), name="moe_out",
    )(tile_expert, n_used, act, w2)
    pos2 = pos.reshape(TOP_K, n // tc, tc).transpose(1, 0, 2).reshape(n // tc, 1, TOP_K * tc)
    return pl.pallas_call(
        functools.partial(_moe_combine_kernel, tc=tc),
        grid=(n_rows // tc,),
        in_specs=[smem((None, 1, TOP_K * tc), lambda i: (i, 0, 0)),
                  pl.BlockSpec(memory_space=pl.ANY),
                  pl.BlockSpec((tc, d), lambda i: (i, 0)),
                  _mod_spec(which_gate, tc, seq, n_batch, d),
                  pl.BlockSpec((tc, LANES), lambda i: (i, 0))],
        out_specs=pl.BlockSpec((tc, d), lambda i: (i, 0)),
        out_shape=jax.ShapeDtypeStruct((n_rows, d), F32),
        scratch_shapes=[pltpu.VMEM((TOP_K, tc, d), F32), pltpu.SemaphoreType.DMA(())],
        compiler_params=_params("arbitrary"), name="moe_combine",
    )(pos2, y, x_all, mod3, route)


def _na_bias_table(rpb):
    n_heads = rpb.shape[0]
    n_dr = 2 * NA_WIN_ROWS - 1
    n_dc = 2 * NA_WIN_COLS - 1
    cols = jnp.arange(GRID_W)
    col_start = jnp.clip(cols - NA_WIN_COLS // 2, 0, GRID_W - NA_WIN_COLS)
    in_win = (cols[None, :] >= col_start[:, None]) & (cols[None, :] < col_start[:, None] + NA_WIN_COLS)
    dc = cols[None, :] - cols[:, None] + NA_WIN_COLS - 1
    expand = ((dc[None] == jnp.arange(n_dc)[:, None, None]) & in_win[None]).astype(F32)
    t = jnp.dot(rpb.reshape(-1, n_dc), expand.reshape(n_dc, GRID_W * GRID_W), precision=lax.Precision.HIGHEST)
    t = jnp.where(in_win.reshape(1, 1, -1), t.reshape(n_heads, n_dr, -1), NEG_BIG)
    t = t.reshape(n_heads, n_dr, GRID_W, GRID_W)
    masked = jnp.full((n_heads, GRID_W, GRID_W), NEG_BIG, F32)
    kinds = []
    for kind in range(3):
        q_rows = []
        for i in range(NA_QROWS):
            lo, dr0 = ((0, NA_WIN_ROWS - 1 - i), (i, NA_WIN_ROWS // 2 - 1),
                       (NA_QROWS, NA_WIN_ROWS - 1 - NA_QROWS - i))[kind]
            pieces = [t[:, dr0 + j - lo] if lo <= j < lo + NA_WIN_ROWS else masked for j in range(NA_KROWS)]
            q_rows.append(jnp.concatenate(pieces, axis=-1))
        kinds.append(jnp.concatenate(q_rows, axis=1))
    return jnp.stack(kinds, axis=1)


NA_QROWS = 4
NA_KROWS = NA_WIN_ROWS + NA_QROWS


def _na_kernel(q_ref, k_ref, v_ref, qc_ref, kc_ref, vc_ref, bias_ref, o_ref, oc_ref, *, rows, scale):
    kc = kc_ref[...]
    vc = vc_ref[...]
    nq = NA_QROWS * GRID_W
    nk = NA_KROWS * GRID_W
    n_blocks = rows // NA_QROWS

    def block(i, carry):
        base = jnp.clip(i * NA_QROWS - NA_WIN_ROWS // 2, 0, rows - NA_KROWS)
        kind = jnp.where(i == 0, 0, jnp.where(i == n_blocks - 1, 2, 1))
        q0 = pl.multiple_of(i * nq, nq)
        k0 = pl.multiple_of(base * GRID_W, nq)
        q = q_ref[pl.ds(q0, nq), :]
        kw = k_ref[pl.ds(k0, nk), :]
        vw = v_ref[pl.ds(k0, nk), :]
        s_w = lax.dot_general(q, kw, _NT, preferred_element_type=F32) * scale + bias_ref[kind]
        s_c = lax.dot_general(q, kc, _NT, preferred_element_type=F32) * scale
        m = jnp.maximum(jnp.max(s_w, axis=-1, keepdims=True), jnp.max(s_c, axis=-1, keepdims=True))
        p_w = jnp.exp(s_w - m)
        p_c = jnp.exp(s_c - m)
        den = jnp.sum(p_w, axis=-1, keepdims=True) + jnp.sum(p_c, axis=-1, keepdims=True)
        o = _dot(p_w.astype(BF16), vw) + _dot(p_c.astype(BF16), vc)
        o_ref[pl.ds(q0, nq), :] = (o / den).astype(o_ref.dtype)
        return carry

    lax.fori_loop(0, n_blocks, block, 0)

    s = lax.dot_general(qc_ref[...], kc, _NT, preferred_element_type=F32) * scale
    p = jnp.exp(s - jnp.max(s, axis=-1, keepdims=True))
    o = _dot(p.astype(BF16), vc) / jnp.sum(p, axis=-1, keepdims=True)
    oc_ref[...] = o.astype(oc_ref.dtype)


def _neighbourhood_attention(proj, bias_tab, n_batch, seq, ctx_len, n_heads, col0):
    dh = LANES
    rows = seq // GRID_W
    cb = n_batch * seq // ctx_len
    lat = lambda part: pl.BlockSpec((seq, dh), lambda b, h: (b, col0 + part * n_heads + h))
    ctx = lambda part: pl.BlockSpec((ctx_len, dh), lambda b, h: (cb + b, col0 + part * n_heads + h))
    return pl.pallas_call(
        functools.partial(_na_kernel, rows=rows, scale=dh ** -0.5),
        grid=(n_batch, n_heads),
        in_specs=[lat(0), lat(1), lat(2), ctx(0), ctx(1), ctx(2),
                  pl.BlockSpec((None, 3, NA_QROWS * GRID_W, NA_KROWS * GRID_W), lambda b, h: (h, 0, 0, 0))],
        out_specs=[pl.BlockSpec((seq, dh), lambda b, h: (b, h)),
                   pl.BlockSpec((ctx_len, dh), lambda b, h: (b, h))],
        out_shape=[jax.ShapeDtypeStruct((n_batch * seq, n_heads * dh), BF16),
                   jax.ShapeDtypeStruct((n_batch * ctx_len, n_heads * dh), BF16)],
        compiler_params=_params("parallel", "parallel"), name="neighbourhood_attention",
    )(proj, proj, proj, proj, proj, proj, bias_tab)


def _rope_tables(seq, dk):
    quarter = dk // 4
    inv_freq = ROPE_BASE ** (-jnp.arange(quarter, dtype=F32) / quarter)
    pos = jnp.arange(seq, dtype=jnp.int32)
    row = (pos // GRID_W).astype(F32)[:, None] * inv_freq
    col = (pos % GRID_W).astype(F32)[:, None] * inv_freq
    cos = jnp.concatenate([jnp.cos(row), jnp.cos(row), jnp.cos(col), jnp.cos(col)], axis=-1)
    sin = jnp.concatenate([-jnp.sin(row), jnp.sin(row), -jnp.sin(col), jnp.sin(col)], axis=-1)
    return cos, sin


GLA_BLOCK_CHUNKS = 4


def _gla_kernel(q_ref, k_ref, v_ref, g_ref, lr_ref, qc_ref, kc_ref, vc_ref, gc_ref, lrc_ref,
                wg_ref, bg_ref, nw_ref, cos_ref, sin_ref, tri_ref, keep_ref, vmask_ref, o_ref, oc_ref,
                s_ref, acc_ref, accc_ref, *, n_lat, n_ctx):
    dk = q_ref.shape[1]
    dv = v_ref.shape[1]
    c = GLA_CHUNK
    nb_lat = min(GLA_BLOCK_CHUNKS, n_lat)
    nb_ctx = min(GLA_BLOCK_CHUNKS, n_ctx)

    def swap_quarters(t):
        lane = lax.broadcasted_iota(jnp.int32, t.shape, 1)
        first_quarter = (lane % (dk // 2)) < (dk // 4)
        return jnp.where(first_quarter, pltpu.roll(t, dk - dk // 4, 1), pltpu.roll(t, dk // 4, 1))

    def block(refs, blk, nb, d, rope):
        qr, kr, vr, lrr = refs
        n = nb * c
        rows = pl.ds(pl.multiple_of(blk * n, n), n)
        q = qr[rows, :].astype(F32) * (dk ** -0.5)
        k = kr[rows, :].astype(F32)
        if rope:
            cos = cos_ref[rows, :]
            sin = sin_ref[rows, :]
            q = q * cos + swap_quarters(q) * sin
            k = k * cos + swap_quarters(k) * sin
        v = vr[rows, :]
        z = _dot(lrr[rows, :].astype(BF16), wg_ref[d]) + bg_ref[d:d + 1, :]
        logg = -_softplus(-z) * (1.0 / GLA_NORMALIZER)
        g1 = logg.astype(BF16)
        rem = logg - g1.astype(F32)
        g2 = rem.astype(BF16)
        g3 = (rem - g2.astype(F32)).astype(BF16)
        b3 = _dot(tri_ref[d, :n, :n], jnp.concatenate([g1, g2, g3], axis=1))
        b = b3[:, :dk] + b3[:, dk:2 * dk] + b3[:, 2 * dk:]
        end_row = c - 1 if d == 0 else 0
        ends = [b[m * c + end_row:m * c + end_row + 1, :] for m in range(nb)]
        b_end = jnp.concatenate([jnp.broadcast_to(e, (c, dk)) for e in ends], axis=0)
        q_dec = (q * jnp.exp(b)).astype(BF16)
        k_inv = (k * jnp.exp(-b)).astype(BF16)
        k_end_t = (k * jnp.exp(b_end - b)).T.astype(BF16)
        att = lax.dot_general(q_dec, k_inv, _NT, preferred_element_type=F32)
        att = jnp.where(keep_ref[d, :n, :n] != 0.0, att, 0.0).astype(BF16)
        o_intra = _dot(att, v)
        v_bd = jnp.concatenate([v] * nb, axis=1) * vmask_ref[:n, :nb * dv]
        d_all = _dot(k_end_t, v_bd)
        ends_t = jnp.concatenate(ends + [jnp.zeros((dk - nb, dk), F32)], axis=0).T
        state = s_ref[...]
        outs = [None] * nb
        for m in (range(nb) if d == 0 else range(nb - 1, -1, -1)):
            sl = slice(m * c, (m + 1) * c)
            outs[m] = o_intra[sl, :] + _dot(q_dec[sl, :], state.astype(BF16))
            decay = jnp.broadcast_to(jnp.exp(ends_t[:, m:m + 1]), state.shape)
            state = decay * state + d_all[:, m * dv:(m + 1) * dv]
        s_ref[...] = state
        return jnp.concatenate(outs, axis=0)

    def finish(o, g):
        y = o * lax.rsqrt(jnp.mean(o * o, axis=-1, keepdims=True) + EPS) * nw_ref[...]
        return (y * _silu(g.astype(F32))).astype(o_ref.dtype)

    lat = (q_ref, k_ref, v_ref, lr_ref)
    ctx = (qc_ref, kc_ref, vc_ref, lrc_ref)

    def fwd_ctx(i, carry):
        n = nb_ctx * c
        accc_ref[pl.ds(pl.multiple_of(i * n, n), n), :] = block(ctx, i, nb_ctx, 0, False)
        return carry

    def fwd_lat(i, carry):
        n = nb_lat * c
        acc_ref[pl.ds(pl.multiple_of(i * n, n), n), :] = block(lat, i, nb_lat, 0, True)
        return carry

    def bwd_ctx(j, carry):
        i = n_ctx // nb_ctx - 1 - j
        n = nb_ctx * c
        rows = pl.ds(pl.multiple_of(i * n, n), n)
        oc_ref[rows, :] = finish(accc_ref[rows, :] + block(ctx, i, nb_ctx, 1, False), gc_ref[rows, :])
        return carry

    def bwd_lat(j, carry):
        i = n_lat // nb_lat - 1 - j
        n = nb_lat * c
        rows = pl.ds(pl.multiple_of(i * n, n), n)
        o_ref[rows, :] = finish(acc_ref[rows, :] + block(lat, i, nb_lat, 1, True), g_ref[rows, :])
        return carry

    s_ref[...] = jnp.zeros_like(s_ref)
    lax.fori_loop(0, n_ctx // nb_ctx, fwd_ctx, 0)
    lax.fori_loop(0, n_lat // nb_lat, fwd_lat, 0)
    s_ref[...] = jnp.zeros_like(s_ref)
    lax.fori_loop(0, n_ctx // nb_ctx, bwd_ctx, 0)
    lax.fori_loop(0, n_lat // nb_lat, bwd_lat, 0)


def _gla(proj, lr, wg_pad, bg, norm_w, n_batch, seq, ctx_len, n_heads, dk, dv, col_q):
    n_lat = seq // GLA_CHUNK
    n_ctx = ctx_len // GLA_CHUNK
    assert dk == LANES and dv % dk == 0
    assert n_lat % min(GLA_BLOCK_CHUNKS, n_lat) == 0 and n_ctx % min(GLA_BLOCK_CHUNKS, n_ctx) == 0
    cb = n_batch * seq // ctx_len
    vq = dv // dk
    col_v = (col_q + 2 * n_heads) // vq
    col_g = col_v + n_heads
    cos, sin = _rope_tables(seq, dk)
    nblk = GLA_BLOCK_CHUNKS * GLA_CHUNK
    r = jnp.arange(nblk)
    same_chunk = (r[:, None] // GLA_CHUNK) == (r[None, :] // GLA_CHUNK)
    keep = jnp.stack([same_chunk & (r[:, None] >= r[None, :]), same_chunk & (r[:, None] <= r[None, :])])
    keep = keep.astype(F32)
    vmask = ((r[:, None] // GLA_CHUNK) == (jnp.arange(GLA_BLOCK_CHUNKS * dv)[None, :] // dv)).astype(BF16)

    def spec(rows, width, row_block, col):
        return pl.BlockSpec((rows, width), lambda b, h: (row_block(b), col(h)))
    lat_rb = lambda b: b
    ctx_rb = lambda b: cb + b
    in_specs = []
    for rows, rb in ((seq, lat_rb), (ctx_len, ctx_rb)):
        in_specs += [spec(rows, dk, rb, lambda h: col_q + h),
                     spec(rows, dk, rb, lambda h: col_q + n_heads + h),
                     spec(rows, dv, rb, lambda h: col_v + h),
                     spec(rows, dv, rb, lambda h: col_g + h),
                     spec(rows, LANES, rb, lambda h: 0)]
    in_specs += [pl.BlockSpec((2, LANES, dk), lambda b, h: (0, 0, h)),
                 pl.BlockSpec((2, dk), lambda b, h: (0, h)),
                 pl.BlockSpec((1, dv), lambda b, h: (0, 0)),
                 pl.BlockSpec((seq, dk), lambda b, h: (0, 0)),
                 pl.BlockSpec((seq, dk), lambda b, h: (0, 0)),
                 pl.BlockSpec((2, nblk, nblk), lambda b, h: (0, 0, 0)),
                 pl.BlockSpec((2, nblk, nblk), lambda b, h: (0, 0, 0)),
                 pl.BlockSpec((nblk, GLA_BLOCK_CHUNKS * dv), lambda b, h: (0, 0))]
    return pl.pallas_call(
        functools.partial(_gla_kernel, n_lat=n_lat, n_ctx=n_ctx),
        grid=(n_batch, n_heads), in_specs=in_specs,
        out_specs=[pl.BlockSpec((seq, dv), lambda b, h: (b, h)),
                   pl.BlockSpec((ctx_len, dv), lambda b, h: (b, h))],
        out_shape=[jax.ShapeDtypeStruct((n_batch * seq, n_heads * dv), BF16),
                   jax.ShapeDtypeStruct((n_batch * ctx_len, n_heads * dv), BF16)],
        scratch_shapes=[pltpu.VMEM((dk, dv), F32), pltpu.VMEM((seq, dv), F32),
                        pltpu.VMEM((ctx_len, dv), F32)],
        compiler_params=_params("parallel", "parallel"), name="gla",
    )(proj, proj, proj, proj, lr, proj, proj, proj, proj, lr,
      wg_pad, bg, norm_w.reshape(1, dv), cos, sin, keep.astype(BF16), keep, vmask)


def _lru_pitch(n):
    seg = n // SUBLANES
    assert seg % SUBLANES == 0
    return seg + 4


def _lru_kernel(x_ref, g_ref, xc_ref, gc_ref, cw_ref, cb_ref, w4_ref, b4_ref, lam_ref, o_ref, oc_ref,
                xpad_ref, af_ref, cf_ref, ab_ref, cbk_ref, hf_ref, pf_ref, hb_ref, pb_ref, *, seq, ctx_len):
    bw = x_ref.shape[1]
    pad = SUBLANES
    sp = _softplus(-lam_ref[...])
    cw = cw_ref[...]
    w4 = w4_ref[...]
    b4 = b4_ref[...]

    def put(ref, n, t0, val):
        seg, pitch = n // SUBLANES, _lru_pitch(n)
        for s in range(SUBLANES):
            lo, hi = max(t0, s * seg), min(t0 + val.shape[0], (s + 1) * seg)
            if lo < hi:
                ref[s * pitch + lo - s * seg:s * pitch + hi - s * seg, :] = val[lo - t0:hi - t0, :]

    def coeffs(src_ref, n):
        xpad_ref[0:pad, :] = jnp.zeros((pad, bw), F32)
        xpad_ref[pad + n:pad + n + pad, :] = jnp.zeros((pad, bw), F32)
        tile = min(n, 512)
        for t0 in range(0, n, tile):
            xpad_ref[pad + t0:pad + t0 + tile, :] = src_ref[t0:t0 + tile, :].astype(F32)
        for t0 in range(0, n, tile):
            xc = cb_ref[...]
            for j in range(LRU_CONV):
                lo = pad + t0 + j - LRU_CONV // 2
                xc = xc + cw[j:j + 1, :] * xpad_ref[lo:lo + tile, :]
            z = _dot(xc.astype(BF16), w4) + b4
            for d, (a_ref, c_ref) in enumerate(((af_ref, cf_ref), (ab_ref, cbk_ref))):
                r = jax.nn.sigmoid(z[:, (2 * d) * bw:(2 * d + 1) * bw])
                i = jax.nn.sigmoid(z[:, (2 * d + 1) * bw:(2 * d + 2) * bw])
                log_a = (-LRU_C) * r * sp[d:d + 1, :]
                a = jnp.exp(log_a)
                put(a_ref, n, t0, a)
                put(c_ref, n, t0, jnp.sqrt(1.0 - a * a) * (i * xc))

    def scan(n, h0_f, h0_b):
        seg, pitch = n // SUBLANES, _lru_pitch(n)

        def step(g, carry):
            hf, pf, hb, pb = carry
            fwd = pl.ds(g, SUBLANES, stride=pitch)
            bwd = pl.ds(seg - 1 - g, SUBLANES, stride=pitch)
            a = af_ref[fwd, :]
            hf = a * hf + cf_ref[fwd, :]
            pf = a * pf
            hf_ref[fwd, :] = hf
            pf_ref[fwd, :] = pf
            a = ab_ref[bwd, :]
            hb = a * hb + cbk_ref[bwd, :]
            pb = a * pb
            hb_ref[bwd, :] = hb
            pb_ref[bwd, :] = pb
            return hf, pf, hb, pb

        zero = jnp.zeros((SUBLANES, bw), F32)
        one = jnp.ones((SUBLANES, bw), F32)
        hf, pf, hb, pb = lax.fori_loop(0, seg, step, (zero, one, zero, one), unroll=8)
        carry_f = [h0_f]
        for s in range(SUBLANES):
            carry_f.append(pf[s:s + 1, :] * carry_f[s] + hf[s:s + 1, :])
        carry_b = [h0_b]
        for s in range(SUBLANES - 1, -1, -1):
            carry_b.append(pb[s:s + 1, :] * carry_b[-1] + hb[s:s + 1, :])
        carry_b = carry_b[::-1]
        return carry_f, carry_b

    def emit(n, carry_f, carry_b, gate_ref, out_ref):
        seg, pitch = n // SUBLANES, _lru_pitch(n)
        for s in range(SUBLANES):
            rows = slice(s * seg, (s + 1) * seg)
            held = slice(s * pitch, s * pitch + seg)
            h = (hf_ref[held, :] + pf_ref[held, :] * carry_f[s]
                 + hb_ref[held, :] + pb_ref[held, :] * carry_b[s + 1])
            out_ref[rows, :] = (h * _gelu_tanh(gate_ref[rows, :].astype(F32))).astype(out_ref.dtype)

    zero_h = jnp.zeros((1, bw), F32)
    coeffs(xc_ref, ctx_len)
    cf, cbw = scan(ctx_len, zero_h, zero_h)
    emit(ctx_len, cf, cbw, gc_ref, oc_ref)
    coeffs(x_ref, seq)
    lf, lb = scan(seq, cf[SUBLANES], cbw[0])
    emit(seq, lf, lb, g_ref, o_ref)


def _rglru(proj, conv_w, conv_b, w4, b4, lam, n_batch, seq, ctx_len, col_x):
    n_blk = w4.shape[0]
    bw = LANES
    cb = n_batch * seq // ctx_len
    width = n_blk * bw
    return pl.pallas_call(
        functools.partial(_lru_kernel, seq=seq, ctx_len=ctx_len),
        grid=(n_batch, n_blk),
        in_specs=[pl.BlockSpec((seq, bw), lambda b, j: (b, col_x + j)),
                  pl.BlockSpec((seq, bw), lambda b, j: (b, col_x + n_blk + j)),
                  pl.BlockSpec((ctx_len, bw), lambda b, j: (cb + b, col_x + j)),
                  pl.BlockSpec((ctx_len, bw), lambda b, j: (cb + b, col_x + n_blk + j)),
                  pl.BlockSpec((LRU_CONV, bw), lambda b, j: (0, j)),
                  pl.BlockSpec((1, bw), lambda b, j: (0, j)),
                  pl.BlockSpec((None, bw, 4 * bw), lambda b, j: (j, 0, 0)),
                  pl.BlockSpec((None, 1, 4 * bw), lambda b, j: (j, 0, 0)),
                  pl.BlockSpec((2, bw), lambda b, j: (0, j))],
        out_specs=[pl.BlockSpec((seq, bw), lambda b, j: (b, j)),
                   pl.BlockSpec((ctx_len, bw), lambda b, j: (b, j))],
        out_shape=[jax.ShapeDtypeStruct((n_batch * seq, width), BF16),
                   jax.ShapeDtypeStruct((n_batch * ctx_len, width), BF16)],
        scratch_shapes=([pltpu.VMEM((seq + 2 * SUBLANES, bw), F32)]
                        + [pltpu.VMEM((SUBLANES * _lru_pitch(seq), bw), F32)] * 8),
        compiler_params=_params("parallel", "parallel"), name="rglru",
    )(proj, proj, proj, proj, conv_w, conv_b.reshape(1, width), w4, b4, lam)


def _row_tile(n_rows, cap):
    t = cap
    while n_rows % t:
        t //= 2
    return t


def kernel(x, c, ctx, c_ctx, w_mod, b_mod, norm_mix, norm_ffn, w_in, na_rpb, gla_wg, gla_bg, gla_norm,
           conv_w, conv_b, lru_wa, lru_ba, lru_wx, lru_bx, lru_lam, w_out, ffd_w1, ffd_w3, ffd_w2,
           router, moe_w1, moe_w3, moe_w2, final_norm):
    n_batch, seq, d = x.shape
    ctx_len = ctx.shape[1]
    depth = w_in.shape[0]
    n_lat = n_batch * seq
    n_tot = n_lat + n_batch * ctx_len
    na_heads = na_rpb.shape[1]
    na_w = na_heads * LANES
    gla_kw = gla_wg.shape[3]
    gla_heads = gla_kw // LANES
    gla_dv = gla_norm.shape[1]
    gla_vw = gla_heads * gla_dv
    lru_w = conv_w.shape[2]
    n_blk = lru_wa.shape[2]
    assert n_batch < SUBLANES and lru_w == n_blk * LANES and seq % (SUBLANES * GRID_W) == 0

    p_lr = 3 * na_w + 2 * gla_kw + 2 * gla_vw
    col_gla = 3 * na_w // LANES
    col_lru = p_lr // LANES

    tm = _row_tile(n_tot, 1024)
    tm_lat = _row_tile(n_lat, 1024)
    tr = _row_tile(n_tot, 256)

    x_all = jnp.concatenate([x.reshape(n_lat, d), ctx.reshape(n_batch * ctx_len, d)], axis=0)
    c8 = jnp.zeros((SUBLANES, d), F32).at[:n_batch].set(c).at[n_batch].set(c_ctx)

    for l in range(depth):
        last = l == depth - 1
        rows_out = n_lat if last else n_tot
        tmo = tm_lat if last else tm
        mod3 = _modulation(c8, w_mod, b_mod, l).reshape(SUBLANES * N_MOD, 1, d)

        h = _norm_mod(x_all, norm_mix[l], mod3, 1, 0, n_tot, seq, n_batch, tr)
        w_l = w_in[l]
        w_main = jnp.concatenate([w_l[:, :p_lr], w_l[:, p_lr + 2 * GLA_LOWRANK:]], axis=1).astype(BF16)
        w_lr = jnp.pad(w_l[:, p_lr:p_lr + 2 * GLA_LOWRANK], ((0, 0), (0, LANES - 2 * GLA_LOWRANK))).astype(BF16)
        proj = _matmul(h, w_main, BF16, tm, _row_tile(w_main.shape[1], 512))
        lr = _matmul(h, w_lr, F32, tm, LANES)

        na_l, na_c = _neighbourhood_attention(proj, _na_bias_table(na_rpb[l]), n_batch, seq, ctx_len,
                                              na_heads, 0)
        wg = gla_wg[l]
        wg_pad = jnp.zeros((2, LANES, gla_kw), F32)
        wg_pad = wg_pad.at[0, :GLA_LOWRANK].set(wg[0]).at[1, GLA_LOWRANK:2 * GLA_LOWRANK].set(wg[1])
        gla_l, gla_c = _gla(proj, lr, wg_pad.astype(BF16), gla_bg[l], gla_norm[l], n_batch, seq, ctx_len,
                            gla_heads, LANES, gla_dv, col_gla)
        w4 = jnp.concatenate([lru_wa[l, 0], lru_wx[l, 0], lru_wa[l, 1], lru_wx[l, 1]], axis=-1).astype(BF16)
        b4 = jnp.concatenate([lru_ba[l, 0].reshape(n_blk, 1, LANES), lru_bx[l, 0].reshape(n_blk, 1, LANES),
                              lru_ba[l, 1].reshape(n_blk, 1, LANES), lru_bx[l, 1].reshape(n_blk, 1, LANES)],
                             axis=-1)
        lru_l, lru_c = _rglru(proj, conv_w[l], conv_b[l], w4, b4, lru_lam[l], n_batch, seq, ctx_len, col_lru)

        mix = jnp.concatenate([na_l, gla_l, lru_l], axis=1)
        if not last:
            mix = jnp.concatenate([mix, jnp.concatenate([na_c, gla_c, lru_c], axis=1)], axis=0)
        x_all = _matmul_residual(mix, w_out[l].astype(BF16), x_all, mod3, 2, rows_out, seq, n_batch,
                                 tmo, _row_tile(d, 1024), _row_tile(mix.shape[1], 2048))

        j = l // 2
        if l % 2 == 0:
            h = _norm_mod(x_all, norm_ffn[l], mod3, 4, 3, rows_out, seq, n_batch, tr)
            act = _swiglu_in(h, ffd_w1[j].astype(BF16), ffd_w3[j].astype(BF16), tmo,
                             _row_tile(ffd_w1.shape[2], 512))
            x_all = _matmul_residual(act, ffd_w2[j].astype(BF16), x_all, mod3, 5, rows_out, seq, n_batch, tmo,
                                     _row_tile(d, 1024), _row_tile(act.shape[1], 2048))
        else:
            h, route = _norm_mod(x_all, norm_ffn[l], mod3, 4, 3, rows_out, seq, n_batch, tr, router=router[j])
            x_all = _moe(h, route, moe_w1[j].astype(BF16), moe_w3[j].astype(BF16), moe_w2[j].astype(BF16),
                         x_all, mod3, 5, rows_out, seq, n_batch, tn=_row_tile(moe_w1.shape[3], 512),
                         tc=_row_tile(rows_out, 256))

    out = _final_norm(x_all, final_norm, n_lat, tr)
    return out.reshape(n_batch, seq, d)
```

```python
import functools

import jax
import jax.numpy as jnp
from jax import lax
from jax.experimental import pallas as pl
from jax.experimental.pallas import tpu as pltpu

F32 = jnp.float32
BF16 = jnp.bfloat16

EPS = 1e-6
ROPE_BASE = 10000.0
GRID_W = 64
NA_WIN_ROWS = 8
NA_WIN_COLS = 16
GLA_CHUNK = 64
GLA_LOWRANK = 16
GLA_NORMALIZER = 16.0
LRU_C = 8.0
LRU_CONV = 4
N_MOD = 6
TOP_K = 2
LANES = 128
SUBLANES = 8
VMEM_LIMIT = 56 * 1024 * 1024
NEG_BIG = -1e30

_NT = (((1,), (1,)), ((), ()))


def _params(*sem):
    return pltpu.CompilerParams(dimension_semantics=sem, vmem_limit_bytes=VMEM_LIMIT)


def _dot(a, b):
    return jnp.dot(a, b, preferred_element_type=F32)


def _split_bf16(x):
    hi = x.astype(BF16)
    lo = (x - hi.astype(F32)).astype(BF16)
    return hi, lo


def _softplus(x):
    return jnp.maximum(x, 0.0) + jnp.log1p(jnp.exp(-jnp.abs(x)))


def _silu(x):
    return x * jax.nn.sigmoid(x)


def _gelu_tanh(x):
    return 0.5 * x * (1.0 + jnp.tanh(0.7978845608028654 * (x + 0.044715 * (x * x * x))))


def _mod_kernel(c_ref, w_ref, b_ref, o_ref):
    c = c_ref[...]
    a_hi, a_lo = _split_bf16(_silu(c))
    w_hi, w_lo = _split_bf16(w_ref[...])
    o_ref[...] = _dot(a_hi, w_hi) + _dot(a_lo, w_hi) + _dot(a_hi, w_lo) + b_ref[...]


def _modulation(c8, w_mod, b_mod, layer, tn=512):
    d = c8.shape[1]
    n = w_mod.shape[2]
    return pl.pallas_call(
        _mod_kernel,
        grid=(n // tn,),
        in_specs=[pl.BlockSpec((SUBLANES, d), lambda j: (0, 0)),
                  pl.BlockSpec((None, d, tn), lambda j: (layer, 0, j)),
                  pl.BlockSpec((None, 1, tn), lambda j: (layer, 0, j))],
        out_specs=pl.BlockSpec((SUBLANES, tn), lambda j: (0, j)),
        out_shape=jax.ShapeDtypeStruct((SUBLANES, n), F32),
        compiler_params=_params("parallel"),
        name="modulation",
    )(c8, w_mod, b_mod.reshape(b_mod.shape[0], 1, n))


def _mod_spec(which, tile_rows, seq, n_batch, d_block, col_axis=None):
    def row(i):
        return jnp.minimum((i * tile_rows) // seq, n_batch) * N_MOD + which
    if col_axis is None:
        return pl.BlockSpec((None, 1, d_block), lambda i, *_: (row(i), 0, 0))
    return pl.BlockSpec((None, 1, d_block), lambda *g: (row(g[0]), 0, g[col_axis]))


def _norm_mod_body(x_ref, nw_ref, sc_ref, sh_ref):
    x = x_ref[...]
    y = x * lax.rsqrt(jnp.mean(x * x, axis=-1, keepdims=True) + EPS) * nw_ref[...]
    return y * (1.0 + sc_ref[...]) + sh_ref[...]


def _norm_mod_kernel(x_ref, nw_ref, sc_ref, sh_ref, o_ref):
    o_ref[...] = _norm_mod_body(x_ref, nw_ref, sc_ref, sh_ref).astype(o_ref.dtype)


def _norm_route_kernel(x_ref, nw_ref, sc_ref, sh_ref, rhi_ref, rlo_ref, o_ref, g_ref, *, n_experts):
    h = _norm_mod_body(x_ref, nw_ref, sc_ref, sh_ref)
    o_ref[...] = h.astype(o_ref.dtype)
    h_hi, h_lo = _split_bf16(h)
    logits = _dot(h_hi, rhi_ref[...]) + _dot(h_lo, rhi_ref[...]) + _dot(h_hi, rlo_ref[...])
    lane = lax.broadcasted_iota(jnp.int32, logits.shape, 1)
    valid = lane < n_experts
    logits = jnp.where(valid, logits, NEG_BIG)
    e = jnp.exp(logits - jnp.max(logits, axis=-1, keepdims=True))
    probs = jnp.where(valid, e / jnp.sum(e, axis=-1, keepdims=True), -1.0)
    p1 = jnp.max(probs, axis=-1, keepdims=True)
    i1 = jnp.min(jnp.where(probs == p1, lane, LANES), axis=-1, keepdims=True)
    rest = jnp.where(lane == i1, -1.0, probs)
    p2 = jnp.max(rest, axis=-1, keepdims=True)
    i2 = jnp.min(jnp.where(rest == p2, lane, LANES), axis=-1, keepdims=True)
    den = p1 + p2
    g_ref[...] = (jnp.where(lane == 0, p1 / den, 0.0) + jnp.where(lane == 1, p2 / den, 0.0)
                  + jnp.where(lane == 2, i1.astype(F32), 0.0) + jnp.where(lane == 3, i2.astype(F32), 0.0))


def _norm_mod(x_all, norm_w, mod3, which_scale, which_shift, n_rows, seq, n_batch, tr=256,
              router=None):
    d = x_all.shape[1]
    in_specs = [pl.BlockSpec((tr, d), lambda i: (i, 0)),
                pl.BlockSpec((1, d), lambda i: (0, 0)),
                _mod_spec(which_scale, tr, seq, n_batch, d),
                _mod_spec(which_shift, tr, seq, n_batch, d)]
    h_spec = pl.BlockSpec((tr, d), lambda i: (i, 0))
    h_shape = jax.ShapeDtypeStruct((n_rows, d), BF16)
    if router is None:
        return pl.pallas_call(
            _norm_mod_kernel, grid=(n_rows // tr,), in_specs=in_specs, out_specs=h_spec,
            out_shape=h_shape, compiler_params=_params("parallel"), name="norm_mod",
        )(x_all, norm_w.reshape(1, d), mod3, mod3)
    n_experts = router.shape[1]
    r_pad = jnp.pad(router, ((0, 0), (0, LANES - n_experts)))
    r_hi = r_pad.astype(BF16)
    r_lo = (r_pad - r_hi.astype(F32)).astype(BF16)
    w_spec = pl.BlockSpec((d, LANES), lambda i: (0, 0))
    return pl.pallas_call(
        functools.partial(_norm_route_kernel, n_experts=n_experts),
        grid=(n_rows // tr,), in_specs=in_specs + [w_spec, w_spec],
        out_specs=[h_spec, pl.BlockSpec((tr, LANES), lambda i: (i, 0))],
        out_shape=[jax.ShapeDtypeStruct((n_rows, d), F32), jax.ShapeDtypeStruct((n_rows, LANES), F32)],
        compiler_params=_params("parallel"), name="norm_route",
    )(x_all, norm_w.reshape(1, d), mod3, mod3, r_hi, r_lo)


def _final_norm_kernel(x_ref, nw_ref, o_ref):
    x = x_ref[...]
    o_ref[...] = x * lax.rsqrt(jnp.mean(x * x, axis=-1, keepdims=True) + EPS) * nw_ref[...]


def _final_norm(x_all, norm_w, n_rows, tr=256):
    d = x_all.shape[1]
    return pl.pallas_call(
        _final_norm_kernel, grid=(n_rows // tr,),
        in_specs=[pl.BlockSpec((tr, d), lambda i: (i, 0)), pl.BlockSpec((1, d), lambda i: (0, 0))],
        out_specs=pl.BlockSpec((tr, d), lambda i: (i, 0)),
        out_shape=jax.ShapeDtypeStruct((n_rows, d), F32),
        compiler_params=_params("parallel"), name="final_norm",
    )(x_all, norm_w.reshape(1, d))


def _mm_kernel(a_ref, w_ref, o_ref):
    o_ref[...] = _dot(a_ref[...], w_ref[...]).astype(o_ref.dtype)


def _matmul(a, w, out_dtype, tm, tn):
    m, k = a.shape
    n = w.shape[1]
    return pl.pallas_call(
        _mm_kernel, grid=(m // tm, n // tn),
        in_specs=[pl.BlockSpec((tm, k), lambda i, j: (i, 0)),
                  pl.BlockSpec((k, tn), lambda i, j: (0, j))],
        out_specs=pl.BlockSpec((tm, tn), lambda i, j: (i, j)),
        out_shape=jax.ShapeDtypeStruct((m, n), out_dtype),
        compiler_params=_params("parallel", "parallel"), name="matmul",
    )(a, w)


def _mm_res_kernel(a_ref, w_ref, x_ref, g_ref, o_ref, acc_ref, *, nk):
    k = pl.program_id(2)

    @pl.when(k == 0)
    def _():
        acc_ref[...] = jnp.zeros_like(acc_ref)

    acc_ref[...] += _dot(a_ref[...], w_ref[...])

    @pl.when(k == nk - 1)
    def _():
        o_ref[...] = x_ref[...] + g_ref[...] * acc_ref[...]


def _matmul_residual(a, w, x_all, mod3, which_gate, n_rows, seq, n_batch, tm, tn, tk):
    k = a.shape[1]
    d = w.shape[1]
    nk = k // tk
    return pl.pallas_call(
        functools.partial(_mm_res_kernel, nk=nk), grid=(n_rows // tm, d // tn, nk),
        in_specs=[pl.BlockSpec((tm, tk), lambda i, j, kk: (i, kk)),
                  pl.BlockSpec((tk, tn), lambda i, j, kk: (kk, j)),
                  pl.BlockSpec((tm, tn), lambda i, j, kk: (i, j)),
                  _mod_spec(which_gate, tm, seq, n_batch, tn, col_axis=1)],
        out_specs=pl.BlockSpec((tm, tn), lambda i, j, kk: (i, j)),
        out_shape=jax.ShapeDtypeStruct((n_rows, d), F32),
        scratch_shapes=[pltpu.VMEM((tm, tn), F32)],
        compiler_params=_params("parallel", "parallel", "arbitrary"), name="matmul_residual",
    )(a, w, x_all, mod3)


def _swiglu_kernel(a_ref, w1_ref, w3_ref, o_ref):
    a = a_ref[...]
    o_ref[...] = (_silu(_dot(a, w1_ref[...])) * _dot(a, w3_ref[...])).astype(o_ref.dtype)


def _swiglu_in(a, w1, w3, tm, tn):
    m, k = a.shape
    f = w1.shape[1]
    return pl.pallas_call(
        _swiglu_kernel, grid=(m // tm, f // tn),
        in_specs=[pl.BlockSpec((tm, k), lambda i, j: (i, 0)),
                  pl.BlockSpec((k, tn), lambda i, j: (0, j)),
                  pl.BlockSpec((k, tn), lambda i, j: (0, j))],
        out_specs=pl.BlockSpec((tm, tn), lambda i, j: (i, j)),
        out_shape=jax.ShapeDtypeStruct((m, f), BF16),
        compiler_params=_params("parallel", "parallel"), name="swiglu_in",
    )(a, w1, w3)


MOE_TILE = 512


def _moe_dispatch(route, n_experts, tile):
    n = route.shape[0]
    e = jnp.concatenate([route[:, 2], route[:, 3]]).astype(jnp.int32)
    onehot = (e[:, None] == jnp.arange(n_experts, dtype=jnp.int32)[None, :]).astype(jnp.int32)
    csum = jnp.cumsum(onehot, axis=0)
    counts = csum[-1]
    rank = jnp.sum((csum - onehot) * onehot, axis=1)
    padded = (counts + tile - 1) // tile * tile
    ends = jnp.cumsum(padded)
    pos = jnp.sum(onehot * (ends - padded)[None, :], axis=1) + rank
    n_tiles = (TOP_K * n) // tile + n_experts
    tok = jnp.tile(jnp.arange(n, dtype=jnp.int32), TOP_K)
    src = jnp.zeros((n_tiles * tile,), jnp.int32).at[pos].set(tok, unique_indices=True)
    tile_id = jnp.arange(n_tiles, dtype=jnp.int32)
    tile_expert = jnp.sum((tile_id[:, None] * tile >= ends[None, :]).astype(jnp.int32), axis=1)
    n_used = ends[-1] // tile
    tile_expert = jnp.minimum(tile_expert, jnp.max(jnp.where(tile_id < n_used, tile_expert, 0)))
    return src.reshape(n_tiles, 1, tile), pos, tile_expert.astype(jnp.int32), n_used.reshape(1).astype(jnp.int32)


def _moe_in_kernel(te_ref, nu_ref, src_ref, nxt_ref, h_hbm, w1_ref, w3_ref, o_ref, rows_ref, a_ref, sem, *, tile):
    i = pl.program_id(0)
    j = pl.program_id(1)
    used = i < nu_ref[0]
    slot = i % 2

    def gather(idx_ref, s):
        def issue(p, carry):
            pltpu.make_async_copy(h_hbm.at[pl.ds(idx_ref[0, p], 1), :], rows_ref.at[s, pl.ds(p, 1), :],
                                  sem.at[s]).start()
            return carry
        lax.fori_loop(0, tile, issue, 0, unroll=8)

    @pl.when(jnp.logical_and(i == 0, j == 0))
    def _():
        gather(src_ref, 0)

    @pl.when(jnp.logical_and(i + 1 < nu_ref[0], j == 0))
    def _():
        gather(nxt_ref, 1 - slot)

    @pl.when(jnp.logical_and(used, j == 0))
    def _():
        pltpu.make_async_copy(h_hbm.at[pl.ds(0, tile), :], rows_ref.at[slot], sem.at[slot]).wait()
        a_ref[...] = rows_ref[slot].astype(BF16)

    @pl.when(used)
    def _():
        a = a_ref[...]
        o_ref[...] = (_silu(_dot(a, w1_ref[...])) * _dot(a, w3_ref[...])).astype(o_ref.dtype)

    @pl.when(jnp.logical_not(used))
    def _():
        o_ref[...] = jnp.zeros_like(o_ref)


def _moe_out_kernel(te_ref, nu_ref, a_ref, w_ref, o_ref):
    used = pl.program_id(0) < nu_ref[0]

    @pl.when(used)
    def _():
        o_ref[...] = _dot(a_ref[...], w_ref[...])

    @pl.when(jnp.logical_not(used))
    def _():
        o_ref[...] = jnp.zeros_like(o_ref)


def _moe_combine_kernel(pos_ref, y_hbm, x_ref, g_ref, route_ref, *rest, tc, normed):
    o_ref, rows_ref, sem = rest[-3:]

    def issue(t, carry):
        for s in range(TOP_K):
            pltpu.make_async_copy(y_hbm.at[pl.ds(pos_ref[0, s * tc + t], 1), :],
                                  rows_ref.at[s, pl.ds(t, 1), :], sem).start()
        return carry
    lax.fori_loop(0, tc, issue, 0, unroll=8)
    for s in range(TOP_K):
        pltpu.make_async_copy(y_hbm.at[pl.ds(0, tc), :], rows_ref.at[s], sem).wait()
    route = route_ref[...]
    moe = route[:, 0:1] * rows_ref[0] + route[:, 1:2] * rows_ref[1]
    o = x_ref[...] + g_ref[...] * moe
    if normed:
        o = o * lax.rsqrt(jnp.mean(o * o, axis=-1, keepdims=True) + EPS) * rest[0][...]
    o_ref[...] = o


def _moe(h, route, w1, w3, w2, x_all, mod3, which_gate, n_rows, seq, n_batch, tn=512, tc=256, out_norm_w=None):
    n, d = h.shape
    n_experts, _, f = w1.shape
    tile = MOE_TILE
    src, pos, tile_expert, n_used = _moe_dispatch(route, n_experts, tile)
    n_tiles = src.shape[0]
    smem = functools.partial(pl.BlockSpec, memory_space=pltpu.SMEM)
    act = pl.pallas_call(
        functools.partial(_moe_in_kernel, tile=tile),
        grid_spec=pltpu.PrefetchScalarGridSpec(
            num_scalar_prefetch=2, grid=(n_tiles, f // tn),
            in_specs=[smem((None, 1, tile), lambda i, j, te, nu: (i, 0, 0)),
                      smem((None, 1, tile), lambda i, j, te, nu: (jnp.minimum(i + 1, n_tiles - 1), 0, 0)),
                      pl.BlockSpec(memory_space=pl.ANY),
                      pl.BlockSpec((None, d, tn), lambda i, j, te, nu: (te[i], 0, j)),
                      pl.BlockSpec((None, d, tn), lambda i, j, te, nu: (te[i], 0, j))],
            out_specs=pl.BlockSpec((tile, tn), lambda i, j, te, nu: (i, j)),
            scratch_shapes=[pltpu.VMEM((2, tile, d), F32), pltpu.VMEM((tile, d), BF16),
                            pltpu.SemaphoreType.DMA((2,))]),
        out_shape=jax.ShapeDtypeStruct((n_tiles * tile, f), BF16),
        compiler_params=_params("arbitrary", "arbitrary"), name="moe_in",
    )(tile_expert, n_used, src, src, h, w1, w3)
    tn2 = _row_tile(d, 1024)
    y = pl.pallas_call(
        _moe_out_kernel,
        grid_spec=pltpu.PrefetchScalarGridSpec(
            num_scalar_prefetch=2, grid=(n_tiles, d // tn2),
            in_specs=[pl.BlockSpec((tile, f), lambda i, j, te, nu: (i, 0)),
                      pl.BlockSpec((None, f, tn2), lambda i, j, te, nu: (te[i], 0, j))],
            out_specs=pl.BlockSpec((tile, tn2), lambda i, j, te, nu: (i, j))),
        out_shape=jax.ShapeDtypeStruct((n_tiles * tile, d), F32),
        compiler_params=_params("arbitrary", "arbitrary"), name="moe_out",
    )(tile_expert, n_used, act, w2)
    pos2 = pos.reshape(TOP_K, n // tc, tc).transpose(1, 0, 2).reshape(n // tc, 1, TOP_K * tc)
    normed = out_norm_w is not None
    norm_specs = [pl.BlockSpec((1, d), lambda i: (0, 0))] if normed else []
    norm_args = [out_norm_w.reshape(1, d)] if normed else []
    return pl.pallas_call(
        functools.partial(_moe_combine_kernel, tc=tc, normed=normed),
        grid=(n_rows // tc,),
        in_specs=[smem((None, 1, TOP_K * tc), lambda i: (i, 0, 0)),
                  pl.BlockSpec(memory_space=pl.ANY),
                  pl.BlockSpec((tc, d), lambda i: (i, 0)),
                  _mod_spec(which_gate, tc, seq, n_batch, d),
                  pl.BlockSpec((tc, LANES), lambda i: (i, 0))] + norm_specs,
        out_specs=pl.BlockSpec((tc, d), lambda i: (i, 0)),
        out_shape=jax.ShapeDtypeStruct((n_rows, d), F32),
        scratch_shapes=[pltpu.VMEM((TOP_K, tc, d), F32), pltpu.SemaphoreType.DMA(())],
        compiler_params=_params("arbitrary"), name="moe_combine",
    )(pos2, y, x_all, mod3, route, *norm_args)


def _na_bias_table(rpb):
    n_heads = rpb.shape[0]
    n_dr = 2 * NA_WIN_ROWS - 1
    n_dc = 2 * NA_WIN_COLS - 1
    cols = jnp.arange(GRID_W)
    col_start = jnp.clip(cols - NA_WIN_COLS // 2, 0, GRID_W - NA_WIN_COLS)
    in_win = (cols[None, :] >= col_start[:, None]) & (cols[None, :] < col_start[:, None] + NA_WIN_COLS)
    dc = cols[None, :] - cols[:, None] + NA_WIN_COLS - 1
    expand = ((dc[None] == jnp.arange(n_dc)[:, None, None]) & in_win[None]).astype(F32)
    t = jnp.dot(rpb.reshape(-1, n_dc), expand.reshape(n_dc, GRID_W * GRID_W), precision=lax.Precision.HIGHEST)
    t = jnp.where(in_win.reshape(1, 1, -1), t.reshape(n_heads, n_dr, -1), NEG_BIG)
    t = t.reshape(n_heads, n_dr, GRID_W, GRID_W)
    masked = jnp.full((n_heads, GRID_W, GRID_W), NEG_BIG, F32)
    kinds = []
    for kind in range(3):
        q_rows = []
        for i in range(NA_QROWS):
            lo, dr0 = ((0, NA_WIN_ROWS - 1 - i), (i, NA_WIN_ROWS // 2 - 1),
                       (NA_QROWS, NA_WIN_ROWS - 1 - NA_QROWS - i))[kind]
            pieces = [t[:, dr0 + j - lo] if lo <= j < lo + NA_WIN_ROWS else masked for j in range(NA_KROWS)]
            q_rows.append(jnp.concatenate(pieces, axis=-1))
        kinds.append(jnp.concatenate(q_rows, axis=1))
    return jnp.stack(kinds, axis=1)


NA_QROWS = 4
NA_KROWS = NA_WIN_ROWS + NA_QROWS


def _na_kernel(q_ref, k_ref, v_ref, qc_ref, kc_ref, vc_ref, bias_ref, o_ref, oc_ref, *, rows, scale):
    kc = kc_ref[...]
    vc = vc_ref[...]
    nq = NA_QROWS * GRID_W
    nk = NA_KROWS * GRID_W
    n_blocks = rows // NA_QROWS

    def block(i, carry):
        base = jnp.clip(i * NA_QROWS - NA_WIN_ROWS // 2, 0, rows - NA_KROWS)
        kind = jnp.where(i == 0, 0, jnp.where(i == n_blocks - 1, 2, 1))
        q0 = pl.multiple_of(i * nq, nq)
        k0 = pl.multiple_of(base * GRID_W, nq)
        q = q_ref[pl.ds(q0, nq), :]
        kw = k_ref[pl.ds(k0, nk), :]
        vw = v_ref[pl.ds(k0, nk), :]
        s_w = lax.dot_general(q, kw, _NT, preferred_element_type=F32) * scale + bias_ref[kind]
        s_c = lax.dot_general(q, kc, _NT, preferred_element_type=F32) * scale
        m = jnp.maximum(jnp.max(s_w, axis=-1, keepdims=True), jnp.max(s_c, axis=-1, keepdims=True))
        p_w = jnp.exp(s_w - m)
        p_c = jnp.exp(s_c - m)
        den = jnp.sum(p_w, axis=-1, keepdims=True) + jnp.sum(p_c, axis=-1, keepdims=True)
        o = _dot(p_w.astype(BF16), vw) + _dot(p_c.astype(BF16), vc)
        o_ref[pl.ds(q0, nq), :] = (o / den).astype(o_ref.dtype)
        return carry

    lax.fori_loop(0, n_blocks, block, 0, unroll=2)

    s = lax.dot_general(qc_ref[...], kc, _NT, preferred_element_type=F32) * scale
    p = jnp.exp(s - jnp.max(s, axis=-1, keepdims=True))
    o = _dot(p.astype(BF16), vc) / jnp.sum(p, axis=-1, keepdims=True)
    oc_ref[...] = o.astype(oc_ref.dtype)


def _neighbourhood_attention(proj, bias_tab, n_batch, seq, ctx_len, n_heads, col0):
    dh = LANES
    rows = seq // GRID_W
    cb = n_batch * seq // ctx_len
    lat = lambda part: pl.BlockSpec((seq, dh), lambda b, h: (b, col0 + part * n_heads + h))
    ctx = lambda part: pl.BlockSpec((ctx_len, dh), lambda b, h: (cb + b, col0 + part * n_heads + h))
    return pl.pallas_call(
        functools.partial(_na_kernel, rows=rows, scale=dh ** -0.5),
        grid=(n_batch, n_heads),
        in_specs=[lat(0), lat(1), lat(2), ctx(0), ctx(1), ctx(2),
                  pl.BlockSpec((None, 3, NA_QROWS * GRID_W, NA_KROWS * GRID_W), lambda b, h: (h, 0, 0, 0))],
        out_specs=[pl.BlockSpec((seq, dh), lambda b, h: (b, h)),
                   pl.BlockSpec((ctx_len, dh), lambda b, h: (b, h))],
        out_shape=[jax.ShapeDtypeStruct((n_batch * seq, n_heads * dh), BF16),
                   jax.ShapeDtypeStruct((n_batch * ctx_len, n_heads * dh), BF16)],
        compiler_params=_params("parallel", "parallel"), name="neighbourhood_attention",
    )(proj, proj, proj, proj, proj, proj, bias_tab)


def _rope_tables(seq, dk):
    quarter = dk // 4
    inv_freq = ROPE_BASE ** (-jnp.arange(quarter, dtype=F32) / quarter)
    pos = jnp.arange(seq, dtype=jnp.int32)
    row = (pos // GRID_W).astype(F32)[:, None] * inv_freq
    col = (pos % GRID_W).astype(F32)[:, None] * inv_freq
    cos = jnp.concatenate([jnp.cos(row), jnp.cos(row), jnp.cos(col), jnp.cos(col)], axis=-1)
    sin = jnp.concatenate([-jnp.sin(row), jnp.sin(row), -jnp.sin(col), jnp.sin(col)], axis=-1)
    return cos, sin


GLA_BLOCK_CHUNKS = 4


def _gla_kernel(q_ref, k_ref, v_ref, g_ref, lr_ref, qc_ref, kc_ref, vc_ref, gc_ref, lrc_ref,
                wg_ref, bg_ref, nw_ref, cos_ref, sin_ref, tri_ref, keep_ref, vmask_ref, o_ref, oc_ref,
                s_ref, acc_ref, accc_ref, *, n_lat, n_ctx):
    dk = q_ref.shape[1]
    dv = v_ref.shape[1]
    c = GLA_CHUNK
    nb_lat = min(GLA_BLOCK_CHUNKS, n_lat)
    nb_ctx = min(GLA_BLOCK_CHUNKS, n_ctx)

    def swap_quarters(t):
        lane = lax.broadcasted_iota(jnp.int32, t.shape, 1)
        first_quarter = (lane % (dk // 2)) < (dk // 4)
        return jnp.where(first_quarter, pltpu.roll(t, dk - dk // 4, 1), pltpu.roll(t, dk // 4, 1))

    def block(refs, blk, nb, d, rope):
        qr, kr, vr, lrr = refs
        n = nb * c
        rows = pl.ds(pl.multiple_of(blk * n, n), n)
        q = qr[rows, :].astype(F32) * (dk ** -0.5)
        k = kr[rows, :].astype(F32)
        if rope:
            cos = cos_ref[rows, :]
            sin = sin_ref[rows, :]
            q = q * cos + swap_quarters(q) * sin
            k = k * cos + swap_quarters(k) * sin
        v = vr[rows, :]
        z = _dot(lrr[rows, :].astype(BF16), wg_ref[d]) + bg_ref[d:d + 1, :]
        logg = -_softplus(-z) * (1.0 / GLA_NORMALIZER)
        g1 = logg.astype(BF16)
        rem = logg - g1.astype(F32)
        g2 = rem.astype(BF16)
        g3 = (rem - g2.astype(F32)).astype(BF16)
        b3 = _dot(tri_ref[d, :n, :n], jnp.concatenate([g1, g2, g3], axis=1))
        b = b3[:, :dk] + b3[:, dk:2 * dk] + b3[:, 2 * dk:]
        end_row = c - 1 if d == 0 else 0
        ends = [b[m * c + end_row:m * c + end_row + 1, :] for m in range(nb)]
        b_end = jnp.concatenate([jnp.broadcast_to(e, (c, dk)) for e in ends], axis=0)
        q_dec = (q * jnp.exp(b)).astype(BF16)
        k_inv = (k * jnp.exp(-b)).astype(BF16)
        k_end_t = (k * jnp.exp(b_end - b)).T.astype(BF16)
        att = lax.dot_general(q_dec, k_inv, _NT, preferred_element_type=F32)
        att = jnp.where(keep_ref[d, :n, :n] != 0.0, att, 0.0).astype(BF16)
        o_intra = _dot(att, v)
        v_bd = jnp.concatenate([v] * nb, axis=1) * vmask_ref[:n, :nb * dv]
        d_all = _dot(k_end_t, v_bd)
        ends_t = jnp.concatenate(ends + [jnp.zeros((dk - nb, dk), F32)], axis=0).T
        state = s_ref[...]
        outs = [None] * nb
        for m in (range(nb) if d == 0 else range(nb - 1, -1, -1)):
            sl = slice(m * c, (m + 1) * c)
            outs[m] = o_intra[sl, :] + _dot(q_dec[sl, :], state.astype(BF16))
            decay = jnp.broadcast_to(jnp.exp(ends_t[:, m:m + 1]), state.shape)
            state = decay * state + d_all[:, m * dv:(m + 1) * dv]
        s_ref[...] = state
        return jnp.concatenate(outs, axis=0)

    def finish(o, g):
        y = o * lax.rsqrt(jnp.mean(o * o, axis=-1, keepdims=True) + EPS) * nw_ref[...]
        return (y * _silu(g.astype(F32))).astype(o_ref.dtype)

    lat = (q_ref, k_ref, v_ref, lr_ref)
    ctx = (qc_ref, kc_ref, vc_ref, lrc_ref)

    def fwd_ctx(i, carry):
        n = nb_ctx * c
        accc_ref[pl.ds(pl.multiple_of(i * n, n), n), :] = block(ctx, i, nb_ctx, 0, False)
        return carry

    def fwd_lat(i, carry):
        n = nb_lat * c
        acc_ref[pl.ds(pl.multiple_of(i * n, n), n), :] = block(lat, i, nb_lat, 0, True)
        return carry

    def bwd_ctx(j, carry):
        i = n_ctx // nb_ctx - 1 - j
        n = nb_ctx * c
        rows = pl.ds(pl.multiple_of(i * n, n), n)
        oc_ref[rows, :] = finish(accc_ref[rows, :] + block(ctx, i, nb_ctx, 1, False), gc_ref[rows, :])
        return carry

    def bwd_lat(j, carry):
        i = n_lat // nb_lat - 1 - j
        n = nb_lat * c
        rows = pl.ds(pl.multiple_of(i * n, n), n)
        o_ref[rows, :] = finish(acc_ref[rows, :] + block(lat, i, nb_lat, 1, True), g_ref[rows, :])
        return carry

    s_ref[...] = jnp.zeros_like(s_ref)
    lax.fori_loop(0, n_ctx // nb_ctx, fwd_ctx, 0)
    lax.fori_loop(0, n_lat // nb_lat, fwd_lat, 0, unroll=2)
    s_ref[...] = jnp.zeros_like(s_ref)
    lax.fori_loop(0, n_ctx // nb_ctx, bwd_ctx, 0)
    lax.fori_loop(0, n_lat // nb_lat, bwd_lat, 0, unroll=2)


def _gla(proj, lr, wg_pad, bg, norm_w, n_batch, seq, ctx_len, n_heads, dk, dv, col_q):
    n_lat = seq // GLA_CHUNK
    n_ctx = ctx_len // GLA_CHUNK
    assert dk == LANES and dv % dk == 0
    assert n_lat % min(GLA_BLOCK_CHUNKS, n_lat) == 0 and n_ctx % min(GLA_BLOCK_CHUNKS, n_ctx) == 0
    cb = n_batch * seq // ctx_len
    vq = dv // dk
    col_v = (col_q + 2 * n_heads) // vq
    col_g = col_v + n_heads
    cos, sin = _rope_tables(seq, dk)
    nblk = GLA_BLOCK_CHUNKS * GLA_CHUNK
    r = jnp.arange(nblk)
    same_chunk = (r[:, None] // GLA_CHUNK) == (r[None, :] // GLA_CHUNK)
    keep = jnp.stack([same_chunk & (r[:, None] >= r[None, :]), same_chunk & (r[:, None] <= r[None, :])])
    keep = keep.astype(F32)
    vmask = ((r[:, None] // GLA_CHUNK) == (jnp.arange(GLA_BLOCK_CHUNKS * dv)[None, :] // dv)).astype(BF16)

    def spec(rows, width, row_block, col):
        return pl.BlockSpec((rows, width), lambda b, h: (row_block(b), col(h)))
    lat_rb = lambda b: b
    ctx_rb = lambda b: cb + b
    in_specs = []
    for rows, rb in ((seq, lat_rb), (ctx_len, ctx_rb)):
        in_specs += [spec(rows, dk, rb, lambda h: col_q + h),
                     spec(rows, dk, rb, lambda h: col_q + n_heads + h),
                     spec(rows, dv, rb, lambda h: col_v + h),
                     spec(rows, dv, rb, lambda h: col_g + h),
                     spec(rows, LANES, rb, lambda h: 0)]
    in_specs += [pl.BlockSpec((2, LANES, dk), lambda b, h: (0, 0, h)),
                 pl.BlockSpec((2, dk), lambda b, h: (0, h)),
                 pl.BlockSpec((1, dv), lambda b, h: (0, 0)),
                 pl.BlockSpec((seq, dk), lambda b, h: (0, 0)),
                 pl.BlockSpec((seq, dk), lambda b, h: (0, 0)),
                 pl.BlockSpec((2, nblk, nblk), lambda b, h: (0, 0, 0)),
                 pl.BlockSpec((2, nblk, nblk), lambda b, h: (0, 0, 0)),
                 pl.BlockSpec((nblk, GLA_BLOCK_CHUNKS * dv), lambda b, h: (0, 0))]
    return pl.pallas_call(
        functools.partial(_gla_kernel, n_lat=n_lat, n_ctx=n_ctx),
        grid=(n_batch, n_heads), in_specs=in_specs,
        out_specs=[pl.BlockSpec((seq, dv), lambda b, h: (b, h)),
                   pl.BlockSpec((ctx_len, dv), lambda b, h: (b, h))],
        out_shape=[jax.ShapeDtypeStruct((n_batch * seq, n_heads * dv), BF16),
                   jax.ShapeDtypeStruct((n_batch * ctx_len, n_heads * dv), BF16)],
        scratch_shapes=[pltpu.VMEM((dk, dv), F32), pltpu.VMEM((seq, dv), F32),
                        pltpu.VMEM((ctx_len, dv), F32)],
        compiler_params=_params("parallel", "parallel"), name="gla",
    )(proj, proj, proj, proj, lr, proj, proj, proj, proj, lr,
      wg_pad, bg, norm_w.reshape(1, dv), cos, sin, keep.astype(BF16), keep, vmask)


def _lru_pitch(n):
    seg = n // SUBLANES
    assert seg % SUBLANES == 0
    return seg + 4


def _lru_kernel(x_ref, g_ref, xc_ref, gc_ref, cw_ref, cb_ref, w4_ref, b4_ref, lam_ref, o_ref, oc_ref,
                xpad_ref, af_ref, cf_ref, ab_ref, cbk_ref, hf_ref, pf_ref, hb_ref, pb_ref, *, seq, ctx_len):
    bw = x_ref.shape[1]
    pad = SUBLANES
    sp = _softplus(-lam_ref[...])
    cw = cw_ref[...]
    w4 = w4_ref[...]
    b4 = b4_ref[...]

    def put(ref, n, t0, val):
        seg, pitch = n // SUBLANES, _lru_pitch(n)
        for s in range(SUBLANES):
            lo, hi = max(t0, s * seg), min(t0 + val.shape[0], (s + 1) * seg)
            if lo < hi:
                ref[s * pitch + lo - s * seg:s * pitch + hi - s * seg, :] = val[lo - t0:hi - t0, :]

    def coeffs(src_ref, n):
        xpad_ref[0:pad, :] = jnp.zeros((pad, bw), F32)
        xpad_ref[pad + n:pad + n + pad, :] = jnp.zeros((pad, bw), F32)
        tile = min(n, 512)
        for t0 in range(0, n, tile):
            xpad_ref[pad + t0:pad + t0 + tile, :] = src_ref[t0:t0 + tile, :].astype(F32)
        for t0 in range(0, n, tile):
            xc = cb_ref[...]
            for j in range(LRU_CONV):
                lo = pad + t0 + j - LRU_CONV // 2
                xc = xc + cw[j:j + 1, :] * xpad_ref[lo:lo + tile, :]
            z = _dot(xc.astype(BF16), w4) + b4
            for d, (a_ref, c_ref) in enumerate(((af_ref, cf_ref), (ab_ref, cbk_ref))):
                r = jax.nn.sigmoid(z[:, (2 * d) * bw:(2 * d + 1) * bw])
                i = jax.nn.sigmoid(z[:, (2 * d + 1) * bw:(2 * d + 2) * bw])
                log_a = (-LRU_C) * r * sp[d:d + 1, :]
                a = jnp.exp(log_a)
                put(a_ref, n, t0, a)
                put(c_ref, n, t0, jnp.sqrt(1.0 - a * a) * (i * xc))

    def scan(n, h0_f, h0_b):
        seg, pitch = n // SUBLANES, _lru_pitch(n)

        def step(g, carry):
            hf, pf, hb, pb = carry
            fwd = pl.ds(g, SUBLANES, stride=pitch)
            bwd = pl.ds(seg - 1 - g, SUBLANES, stride=pitch)
            a = af_ref[fwd, :]
            hf = a * hf + cf_ref[fwd, :]
            pf = a * pf
            hf_ref[fwd, :] = hf
            pf_ref[fwd, :] = pf
            a = ab_ref[bwd, :]
            hb = a * hb + cbk_ref[bwd, :]
            pb = a * pb
            hb_ref[bwd, :] = hb
            pb_ref[bwd, :] = pb
            return hf, pf, hb, pb

        zero = jnp.zeros((SUBLANES, bw), F32)
        one = jnp.ones((SUBLANES, bw), F32)
        hf, pf, hb, pb = lax.fori_loop(0, seg, step, (zero, one, zero, one), unroll=8)
        carry_f = [h0_f]
        for s in range(SUBLANES):
            carry_f.append(pf[s:s + 1, :] * carry_f[s] + hf[s:s + 1, :])
        carry_b = [h0_b]
        for s in range(SUBLANES - 1, -1, -1):
            carry_b.append(pb[s:s + 1, :] * carry_b[-1] + hb[s:s + 1, :])
        carry_b = carry_b[::-1]
        return carry_f, carry_b

    def emit(n, carry_f, carry_b, gate_ref, out_ref):
        seg, pitch = n // SUBLANES, _lru_pitch(n)
        for s in range(SUBLANES):
            rows = slice(s * seg, (s + 1) * seg)
            held = slice(s * pitch, s * pitch + seg)
            h = (hf_ref[held, :] + pf_ref[held, :] * carry_f[s]
                 + hb_ref[held, :] + pb_ref[held, :] * carry_b[s + 1])
            out_ref[rows, :] = (h * _gelu_tanh(gate_ref[rows, :].astype(F32))).astype(out_ref.dtype)

    zero_h = jnp.zeros((1, bw), F32)
    coeffs(xc_ref, ctx_len)
    cf, cbw = scan(ctx_len, zero_h, zero_h)
    emit(ctx_len, cf, cbw, gc_ref, oc_ref)
    coeffs(x_ref, seq)
    lf, lb = scan(seq, cf[SUBLANES], cbw[0])
    emit(seq, lf, lb, g_ref, o_ref)


def _rglru(proj, conv_w, conv_b, w4, b4, lam, n_batch, seq, ctx_len, col_x):
    n_blk = w4.shape[0]
    bw = LANES
    cb = n_batch * seq // ctx_len
    width = n_blk * bw
    return pl.pallas_call(
        functools.partial(_lru_kernel, seq=seq, ctx_len=ctx_len),
        grid=(n_batch, n_blk),
        in_specs=[pl.BlockSpec((seq, bw), lambda b, j: (b, col_x + j)),
                  pl.BlockSpec((seq, bw), lambda b, j: (b, col_x + n_blk + j)),
                  pl.BlockSpec((ctx_len, bw), lambda b, j: (cb + b, col_x + j)),
                  pl.BlockSpec((ctx_len, bw), lambda b, j: (cb + b, col_x + n_blk + j)),
                  pl.BlockSpec((LRU_CONV, bw), lambda b, j: (0, j)),
                  pl.BlockSpec((1, bw), lambda b, j: (0, j)),
                  pl.BlockSpec((None, bw, 4 * bw), lambda b, j: (j, 0, 0)),
                  pl.BlockSpec((None, 1, 4 * bw), lambda b, j: (j, 0, 0)),
                  pl.BlockSpec((2, bw), lambda b, j: (0, j))],
        out_specs=[pl.BlockSpec((seq, bw), lambda b, j: (b, j)),
                   pl.BlockSpec((ctx_len, bw), lambda b, j: (b, j))],
        out_shape=[jax.ShapeDtypeStruct((n_batch * seq, width), BF16),
                   jax.ShapeDtypeStruct((n_batch * ctx_len, width), BF16)],
        scratch_shapes=([pltpu.VMEM((seq + 2 * SUBLANES, bw), F32)]
                        + [pltpu.VMEM((SUBLANES * _lru_pitch(seq), bw), F32)] * 8),
        compiler_params=_params("parallel", "parallel"), name="rglru",
    )(proj, proj, proj, proj, conv_w, conv_b.reshape(1, width), w4, b4, lam)


def _row_tile(n_rows, cap):
    t = cap
    while n_rows % t:
        t //= 2
    return t


def kernel(x, c, ctx, c_ctx, w_mod, b_mod, norm_mix, norm_ffn, w_in, na_rpb, gla_wg, gla_bg, gla_norm,
           conv_w, conv_b, lru_wa, lru_ba, lru_wx, lru_bx, lru_lam, w_out, ffd_w1, ffd_w3, ffd_w2,
           router, moe_w1, moe_w3, moe_w2, final_norm):
    n_batch, seq, d = x.shape
    ctx_len = ctx.shape[1]
    depth = w_in.shape[0]
    n_lat = n_batch * seq
    n_tot = n_lat + n_batch * ctx_len
    na_heads = na_rpb.shape[1]
    na_w = na_heads * LANES
    gla_kw = gla_wg.shape[3]
    gla_heads = gla_kw // LANES
    gla_dv = gla_norm.shape[1]
    gla_vw = gla_heads * gla_dv
    lru_w = conv_w.shape[2]
    n_blk = lru_wa.shape[2]
    assert n_batch < SUBLANES and lru_w == n_blk * LANES and seq % (SUBLANES * GRID_W) == 0

    p_lr = 3 * na_w + 2 * gla_kw + 2 * gla_vw
    col_gla = 3 * na_w // LANES
    col_lru = p_lr // LANES

    tm = _row_tile(n_tot, 1024)
    tm_lat = _row_tile(n_lat, 1024)
    tr = _row_tile(n_tot, 256)

    x_all = jnp.concatenate([x.reshape(n_lat, d), ctx.reshape(n_batch * ctx_len, d)], axis=0)
    c8 = jnp.zeros((SUBLANES, d), F32).at[:n_batch].set(c).at[n_batch].set(c_ctx)

    for l in range(depth):
        last = l == depth - 1
        rows_out = n_lat if last else n_tot
        tmo = tm_lat if last else tm
        mod3 = _modulation(c8, w_mod, b_mod, l).reshape(SUBLANES * N_MOD, 1, d)

        h = _norm_mod(x_all, norm_mix[l], mod3, 1, 0, n_tot, seq, n_batch, tr)
        w_l = w_in[l]
        w_main = jnp.concatenate([w_l[:, :p_lr], w_l[:, p_lr + 2 * GLA_LOWRANK:]], axis=1).astype(BF16)
        w_lr = jnp.pad(w_l[:, p_lr:p_lr + 2 * GLA_LOWRANK], ((0, 0), (0, LANES - 2 * GLA_LOWRANK))).astype(BF16)
        proj = _matmul(h, w_main, BF16, tm, _row_tile(w_main.shape[1], 512))
        lr = _matmul(h, w_lr, F32, tm, LANES)

        na_l, na_c = _neighbourhood_attention(proj, _na_bias_table(na_rpb[l]), n_batch, seq, ctx_len,
                                              na_heads, 0)
        wg = gla_wg[l]
        wg_pad = jnp.zeros((2, LANES, gla_kw), F32)
        wg_pad = wg_pad.at[0, :GLA_LOWRANK].set(wg[0]).at[1, GLA_LOWRANK:2 * GLA_LOWRANK].set(wg[1])
        gla_l, gla_c = _gla(proj, lr, wg_pad.astype(BF16), gla_bg[l], gla_norm[l], n_batch, seq, ctx_len,
                            gla_heads, LANES, gla_dv, col_gla)
        w4 = jnp.concatenate([lru_wa[l, 0], lru_wx[l, 0], lru_wa[l, 1], lru_wx[l, 1]], axis=-1).astype(BF16)
        b4 = jnp.concatenate([lru_ba[l, 0].reshape(n_blk, 1, LANES), lru_bx[l, 0].reshape(n_blk, 1, LANES),
                              lru_ba[l, 1].reshape(n_blk, 1, LANES), lru_bx[l, 1].reshape(n_blk, 1, LANES)],
                             axis=-1)
        lru_l, lru_c = _rglru(proj, conv_w[l], conv_b[l], w4, b4, lru_lam[l], n_batch, seq, ctx_len, col_lru)

        mix = jnp.concatenate([na_l, gla_l, lru_l], axis=1)
        if not last:
            mix = jnp.concatenate([mix, jnp.concatenate([na_c, gla_c, lru_c], axis=1)], axis=0)
        x_all = _matmul_residual(mix, w_out[l].astype(BF16), x_all, mod3, 2, rows_out, seq, n_batch,
                                 tmo, _row_tile(d, 1024), _row_tile(mix.shape[1], 2048))

        j = l // 2
        if l % 2 == 0:
            h = _norm_mod(x_all, norm_ffn[l], mod3, 4, 3, rows_out, seq, n_batch, tr)
            act = _swiglu_in(h, ffd_w1[j].astype(BF16), ffd_w3[j].astype(BF16), tmo,
                             _row_tile(ffd_w1.shape[2], 512))
            x_all = _matmul_residual(act, ffd_w2[j].astype(BF16), x_all, mod3, 5, rows_out, seq, n_batch, tmo,
                                     _row_tile(d, 1024), _row_tile(act.shape[1], 2048))
        else:
            h, route = _norm_mod(x_all, norm_ffn[l], mod3, 4, 3, rows_out, seq, n_batch, tr, router=router[j])
            x_all = _moe(h, route, moe_w1[j].astype(BF16), moe_w3[j].astype(BF16), moe_w2[j].astype(BF16),
                         x_all, mod3, 5, rows_out, seq, n_batch, tn=_row_tile(moe_w1.shape[3], 512),
                         tc=_row_tile(rows_out, 256), out_norm_w=final_norm if last else None)

    out = x_all if depth % 2 == 0 else _final_norm(x_all, final_norm, n_lat, tr)
    return out.reshape(n_batch, seq, d)
```

```python
import functools

import jax
import jax.numpy as jnp
from jax import lax
from jax.experimental import pallas as pl
from jax.experimental.pallas import tpu as pltpu

F32 = jnp.float32
BF16 = jnp.bfloat16

EPS = 1e-6
ROPE_BASE = 10000.0
GRID_W = 64
NA_WIN_ROWS = 8
NA_WIN_COLS = 16
GLA_CHUNK = 64
GLA_LOWRANK = 16
GLA_NORMALIZER = 16.0
LRU_C = 8.0
LRU_CONV = 4
N_MOD = 6
TOP_K = 2
LANES = 128
SUBLANES = 8
VMEM_LIMIT = 56 * 1024 * 1024
NEG_BIG = -1e30

_NT = (((1,), (1,)), ((), ()))


def _params(*sem):
    return pltpu.CompilerParams(dimension_semantics=sem, vmem_limit_bytes=VMEM_LIMIT)


def _dot(a, b):
    return jnp.dot(a, b, preferred_element_type=F32)


def _split_bf16(x):
    hi = x.astype(BF16)
    lo = (x - hi.astype(F32)).astype(BF16)
    return hi, lo


def _softplus(x):
    return jnp.maximum(x, 0.0) + jnp.log1p(jnp.exp(-jnp.abs(x)))


def _silu(x):
    return x * jax.nn.sigmoid(x)


def _gelu_tanh(x):
    return 0.5 * x * (1.0 + jnp.tanh(0.7978845608028654 * (x + 0.044715 * (x * x * x))))


def _mod_kernel(c_ref, w_ref, b_ref, o_ref):
    c = c_ref[...]
    a_hi, a_lo = _split_bf16(_silu(c))
    w_hi, w_lo = _split_bf16(w_ref[...])
    o_ref[...] = _dot(a_hi, w_hi) + _dot(a_lo, w_hi) + _dot(a_hi, w_lo) + b_ref[...]


def _modulation(c8, w_mod, b_mod, layer, tn=512):
    d = c8.shape[1]
    n = w_mod.shape[2]
    return pl.pallas_call(
        _mod_kernel,
        grid=(n // tn,),
        in_specs=[pl.BlockSpec((SUBLANES, d), lambda j: (0, 0)),
                  pl.BlockSpec((None, d, tn), lambda j: (layer, 0, j)),
                  pl.BlockSpec((None, 1, tn), lambda j: (layer, 0, j))],
        out_specs=pl.BlockSpec((SUBLANES, tn), lambda j: (0, j)),
        out_shape=jax.ShapeDtypeStruct((SUBLANES, n), F32),
        compiler_params=_params("parallel"),
        name="modulation",
    )(c8, w_mod, b_mod.reshape(b_mod.shape[0], 1, n))


def _mod_spec(which, tile_rows, seq, n_batch, d_block, col_axis=None):
    def row(i):
        return jnp.minimum((i * tile_rows) // seq, n_batch) * N_MOD + which
    if col_axis is None:
        return pl.BlockSpec((None, 1, d_block), lambda i, *_: (row(i), 0, 0))
    return pl.BlockSpec((None, 1, d_block), lambda *g: (row(g[0]), 0, g[col_axis]))


def _norm_mod_body(x_ref, nw_ref, sc_ref, sh_ref):
    x = x_ref[...]
    y = x * lax.rsqrt(jnp.mean(x * x, axis=-1, keepdims=True) + EPS) * nw_ref[...]
    return y * (1.0 + sc_ref[...]) + sh_ref[...]


def _row_group_specs(group_rows, tile, cols, col_index):
    specs, bounds = [], [0]
    for rows in group_rows:
        first, n_tiles = bounds[-1], rows // tile
        specs.append(pl.BlockSpec(
            (tile, cols), lambda i, *g, first=first, n_tiles=n_tiles: (jnp.clip(i - first, 0, n_tiles - 1),
                                                                        col_index(*g))))
        bounds.append(first + n_tiles)
    return specs, bounds


def _in_group(i, bounds, g):
    return jnp.logical_and(i >= bounds[g], i < bounds[g + 1])


def _norm_mod_kernel(*refs, bounds):
    n_groups = len(bounds) - 1
    nw_ref, sc_ref, sh_ref, o_ref = refs[n_groups:]
    for g in range(n_groups):
        @pl.when(_in_group(pl.program_id(0), bounds, g))
        def _(g=g):
            o_ref[...] = _norm_mod_body(refs[g], nw_ref, sc_ref, sh_ref).astype(o_ref.dtype)


def _norm_route_kernel(x_ref, nw_ref, sc_ref, sh_ref, rhi_ref, rlo_ref, o_ref, g_ref, *, n_experts):
    h = _norm_mod_body(x_ref, nw_ref, sc_ref, sh_ref)
    o_ref[...] = h.astype(o_ref.dtype)
    h_hi, h_lo = _split_bf16(h)
    logits = _dot(h_hi, rhi_ref[...]) + _dot(h_lo, rhi_ref[...]) + _dot(h_hi, rlo_ref[...])
    lane = lax.broadcasted_iota(jnp.int32, logits.shape, 1)
    valid = lane < n_experts
    logits = jnp.where(valid, logits, NEG_BIG)
    e = jnp.exp(logits - jnp.max(logits, axis=-1, keepdims=True))
    probs = jnp.where(valid, e / jnp.sum(e, axis=-1, keepdims=True), -1.0)
    p1 = jnp.max(probs, axis=-1, keepdims=True)
    i1 = jnp.min(jnp.where(probs == p1, lane, LANES), axis=-1, keepdims=True)
    rest = jnp.where(lane == i1, -1.0, probs)
    p2 = jnp.max(rest, axis=-1, keepdims=True)
    i2 = jnp.min(jnp.where(rest == p2, lane, LANES), axis=-1, keepdims=True)
    den = p1 + p2
    g_ref[...] = (jnp.where(lane == 0, p1 / den, 0.0) + jnp.where(lane == 1, p2 / den, 0.0)
                  + jnp.where(lane == 2, i1.astype(F32), 0.0) + jnp.where(lane == 3, i2.astype(F32), 0.0))


def _norm_mod(streams, norm_w, mod3, which_scale, which_shift, n_rows, seq, n_batch, tr=256,
              router=None):
    d = streams[0].shape[1]
    group_rows = [n_rows] if len(streams) == 1 else [s.shape[0] for s in streams]
    assert sum(group_rows) == n_rows
    x_specs, bounds = _row_group_specs(group_rows, tr, d, lambda: 0)
    in_specs = x_specs + [pl.BlockSpec((1, d), lambda i: (0, 0)),
                          _mod_spec(which_scale, tr, seq, n_batch, d),
                          _mod_spec(which_shift, tr, seq, n_batch, d)]
    h_spec = pl.BlockSpec((tr, d), lambda i: (i, 0))
    h_shape = jax.ShapeDtypeStruct((n_rows, d), BF16)
    if router is None:
        return pl.pallas_call(
            functools.partial(_norm_mod_kernel, bounds=bounds), grid=(n_rows // tr,), in_specs=in_specs,
            out_specs=h_spec, out_shape=h_shape, compiler_params=_params("parallel"), name="norm_mod",
        )(*streams, norm_w.reshape(1, d), mod3, mod3)
    (x_all,) = streams
    n_experts = router.shape[1]
    r_pad = jnp.pad(router, ((0, 0), (0, LANES - n_experts)))
    r_hi = r_pad.astype(BF16)
    r_lo = (r_pad - r_hi.astype(F32)).astype(BF16)
    w_spec = pl.BlockSpec((d, LANES), lambda i: (0, 0))
    return pl.pallas_call(
        functools.partial(_norm_route_kernel, n_experts=n_experts),
        grid=(n_rows // tr,), in_specs=in_specs + [w_spec, w_spec],
        out_specs=[h_spec, pl.BlockSpec((tr, LANES), lambda i: (i, 0))],
        out_shape=[jax.ShapeDtypeStruct((n_rows, d), F32), jax.ShapeDtypeStruct((n_rows, LANES), F32)],
        compiler_params=_params("parallel"), name="norm_route",
    )(x_all, norm_w.reshape(1, d), mod3, mod3, r_hi, r_lo)


def _final_norm_kernel(x_ref, nw_ref, o_ref):
    x = x_ref[...]
    o_ref[...] = x * lax.rsqrt(jnp.mean(x * x, axis=-1, keepdims=True) + EPS) * nw_ref[...]


def _final_norm(x_all, norm_w, n_rows, tr=256):
    d = x_all.shape[1]
    return pl.pallas_call(
        _final_norm_kernel, grid=(n_rows // tr,),
        in_specs=[pl.BlockSpec((tr, d), lambda i: (i, 0)), pl.BlockSpec((1, d), lambda i: (0, 0))],
        out_specs=pl.BlockSpec((tr, d), lambda i: (i, 0)),
        out_shape=jax.ShapeDtypeStruct((n_rows, d), F32),
        compiler_params=_params("parallel"), name="final_norm",
    )(x_all, norm_w.reshape(1, d))


def _mm_kernel(a_ref, w_ref, o_ref):
    o_ref[...] = _dot(a_ref[...], w_ref[...]).astype(o_ref.dtype)


def _matmul(a, w, out_dtype, tm, tn):
    m, k = a.shape
    n = w.shape[1]
    return pl.pallas_call(
        _mm_kernel, grid=(m // tm, n // tn),
        in_specs=[pl.BlockSpec((tm, k), lambda i, j: (i, 0)),
                  pl.BlockSpec((k, tn), lambda i, j: (0, j))],
        out_specs=pl.BlockSpec((tm, tn), lambda i, j: (i, j)),
        out_shape=jax.ShapeDtypeStruct((m, n), out_dtype),
        compiler_params=_params("parallel", "parallel"), name="matmul",
    )(a, w)


def _mm_res_kernel(a_ref, w_ref, x_ref, g_ref, o_ref, acc_ref, *, nk):
    k = pl.program_id(2)

    @pl.when(k == 0)
    def _():
        acc_ref[...] = jnp.zeros_like(acc_ref)

    acc_ref[...] += _dot(a_ref[...], w_ref[...])

    @pl.when(k == nk - 1)
    def _():
        o_ref[...] = x_ref[...] + g_ref[...] * acc_ref[...]


def _matmul_residual(a, w, x_all, mod3, which_gate, n_rows, seq, n_batch, tm, tn, tk):
    k = a.shape[1]
    d = w.shape[1]
    nk = k // tk
    return pl.pallas_call(
        functools.partial(_mm_res_kernel, nk=nk), grid=(n_rows // tm, d // tn, nk),
        in_specs=[pl.BlockSpec((tm, tk), lambda i, j, kk: (i, kk)),
                  pl.BlockSpec((tk, tn), lambda i, j, kk: (kk, j)),
                  pl.BlockSpec((tm, tn), lambda i, j, kk: (i, j)),
                  _mod_spec(which_gate, tm, seq, n_batch, tn, col_axis=1)],
        out_specs=pl.BlockSpec((tm, tn), lambda i, j, kk: (i, j)),
        out_shape=jax.ShapeDtypeStruct((n_rows, d), F32),
        scratch_shapes=[pltpu.VMEM((tm, tn), F32)],
        compiler_params=_params("parallel", "parallel", "arbitrary"), name="matmul_residual",
    )(a, w, x_all, mod3)


def _out_proj_kernel(*refs, n_groups, k_tiles, bounds, n_x):
    n_src = len(k_tiles)
    w_ref = refs[n_groups * n_src]
    x_refs = refs[n_groups * n_src + 1:n_groups * n_src + 1 + n_x]
    g_ref, o_ref, acc_ref = refs[-3:]
    i = pl.program_id(0)
    k = pl.program_id(2)

    @pl.when(k == 0)
    def _():
        acc_ref[...] = jnp.zeros_like(acc_ref)

    k0 = 0
    for s in range(n_src):
        for g in range(n_groups):
            @pl.when(jnp.logical_and(_in_group(i, bounds, g), jnp.logical_and(k >= k0, k < k0 + k_tiles[s])))
            def _(a_ref=refs[g * n_src + s]):
                acc_ref[...] += _dot(a_ref[...], w_ref[...])
        k0 += k_tiles[s]

    @pl.when(k == k0 - 1)
    def _():
        if n_x == 1:
            o_ref[...] = x_refs[0][...] + g_ref[...] * acc_ref[...]
        else:
            for g in range(n_groups):
                @pl.when(_in_group(i, bounds, g))
                def _(x_ref=x_refs[g]):
                    o_ref[...] = x_ref[...] + g_ref[...] * acc_ref[...]


def _out_proj(sources, w, streams, mod3, which_gate, seq, n_batch, tm, tn, tk):
    d = w.shape[1]
    group_rows = [grp[0].shape[0] for grp in sources]
    k_tiles = [a.shape[1] // tk for a in sources[0]]
    k_first = [sum(k_tiles[:s]) for s in range(len(k_tiles))]
    n_rows = sum(group_rows)
    a_specs, args = [], []
    for g, grp in enumerate(sources):
        for s, a in enumerate(grp):
            col = lambda j, kk, s=s: jnp.clip(kk - k_first[s], 0, k_tiles[s] - 1)
            specs, bounds = _row_group_specs(group_rows, tm, tk, col)
            a_specs.append(specs[g])
            args.append(a)
    if len(streams) == 1:
        x_specs = [pl.BlockSpec((tm, tn), lambda i, j, kk: (i, j))]
    else:
        assert [s.shape[0] for s in streams] == group_rows
        x_specs, _ = _row_group_specs(group_rows, tm, tn, lambda j, kk: j)
    return pl.pallas_call(
        functools.partial(_out_proj_kernel, n_groups=len(sources), k_tiles=k_tiles, bounds=bounds,
                          n_x=len(streams)),
        grid=(n_rows // tm, d // tn, sum(k_tiles)),
        in_specs=a_specs + [pl.BlockSpec((tk, tn), lambda i, j, kk: (kk, j))] + x_specs
        + [_mod_spec(which_gate, tm, seq, n_batch, tn, col_axis=1)],
        out_specs=pl.BlockSpec((tm, tn), lambda i, j, kk: (i, j)),
        out_shape=jax.ShapeDtypeStruct((n_rows, d), F32),
        scratch_shapes=[pltpu.VMEM((tm, tn), F32)],
        compiler_params=_params("parallel", "parallel", "arbitrary"), name="out_proj",
    )(*args, w, *streams, mod3)


def _swiglu_kernel(a_ref, w1_ref, w3_ref, o_ref):
    a = a_ref[...]
    o_ref[...] = (_silu(_dot(a, w1_ref[...])) * _dot(a, w3_ref[...])).astype(o_ref.dtype)


def _swiglu_in(a, w1, w3, tm, tn):
    m, k = a.shape
    f = w1.shape[1]
    return pl.pallas_call(
        _swiglu_kernel, grid=(m // tm, f // tn),
        in_specs=[pl.BlockSpec((tm, k), lambda i, j: (i, 0)),
                  pl.BlockSpec((k, tn), lambda i, j: (0, j)),
                  pl.BlockSpec((k, tn), lambda i, j: (0, j))],
        out_specs=pl.BlockSpec((tm, tn), lambda i, j: (i, j)),
        out_shape=jax.ShapeDtypeStruct((m, f), BF16),
        compiler_params=_params("parallel", "parallel"), name="swiglu_in",
    )(a, w1, w3)


MOE_TILE = 512


def _moe_dispatch(route, n_experts, tile):
    n = route.shape[0]
    e = jnp.concatenate([route[:, 2], route[:, 3]]).astype(jnp.int32)
    onehot = (e[:, None] == jnp.arange(n_experts, dtype=jnp.int32)[None, :]).astype(jnp.int32)
    csum = jnp.cumsum(onehot, axis=0)
    counts = csum[-1]
    rank = jnp.sum((csum - onehot) * onehot, axis=1)
    padded = (counts + tile - 1) // tile * tile
    ends = jnp.cumsum(padded)
    pos = jnp.sum(onehot * (ends - padded)[None, :], axis=1) + rank
    n_tiles = (TOP_K * n) // tile + n_experts
    tok = jnp.tile(jnp.arange(n, dtype=jnp.int32), TOP_K)
    src = jnp.zeros((n_tiles * tile,), jnp.int32).at[pos].set(tok, unique_indices=True)
    tile_id = jnp.arange(n_tiles, dtype=jnp.int32)
    tile_expert = jnp.sum((tile_id[:, None] * tile >= ends[None, :]).astype(jnp.int32), axis=1)
    n_used = ends[-1] // tile
    tile_expert = jnp.minimum(tile_expert, jnp.max(jnp.where(tile_id < n_used, tile_expert, 0)))
    return src.reshape(n_tiles, 1, tile), pos, tile_expert.astype(jnp.int32), n_used.reshape(1).astype(jnp.int32)


def _moe_in_kernel(te_ref, nu_ref, src_ref, nxt_ref, h_hbm, w1_ref, w3_ref, o_ref, rows_ref, a_ref, sem, *, tile):
    i = pl.program_id(0)
    j = pl.program_id(1)
    used = i < nu_ref[0]
    slot = i % 2

    def gather(idx_ref, s):
        def issue(p, carry):
            pltpu.make_async_copy(h_hbm.at[pl.ds(idx_ref[0, p], 1), :], rows_ref.at[s, pl.ds(p, 1), :],
                                  sem.at[s]).start()
            return carry
        lax.fori_loop(0, tile, issue, 0, unroll=8)

    @pl.when(jnp.logical_and(i == 0, j == 0))
    def _():
        gather(src_ref, 0)

    @pl.when(jnp.logical_and(i + 1 < nu_ref[0], j == 0))
    def _():
        gather(nxt_ref, 1 - slot)

    @pl.when(jnp.logical_and(used, j == 0))
    def _():
        pltpu.make_async_copy(h_hbm.at[pl.ds(0, tile), :], rows_ref.at[slot], sem.at[slot]).wait()
        a_ref[...] = rows_ref[slot].astype(BF16)

    @pl.when(used)
    def _():
        a = a_ref[...]
        o_ref[...] = (_silu(_dot(a, w1_ref[...])) * _dot(a, w3_ref[...])).astype(o_ref.dtype)

    @pl.when(jnp.logical_not(used))
    def _():
        o_ref[...] = jnp.zeros_like(o_ref)


def _moe_out_kernel(te_ref, nu_ref, a_ref, w_ref, o_ref):
    used = pl.program_id(0) < nu_ref[0]

    @pl.when(used)
    def _():
        o_ref[...] = _dot(a_ref[...], w_ref[...])

    @pl.when(jnp.logical_not(used))
    def _():
        o_ref[...] = jnp.zeros_like(o_ref)


def _moe_combine_kernel(pos_ref, y_hbm, x_ref, g_ref, route_ref, *rest, tc, normed):
    o_ref, rows_ref, sem = rest[-3:]

    def issue(t, carry):
        for s in range(TOP_K):
            pltpu.make_async_copy(y_hbm.at[pl.ds(pos_ref[0, s * tc + t], 1), :],
                                  rows_ref.at[s, pl.ds(t, 1), :], sem).start()
        return carry
    lax.fori_loop(0, tc, issue, 0, unroll=8)
    for s in range(TOP_K):
        pltpu.make_async_copy(y_hbm.at[pl.ds(0, tc), :], rows_ref.at[s], sem).wait()
    route = route_ref[...]
    moe = route[:, 0:1] * rows_ref[0] + route[:, 1:2] * rows_ref[1]
    o = x_ref[...] + g_ref[...] * moe
    if normed:
        o = o * lax.rsqrt(jnp.mean(o * o, axis=-1, keepdims=True) + EPS) * rest[0][...]
    o_ref[...] = o


def _moe(h, route, w1, w3, w2, x_all, mod3, which_gate, n_rows, seq, n_batch, tn=512, tc=256, out_norm_w=None):
    n, d = h.shape
    n_experts, _, f = w1.shape
    tile = MOE_TILE
    src, pos, tile_expert, n_used = _moe_dispatch(route, n_experts, tile)
    n_tiles = src.shape[0]
    smem = functools.partial(pl.BlockSpec, memory_space=pltpu.SMEM)
    act = pl.pallas_call(
        functools.partial(_moe_in_kernel, tile=tile),
        grid_spec=pltpu.PrefetchScalarGridSpec(
            num_scalar_prefetch=2, grid=(n_tiles, f // tn),
            in_specs=[smem((None, 1, tile), lambda i, j, te, nu: (i, 0, 0)),
                      smem((None, 1, tile), lambda i, j, te, nu: (jnp.minimum(i + 1, n_tiles - 1), 0, 0)),
                      pl.BlockSpec(memory_space=pl.ANY),
                      pl.BlockSpec((None, d, tn), lambda i, j, te, nu: (te[i], 0, j)),
                      pl.BlockSpec((None, d, tn), lambda i, j, te, nu: (te[i], 0, j))],
            out_specs=pl.BlockSpec((tile, tn), lambda i, j, te, nu: (i, j)),
            scratch_shapes=[pltpu.VMEM((2, tile, d), F32), pltpu.VMEM((tile, d), BF16),
                            pltpu.SemaphoreType.DMA((2,))]),
        out_shape=jax.ShapeDtypeStruct((n_tiles * tile, f), BF16),
        compiler_params=_params("arbitrary", "arbitrary"), name="moe_in",
    )(tile_expert, n_used, src, src, h, w1, w3)
    tn2 = _row_tile(d, 1024)
    y = pl.pallas_call(
        _moe_out_kernel,
        grid_spec=pltpu.PrefetchScalarGridSpec(
            num_scalar_prefetch=2, grid=(n_tiles, d // tn2),
            in_specs=[pl.BlockSpec((tile, f), lambda i, j, te, nu: (i, 0)),
                      pl.BlockSpec((None, f, tn2), lambda i, j, te, nu: (te[i], 0, j))],
            out_specs=pl.BlockSpec((tile, tn2), lambda i, j, te, nu: (i, j))),
        out_shape=jax.ShapeDtypeStruct((n_tiles * tile, d), F32),
        compiler_params=_params("arbitrary", "arbitrary"), name="moe_out",
    )(tile_expert, n_used, act, w2)
    pos2 = pos.reshape(TOP_K, n // tc, tc).transpose(1, 0, 2).reshape(n // tc, 1, TOP_K * tc)
    normed = out_norm_w is not None
    norm_specs = [pl.BlockSpec((1, d), lambda i: (0, 0))] if normed else []
    norm_args = [out_norm_w.reshape(1, d)] if normed else []
    return pl.pallas_call(
        functools.partial(_moe_combine_kernel, tc=tc, normed=normed),
        grid=(n_rows // tc,),
        in_specs=[smem((None, 1, TOP_K * tc), lambda i: (i, 0, 0)),
                  pl.BlockSpec(memory_space=pl.ANY),
                  pl.BlockSpec((tc, d), lambda i: (i, 0)),
                  _mod_spec(which_gate, tc, seq, n_batch, d),
                  pl.BlockSpec((tc, LANES), lambda i: (i, 0))] + norm_specs,
        out_specs=pl.BlockSpec((tc, d), lambda i: (i, 0)),
        out_shape=jax.ShapeDtypeStruct((n_rows, d), F32),
        scratch_shapes=[pltpu.VMEM((TOP_K, tc, d), F32), pltpu.SemaphoreType.DMA(())],
        compiler_params=_params("arbitrary"), name="moe_combine",
    )(pos2, y, x_all, mod3, route, *norm_args)


def _na_bias_table(rpb):
    n_heads = rpb.shape[0]
    n_dr = 2 * NA_WIN_ROWS - 1
    n_dc = 2 * NA_WIN_COLS - 1
    cols = jnp.arange(GRID_W)
    col_start = jnp.clip(cols - NA_WIN_COLS // 2, 0, GRID_W - NA_WIN_COLS)
    in_win = (cols[None, :] >= col_start[:, None]) & (cols[None, :] < col_start[:, None] + NA_WIN_COLS)
    dc = cols[None, :] - cols[:, None] + NA_WIN_COLS - 1
    expand = ((dc[None] == jnp.arange(n_dc)[:, None, None]) & in_win[None]).astype(F32)
    t = jnp.dot(rpb.reshape(-1, n_dc), expand.reshape(n_dc, GRID_W * GRID_W), precision=lax.Precision.HIGHEST)
    t = jnp.where(in_win.reshape(1, 1, -1), t.reshape(n_heads, n_dr, -1), NEG_BIG)
    t = t.reshape(n_heads, n_dr, GRID_W, GRID_W)
    masked = jnp.full((n_heads, GRID_W, GRID_W), NEG_BIG, F32)
    kinds = []
    for kind in range(3):
        q_rows = []
        for i in range(NA_QROWS):
            lo, dr0 = ((0, NA_WIN_ROWS - 1 - i), (i, NA_WIN_ROWS // 2 - 1),
                       (NA_QROWS, NA_WIN_ROWS - 1 - NA_QROWS - i))[kind]
            pieces = [t[:, dr0 + j - lo] if lo <= j < lo + NA_WIN_ROWS else masked for j in range(NA_KROWS)]
            q_rows.append(jnp.concatenate(pieces, axis=-1))
        kinds.append(jnp.concatenate(q_rows, axis=1))
    return jnp.stack(kinds, axis=1)


NA_QROWS = 4
NA_KROWS = NA_WIN_ROWS + NA_QROWS


def _na_kernel(q_ref, k_ref, v_ref, qc_ref, kc_ref, vc_ref, bias_ref, o_ref, oc_ref, *, rows, scale):
    kc = kc_ref[...]
    vc = vc_ref[...]
    nq = NA_QROWS * GRID_W
    nk = NA_KROWS * GRID_W
    n_blocks = rows // NA_QROWS

    def block(i, carry):
        base = jnp.clip(i * NA_QROWS - NA_WIN_ROWS // 2, 0, rows - NA_KROWS)
        kind = jnp.where(i == 0, 0, jnp.where(i == n_blocks - 1, 2, 1))
        q0 = pl.multiple_of(i * nq, nq)
        k0 = pl.multiple_of(base * GRID_W, nq)
        q = q_ref[pl.ds(q0, nq), :]
        kw = k_ref[pl.ds(k0, nk), :]
        vw = v_ref[pl.ds(k0, nk), :]
        s_w = lax.dot_general(q, kw, _NT, preferred_element_type=F32) * scale + bias_ref[kind]
        s_c = lax.dot_general(q, kc, _NT, preferred_element_type=F32) * scale
        m = jnp.maximum(jnp.max(s_w, axis=-1, keepdims=True), jnp.max(s_c, axis=-1, keepdims=True))
        p_w = jnp.exp(s_w - m)
        p_c = jnp.exp(s_c - m)
        den = jnp.sum(p_w, axis=-1, keepdims=True) + jnp.sum(p_c, axis=-1, keepdims=True)
        o = _dot(p_w.astype(BF16), vw) + _dot(p_c.astype(BF16), vc)
        o_ref[pl.ds(q0, nq), :] = (o / den).astype(o_ref.dtype)
        return carry

    lax.fori_loop(0, n_blocks, block, 0, unroll=2)

    s = lax.dot_general(qc_ref[...], kc, _NT, preferred_element_type=F32) * scale
    p = jnp.exp(s - jnp.max(s, axis=-1, keepdims=True))
    o = _dot(p.astype(BF16), vc) / jnp.sum(p, axis=-1, keepdims=True)
    oc_ref[...] = o.astype(oc_ref.dtype)


def _neighbourhood_attention(proj, bias_tab, n_batch, seq, ctx_len, n_heads, col0):
    dh = LANES
    rows = seq // GRID_W
    cb = n_batch * seq // ctx_len
    lat = lambda part: pl.BlockSpec((seq, dh), lambda b, h: (b, col0 + part * n_heads + h))
    ctx = lambda part: pl.BlockSpec((ctx_len, dh), lambda b, h: (cb + b, col0 + part * n_heads + h))
    return pl.pallas_call(
        functools.partial(_na_kernel, rows=rows, scale=dh ** -0.5),
        grid=(n_batch, n_heads),
        in_specs=[lat(0), lat(1), lat(2), ctx(0), ctx(1), ctx(2),
                  pl.BlockSpec((None, 3, NA_QROWS * GRID_W, NA_KROWS * GRID_W), lambda b, h: (h, 0, 0, 0))],
        out_specs=[pl.BlockSpec((seq, dh), lambda b, h: (b, h)),
                   pl.BlockSpec((ctx_len, dh), lambda b, h: (b, h))],
        out_shape=[jax.ShapeDtypeStruct((n_batch * seq, n_heads * dh), BF16),
                   jax.ShapeDtypeStruct((n_batch * ctx_len, n_heads * dh), BF16)],
        compiler_params=_params("parallel", "parallel"), name="neighbourhood_attention",
    )(proj, proj, proj, proj, proj, proj, bias_tab)


def _rope_tables(seq, dk):
    quarter = dk // 4
    inv_freq = ROPE_BASE ** (-jnp.arange(quarter, dtype=F32) / quarter)
    pos = jnp.arange(seq, dtype=jnp.int32)
    row = (pos // GRID_W).astype(F32)[:, None] * inv_freq
    col = (pos % GRID_W).astype(F32)[:, None] * inv_freq
    cos = jnp.concatenate([jnp.cos(row), jnp.cos(row), jnp.cos(col), jnp.cos(col)], axis=-1)
    sin = jnp.concatenate([-jnp.sin(row), jnp.sin(row), -jnp.sin(col), jnp.sin(col)], axis=-1)
    return cos, sin


GLA_BLOCK_CHUNKS = 4


def _gla_kernel(q_ref, k_ref, v_ref, g_ref, lr_ref, qc_ref, kc_ref, vc_ref, gc_ref, lrc_ref,
                wg_ref, bg_ref, nw_ref, cos_ref, sin_ref, tri_ref, keep_ref, vmask_ref, o_ref, oc_ref,
                s_ref, acc_ref, accc_ref, *, n_lat, n_ctx):
    dk = q_ref.shape[1]
    dv = v_ref.shape[1]
    c = GLA_CHUNK
    nb_lat = min(GLA_BLOCK_CHUNKS, n_lat)
    nb_ctx = min(GLA_BLOCK_CHUNKS, n_ctx)

    def swap_quarters(t):
        lane = lax.broadcasted_iota(jnp.int32, t.shape, 1)
        first_quarter = (lane % (dk // 2)) < (dk // 4)
        return jnp.where(first_quarter, pltpu.roll(t, dk - dk // 4, 1), pltpu.roll(t, dk // 4, 1))

    def block(refs, blk, nb, d, rope):
        qr, kr, vr, lrr = refs
        n = nb * c
        rows = pl.ds(pl.multiple_of(blk * n, n), n)
        q = qr[rows, :].astype(F32) * (dk ** -0.5)
        k = kr[rows, :].astype(F32)
        if rope:
            cos = cos_ref[rows, :]
            sin = sin_ref[rows, :]
            q = q * cos + swap_quarters(q) * sin
            k = k * cos + swap_quarters(k) * sin
        v = vr[rows, :]
        z = _dot(lrr[rows, :].astype(BF16), wg_ref[d]) + bg_ref[d:d + 1, :]
        logg = -_softplus(-z) * (1.0 / GLA_NORMALIZER)
        g1 = logg.astype(BF16)
        rem = logg - g1.astype(F32)
        g2 = rem.astype(BF16)
        g3 = (rem - g2.astype(F32)).astype(BF16)
        b3 = _dot(tri_ref[d, :n, :n], jnp.concatenate([g1, g2, g3], axis=1))
        b = b3[:, :dk] + b3[:, dk:2 * dk] + b3[:, 2 * dk:]
        end_row = c - 1 if d == 0 else 0
        ends = [b[m * c + end_row:m * c + end_row + 1, :] for m in range(nb)]
        b_end = jnp.concatenate([jnp.broadcast_to(e, (c, dk)) for e in ends], axis=0)
        q_dec = (q * jnp.exp(b)).astype(BF16)
        k_inv = (k * jnp.exp(-b)).astype(BF16)
        k_end_t = (k * jnp.exp(b_end - b)).T.astype(BF16)
        att = lax.dot_general(q_dec, k_inv, _NT, preferred_element_type=F32)
        att = jnp.where(keep_ref[d, :n, :n] != 0.0, att, 0.0).astype(BF16)
        o_intra = _dot(att, v)
        v_bd = jnp.concatenate([v] * nb, axis=1) * vmask_ref[:n, :nb * dv]
        d_all = _dot(k_end_t, v_bd)
        ends_t = jnp.concatenate(ends + [jnp.zeros((dk - nb, dk), F32)], axis=0).T
        state = s_ref[...]
        outs = [None] * nb
        for m in (range(nb) if d == 0 else range(nb - 1, -1, -1)):
            sl = slice(m * c, (m + 1) * c)
            outs[m] = o_intra[sl, :] + _dot(q_dec[sl, :], state.astype(BF16))
            decay = jnp.broadcast_to(jnp.exp(ends_t[:, m:m + 1]), state.shape)
            state = decay * state + d_all[:, m * dv:(m + 1) * dv]
        s_ref[...] = state
        return jnp.concatenate(outs, axis=0)

    def finish(o, g):
        y = o * lax.rsqrt(jnp.mean(o * o, axis=-1, keepdims=True) + EPS) * nw_ref[...]
        return (y * _silu(g.astype(F32))).astype(o_ref.dtype)

    lat = (q_ref, k_ref, v_ref, lr_ref)
    ctx = (qc_ref, kc_ref, vc_ref, lrc_ref)

    def fwd_ctx(i, carry):
        n = nb_ctx * c
        accc_ref[pl.ds(pl.multiple_of(i * n, n), n), :] = block(ctx, i, nb_ctx, 0, False)
        return carry

    def fwd_lat(i, carry):
        n = nb_lat * c
        acc_ref[pl.ds(pl.multiple_of(i * n, n), n), :] = block(lat, i, nb_lat, 0, True)
        return carry

    def bwd_ctx(j, carry):
        i = n_ctx // nb_ctx - 1 - j
        n = nb_ctx * c
        rows = pl.ds(pl.multiple_of(i * n, n), n)
        oc_ref[rows, :] = finish(accc_ref[rows, :] + block(ctx, i, nb_ctx, 1, False), gc_ref[rows, :])
        return carry

    def bwd_lat(j, carry):
        i = n_lat // nb_lat - 1 - j
        n = nb_lat * c
        rows = pl.ds(pl.multiple_of(i * n, n), n)
        o_ref[rows, :] = finish(acc_ref[rows, :] + block(lat, i, nb_lat, 1, True), g_ref[rows, :])
        return carry

    s_ref[...] = jnp.zeros_like(s_ref)
    lax.fori_loop(0, n_ctx // nb_ctx, fwd_ctx, 0)
    lax.fori_loop(0, n_lat // nb_lat, fwd_lat, 0, unroll=2)
    s_ref[...] = jnp.zeros_like(s_ref)
    lax.fori_loop(0, n_ctx // nb_ctx, bwd_ctx, 0)
    lax.fori_loop(0, n_lat // nb_lat, bwd_lat, 0, unroll=2)


def _gla(proj, lr, wg_pad, bg, norm_w, n_batch, seq, ctx_len, n_heads, dk, dv, col_q):
    n_lat = seq // GLA_CHUNK
    n_ctx = ctx_len // GLA_CHUNK
    assert dk == LANES and dv % dk == 0
    assert n_lat % min(GLA_BLOCK_CHUNKS, n_lat) == 0 and n_ctx % min(GLA_BLOCK_CHUNKS, n_ctx) == 0
    cb = n_batch * seq // ctx_len
    vq = dv // dk
    col_v = (col_q + 2 * n_heads) // vq
    col_g = col_v + n_heads
    cos, sin = _rope_tables(seq, dk)
    nblk = GLA_BLOCK_CHUNKS * GLA_CHUNK
    r = jnp.arange(nblk)
    same_chunk = (r[:, None] // GLA_CHUNK) == (r[None, :] // GLA_CHUNK)
    keep = jnp.stack([same_chunk & (r[:, None] >= r[None, :]), same_chunk & (r[:, None] <= r[None, :])])
    keep = keep.astype(F32)
    vmask = ((r[:, None] // GLA_CHUNK) == (jnp.arange(GLA_BLOCK_CHUNKS * dv)[None, :] // dv)).astype(BF16)

    def spec(rows, width, row_block, col):
        return pl.BlockSpec((rows, width), lambda b, h: (row_block(b), col(h)))
    lat_rb = lambda b: b
    ctx_rb = lambda b: cb + b
    in_specs = []
    for rows, rb in ((seq, lat_rb), (ctx_len, ctx_rb)):
        in_specs += [spec(rows, dk, rb, lambda h: col_q + h),
                     spec(rows, dk, rb, lambda h: col_q + n_heads + h),
                     spec(rows, dv, rb, lambda h: col_v + h),
                     spec(rows, dv, rb, lambda h: col_g + h),
                     spec(rows, LANES, rb, lambda h: 0)]
    in_specs += [pl.BlockSpec((2, LANES, dk), lambda b, h: (0, 0, h)),
                 pl.BlockSpec((2, dk), lambda b, h: (0, h)),
                 pl.BlockSpec((1, dv), lambda b, h: (0, 0)),
                 pl.BlockSpec((seq, dk), lambda b, h: (0, 0)),
                 pl.BlockSpec((seq, dk), lambda b, h: (0, 0)),
                 pl.BlockSpec((2, nblk, nblk), lambda b, h: (0, 0, 0)),
                 pl.BlockSpec((2, nblk, nblk), lambda b, h: (0, 0, 0)),
                 pl.BlockSpec((nblk, GLA_BLOCK_CHUNKS * dv), lambda b, h: (0, 0))]
    return pl.pallas_call(
        functools.partial(_gla_kernel, n_lat=n_lat, n_ctx=n_ctx),
        grid=(n_batch, n_heads), in_specs=in_specs,
        out_specs=[pl.BlockSpec((seq, dv), lambda b, h: (b, h)),
                   pl.BlockSpec((ctx_len, dv), lambda b, h: (b, h))],
        out_shape=[jax.ShapeDtypeStruct((n_batch * seq, n_heads * dv), BF16),
                   jax.ShapeDtypeStruct((n_batch * ctx_len, n_heads * dv), BF16)],
        scratch_shapes=[pltpu.VMEM((dk, dv), F32), pltpu.VMEM((seq, dv), F32),
                        pltpu.VMEM((ctx_len, dv), F32)],
        compiler_params=_params("parallel", "parallel"), name="gla",
    )(proj, proj, proj, proj, lr, proj, proj, proj, proj, lr,
      wg_pad, bg, norm_w.reshape(1, dv), cos, sin, keep.astype(BF16), keep, vmask)


def _lru_pitch(n):
    seg = n // SUBLANES
    assert seg % SUBLANES == 0
    return seg + 4


def _lru_kernel(x_ref, g_ref, xc_ref, gc_ref, cw_ref, cb_ref, w4_ref, b4_ref, lam_ref, o_ref, oc_ref,
                xpad_ref, af_ref, cf_ref, ab_ref, cbk_ref, hf_ref, pf_ref, hb_ref, pb_ref, *, seq, ctx_len):
    bw = x_ref.shape[1]
    pad = SUBLANES
    sp = _softplus(-lam_ref[...])
    cw = cw_ref[...]
    w4 = w4_ref[...]
    b4 = b4_ref[...]

    def put(ref, n, t0, val):
        seg, pitch = n // SUBLANES, _lru_pitch(n)
        for s in range(SUBLANES):
            lo, hi = max(t0, s * seg), min(t0 + val.shape[0], (s + 1) * seg)
            if lo < hi:
                ref[s * pitch + lo - s * seg:s * pitch + hi - s * seg, :] = val[lo - t0:hi - t0, :]

    def coeffs(src_ref, n):
        xpad_ref[0:pad, :] = jnp.zeros((pad, bw), F32)
        xpad_ref[pad + n:pad + n + pad, :] = jnp.zeros((pad, bw), F32)
        tile = min(n, 512)
        for t0 in range(0, n, tile):
            xpad_ref[pad + t0:pad + t0 + tile, :] = src_ref[t0:t0 + tile, :].astype(F32)
        for t0 in range(0, n, tile):
            xc = cb_ref[...]
            for j in range(LRU_CONV):
                lo = pad + t0 + j - LRU_CONV // 2
                xc = xc + cw[j:j + 1, :] * xpad_ref[lo:lo + tile, :]
            z = _dot(xc.astype(BF16), w4) + b4
            for d, (a_ref, c_ref) in enumerate(((af_ref, cf_ref), (ab_ref, cbk_ref))):
                r = jax.nn.sigmoid(z[:, (2 * d) * bw:(2 * d + 1) * bw])
                i = jax.nn.sigmoid(z[:, (2 * d + 1) * bw:(2 * d + 2) * bw])
                log_a = (-LRU_C) * r * sp[d:d + 1, :]
                a = jnp.exp(log_a)
                put(a_ref, n, t0, a)
                put(c_ref, n, t0, jnp.sqrt(1.0 - a * a) * (i * xc))

    def scan(n, h0_f, h0_b):
        seg, pitch = n // SUBLANES, _lru_pitch(n)

        def step(g, carry):
            hf, pf, hb, pb = carry
            fwd = pl.ds(g, SUBLANES, stride=pitch)
            bwd = pl.ds(seg - 1 - g, SUBLANES, stride=pitch)
            a = af_ref[fwd, :]
            hf = a * hf + cf_ref[fwd, :]
            pf = a * pf
            hf_ref[fwd, :] = hf
            pf_ref[fwd, :] = pf
            a = ab_ref[bwd, :]
            hb = a * hb + cbk_ref[bwd, :]
            pb = a * pb
            hb_ref[bwd, :] = hb
            pb_ref[bwd, :] = pb
            return hf, pf, hb, pb

        zero = jnp.zeros((SUBLANES, bw), F32)
        one = jnp.ones((SUBLANES, bw), F32)
        hf, pf, hb, pb = lax.fori_loop(0, seg, step, (zero, one, zero, one), unroll=8)
        carry_f = [h0_f]
        for s in range(SUBLANES):
            carry_f.append(pf[s:s + 1, :] * carry_f[s] + hf[s:s + 1, :])
        carry_b = [h0_b]
        for s in range(SUBLANES - 1, -1, -1):
            carry_b.append(pb[s:s + 1, :] * carry_b[-1] + hb[s:s + 1, :])
        carry_b = carry_b[::-1]
        return carry_f, carry_b

    def emit(n, carry_f, carry_b, gate_ref, out_ref):
        seg, pitch = n // SUBLANES, _lru_pitch(n)
        for s in range(SUBLANES):
            rows = slice(s * seg, (s + 1) * seg)
            held = slice(s * pitch, s * pitch + seg)
            h = (hf_ref[held, :] + pf_ref[held, :] * carry_f[s]
                 + hb_ref[held, :] + pb_ref[held, :] * carry_b[s + 1])
            out_ref[rows, :] = (h * _gelu_tanh(gate_ref[rows, :].astype(F32))).astype(out_ref.dtype)

    zero_h = jnp.zeros((1, bw), F32)
    coeffs(xc_ref, ctx_len)
    cf, cbw = scan(ctx_len, zero_h, zero_h)
    emit(ctx_len, cf, cbw, gc_ref, oc_ref)
    coeffs(x_ref, seq)
    lf, lb = scan(seq, cf[SUBLANES], cbw[0])
    emit(seq, lf, lb, g_ref, o_ref)


def _rglru(proj, conv_w, conv_b, w4, b4, lam, n_batch, seq, ctx_len, col_x):
    n_blk = w4.shape[0]
    bw = LANES
    cb = n_batch * seq // ctx_len
    width = n_blk * bw
    return pl.pallas_call(
        functools.partial(_lru_kernel, seq=seq, ctx_len=ctx_len),
        grid=(n_batch, n_blk),
        in_specs=[pl.BlockSpec((seq, bw), lambda b, j: (b, col_x + j)),
                  pl.BlockSpec((seq, bw), lambda b, j: (b, col_x + n_blk + j)),
                  pl.BlockSpec((ctx_len, bw), lambda b, j: (cb + b, col_x + j)),
                  pl.BlockSpec((ctx_len, bw), lambda b, j: (cb + b, col_x + n_blk + j)),
                  pl.BlockSpec((LRU_CONV, bw), lambda b, j: (0, j)),
                  pl.BlockSpec((1, bw), lambda b, j: (0, j)),
                  pl.BlockSpec((None, bw, 4 * bw), lambda b, j: (j, 0, 0)),
                  pl.BlockSpec((None, 1, 4 * bw), lambda b, j: (j, 0, 0)),
                  pl.BlockSpec((2, bw), lambda b, j: (0, j))],
        out_specs=[pl.BlockSpec((seq, bw), lambda b, j: (b, j)),
                   pl.BlockSpec((ctx_len, bw), lambda b, j: (b, j))],
        out_shape=[jax.ShapeDtypeStruct((n_batch * seq, width), BF16),
                   jax.ShapeDtypeStruct((n_batch * ctx_len, width), BF16)],
        scratch_shapes=([pltpu.VMEM((seq + 2 * SUBLANES, bw), F32)]
                        + [pltpu.VMEM((SUBLANES * _lru_pitch(seq), bw), F32)] * 8),
        compiler_params=_params("parallel", "parallel"), name="rglru",
    )(proj, proj, proj, proj, conv_w, conv_b.reshape(1, width), w4, b4, lam)


def _row_tile(n_rows, cap):
    t = cap
    while n_rows % t:
        t //= 2
    return t


def kernel(x, c, ctx, c_ctx, w_mod, b_mod, norm_mix, norm_ffn, w_in, na_rpb, gla_wg, gla_bg, gla_norm,
           conv_w, conv_b, lru_wa, lru_ba, lru_wx, lru_bx, lru_lam, w_out, ffd_w1, ffd_w3, ffd_w2,
           router, moe_w1, moe_w3, moe_w2, final_norm):
    n_batch, seq, d = x.shape
    ctx_len = ctx.shape[1]
    depth = w_in.shape[0]
    n_lat = n_batch * seq
    n_tot = n_lat + n_batch * ctx_len
    na_heads = na_rpb.shape[1]
    na_w = na_heads * LANES
    gla_kw = gla_wg.shape[3]
    gla_heads = gla_kw // LANES
    gla_dv = gla_norm.shape[1]
    gla_vw = gla_heads * gla_dv
    lru_w = conv_w.shape[2]
    n_blk = lru_wa.shape[2]
    assert n_batch < SUBLANES and lru_w == n_blk * LANES and seq % (SUBLANES * GRID_W) == 0

    p_lr = 3 * na_w + 2 * gla_kw + 2 * gla_vw
    col_gla = 3 * na_w // LANES
    col_lru = p_lr // LANES

    tm = _row_tile(n_tot, 1024)
    tm_lat = _row_tile(n_lat, 1024)
    tr = _row_tile(n_tot, 256)

    streams = [x.reshape(n_lat, d), ctx.reshape(n_batch * ctx_len, d)]
    c8 = jnp.zeros((SUBLANES, d), F32).at[:n_batch].set(c).at[n_batch].set(c_ctx)

    for l in range(depth):
        last = l == depth - 1
        rows_out = n_lat if last else n_tot
        tmo = tm_lat if last else tm
        mod3 = _modulation(c8, w_mod, b_mod, l).reshape(SUBLANES * N_MOD, 1, d)

        h = _norm_mod(streams, norm_mix[l], mod3, 1, 0, n_tot, seq, n_batch, tr)
        w_l = w_in[l]
        w_main = jnp.concatenate([w_l[:, :p_lr], w_l[:, p_lr + 2 * GLA_LOWRANK:]], axis=1).astype(BF16)
        w_lr = jnp.pad(w_l[:, p_lr:p_lr + 2 * GLA_LOWRANK], ((0, 0), (0, LANES - 2 * GLA_LOWRANK))).astype(BF16)
        proj = _matmul(h, w_main, BF16, tm, _row_tile(w_main.shape[1], 512))
        lr = _matmul(h, w_lr, F32, tm, LANES)

        na_l, na_c = _neighbourhood_attention(proj, _na_bias_table(na_rpb[l]), n_batch, seq, ctx_len,
                                              na_heads, 0)
        wg = gla_wg[l]
        wg_pad = jnp.zeros((2, LANES, gla_kw), F32)
        wg_pad = wg_pad.at[0, :GLA_LOWRANK].set(wg[0]).at[1, GLA_LOWRANK:2 * GLA_LOWRANK].set(wg[1])
        gla_l, gla_c = _gla(proj, lr, wg_pad.astype(BF16), gla_bg[l], gla_norm[l], n_batch, seq, ctx_len,
                            gla_heads, LANES, gla_dv, col_gla)
        w4 = jnp.concatenate([lru_wa[l, 0], lru_wx[l, 0], lru_wa[l, 1], lru_wx[l, 1]], axis=-1).astype(BF16)
        b4 = jnp.concatenate([lru_ba[l, 0].reshape(n_blk, 1, LANES), lru_bx[l, 0].reshape(n_blk, 1, LANES),
                              lru_ba[l, 1].reshape(n_blk, 1, LANES), lru_bx[l, 1].reshape(n_blk, 1, LANES)],
                             axis=-1)
        lru_l, lru_c = _rglru(proj, conv_w[l], conv_b[l], w4, b4, lru_lam[l], n_batch, seq, ctx_len, col_lru)

        mixed = [[na_l, gla_l, lru_l]] + ([] if last else [[na_c, gla_c, lru_c]])
        tk = _row_tile(min(a.shape[1] for a in mixed[0]), 1024)
        x_all = _out_proj(mixed, w_out[l].astype(BF16), streams, mod3, 2, seq, n_batch, _row_tile(tm, 512),
                          _row_tile(d, 1024), tk)

        j = l // 2
        if l % 2 == 0:
            h = _norm_mod([x_all], norm_ffn[l], mod3, 4, 3, rows_out, seq, n_batch, tr)
            act = _swiglu_in(h, ffd_w1[j].astype(BF16), ffd_w3[j].astype(BF16), tmo,
                             _row_tile(ffd_w1.shape[2], 512))
            x_all = _matmul_residual(act, ffd_w2[j].astype(BF16), x_all, mod3, 5, rows_out, seq, n_batch, tmo,
                                     _row_tile(d, 1024), _row_tile(act.shape[1], 2048))
        else:
            h, route = _norm_mod([x_all], norm_ffn[l], mod3, 4, 3, rows_out, seq, n_batch, tr, router=router[j])
            x_all = _moe(h, route, moe_w1[j].astype(BF16), moe_w3[j].astype(BF16), moe_w2[j].astype(BF16),
                         x_all, mod3, 5, rows_out, seq, n_batch, tn=_row_tile(moe_w1.shape[3], 512),
                         tc=_row_tile(rows_out, 256), out_norm_w=final_norm if last else None)
        streams = [x_all]

    out = x_all if depth % 2 == 0 else _final_norm(x_all, final_norm, n_lat, tr)
    return out.reshape(n_batch, seq, d)
```

```python
import functools

import jax
import jax.numpy as jnp
from jax import lax
from jax.experimental import pallas as pl
from jax.experimental.pallas import tpu as pltpu

F32 = jnp.float32
BF16 = jnp.bfloat16

EPS = 1e-6
ROPE_BASE = 10000.0
GRID_W = 64
NA_WIN_ROWS = 8
NA_WIN_COLS = 16
GLA_CHUNK = 64
GLA_LOWRANK = 16
GLA_NORMALIZER = 16.0
LRU_C = 8.0
LRU_CONV = 4
N_MOD = 6
TOP_K = 2
LANES = 128
SUBLANES = 8
VMEM_LIMIT = 56 * 1024 * 1024
NEG_BIG = -1e30

_NT = (((1,), (1,)), ((), ()))


def _params(*sem):
    return pltpu.CompilerParams(dimension_semantics=sem, vmem_limit_bytes=VMEM_LIMIT)


def _dot(a, b):
    return jnp.dot(a, b, preferred_element_type=F32)


def _split_bf16(x):
    hi = x.astype(BF16)
    lo = (x - hi.astype(F32)).astype(BF16)
    return hi, lo


def _softplus(x):
    return jnp.maximum(x, 0.0) + jnp.log1p(jnp.exp(-jnp.abs(x)))


def _silu(x):
    return x * jax.nn.sigmoid(x)


def _gelu_tanh(x):
    return 0.5 * x * (1.0 + jnp.tanh(0.7978845608028654 * (x + 0.044715 * (x * x * x))))


def _mod_kernel(c_ref, w_ref, b_ref, o_ref):
    c = c_ref[...]
    a_hi, a_lo = _split_bf16(_silu(c))
    w_hi, w_lo = _split_bf16(w_ref[...])
    o_ref[...] = _dot(a_hi, w_hi) + _dot(a_lo, w_hi) + _dot(a_hi, w_lo) + b_ref[...]


def _modulation(c8, w_mod, b_mod, layer, tn=512):
    d = c8.shape[1]
    n = w_mod.shape[2]
    return pl.pallas_call(
        _mod_kernel,
        grid=(n // tn,),
        in_specs=[pl.BlockSpec((SUBLANES, d), lambda j: (0, 0)),
                  pl.BlockSpec((None, d, tn), lambda j: (layer, 0, j)),
                  pl.BlockSpec((None, 1, tn), lambda j: (layer, 0, j))],
        out_specs=pl.BlockSpec((SUBLANES, tn), lambda j: (0, j)),
        out_shape=jax.ShapeDtypeStruct((SUBLANES, n), F32),
        compiler_params=_params("parallel"),
        name="modulation",
    )(c8, w_mod, b_mod.reshape(b_mod.shape[0], 1, n))


def _mod_spec(which, tile_rows, seq, n_batch, d_block, col_axis=None):
    def row(i):
        return jnp.minimum((i * tile_rows) // seq, n_batch) * N_MOD + which
    if col_axis is None:
        return pl.BlockSpec((None, 1, d_block), lambda i, *_: (row(i), 0, 0))
    return pl.BlockSpec((None, 1, d_block), lambda *g: (row(g[0]), 0, g[col_axis]))


def _norm_mod_body(x_ref, nw_ref, sc_ref, sh_ref):
    x = x_ref[...]
    y = x * lax.rsqrt(jnp.mean(x * x, axis=-1, keepdims=True) + EPS) * nw_ref[...]
    return y * (1.0 + sc_ref[...]) + sh_ref[...]


def _row_group_specs(group_rows, tile, cols, col_index):
    specs, bounds = [], [0]
    for rows in group_rows:
        first, n_tiles = bounds[-1], rows // tile
        specs.append(pl.BlockSpec(
            (tile, cols), lambda i, *g, first=first, n_tiles=n_tiles: (jnp.clip(i - first, 0, n_tiles - 1),
                                                                        col_index(*g))))
        bounds.append(first + n_tiles)
    return specs, bounds


def _in_group(i, bounds, g):
    return jnp.logical_and(i >= bounds[g], i < bounds[g + 1])


def _norm_mod_kernel(*refs, bounds):
    n_groups = len(bounds) - 1
    nw_ref, sc_ref, sh_ref, o_ref = refs[n_groups:]
    for g in range(n_groups):
        @pl.when(_in_group(pl.program_id(0), bounds, g))
        def _(g=g):
            o_ref[...] = _norm_mod_body(refs[g], nw_ref, sc_ref, sh_ref).astype(o_ref.dtype)


def _norm_route_kernel(x_ref, nw_ref, sc_ref, sh_ref, rhi_ref, rlo_ref, o_ref, g_ref, *, n_experts):
    h = _norm_mod_body(x_ref, nw_ref, sc_ref, sh_ref)
    o_ref[...] = h.astype(o_ref.dtype)
    h_hi, h_lo = _split_bf16(h)
    logits = _dot(h_hi, rhi_ref[...]) + _dot(h_lo, rhi_ref[...]) + _dot(h_hi, rlo_ref[...])
    lane = lax.broadcasted_iota(jnp.int32, logits.shape, 1)
    valid = lane < n_experts
    logits = jnp.where(valid, logits, NEG_BIG)
    e = jnp.exp(logits - jnp.max(logits, axis=-1, keepdims=True))
    probs = jnp.where(valid, e / jnp.sum(e, axis=-1, keepdims=True), -1.0)
    p1 = jnp.max(probs, axis=-1, keepdims=True)
    i1 = jnp.min(jnp.where(probs == p1, lane, LANES), axis=-1, keepdims=True)
    rest = jnp.where(lane == i1, -1.0, probs)
    p2 = jnp.max(rest, axis=-1, keepdims=True)
    i2 = jnp.min(jnp.where(rest == p2, lane, LANES), axis=-1, keepdims=True)
    den = p1 + p2
    g_ref[...] = (jnp.where(lane == 0, p1 / den, 0.0) + jnp.where(lane == 1, p2 / den, 0.0)
                  + jnp.where(lane == 2, i1.astype(F32), 0.0) + jnp.where(lane == 3, i2.astype(F32), 0.0))


def _norm_mod(streams, norm_w, mod3, which_scale, which_shift, n_rows, seq, n_batch, tr=256,
              router=None):
    d = streams[0].shape[1]
    group_rows = [n_rows] if len(streams) == 1 else [s.shape[0] for s in streams]
    assert sum(group_rows) == n_rows
    x_specs, bounds = _row_group_specs(group_rows, tr, d, lambda: 0)
    in_specs = x_specs + [pl.BlockSpec((1, d), lambda i: (0, 0)),
                          _mod_spec(which_scale, tr, seq, n_batch, d),
                          _mod_spec(which_shift, tr, seq, n_batch, d)]
    h_spec = pl.BlockSpec((tr, d), lambda i: (i, 0))
    h_shape = jax.ShapeDtypeStruct((n_rows, d), BF16)
    if router is None:
        return pl.pallas_call(
            functools.partial(_norm_mod_kernel, bounds=bounds), grid=(n_rows // tr,), in_specs=in_specs,
            out_specs=h_spec, out_shape=h_shape, compiler_params=_params("parallel"), name="norm_mod",
        )(*streams, norm_w.reshape(1, d), mod3, mod3)
    (x_all,) = streams
    n_experts = router.shape[1]
    r_pad = jnp.pad(router, ((0, 0), (0, LANES - n_experts)))
    r_hi = r_pad.astype(BF16)
    r_lo = (r_pad - r_hi.astype(F32)).astype(BF16)
    w_spec = pl.BlockSpec((d, LANES), lambda i: (0, 0))
    return pl.pallas_call(
        functools.partial(_norm_route_kernel, n_experts=n_experts),
        grid=(n_rows // tr,), in_specs=in_specs + [w_spec, w_spec],
        out_specs=[h_spec, pl.BlockSpec((tr, LANES), lambda i: (i, 0))],
        out_shape=[jax.ShapeDtypeStruct((n_rows, d), F32), jax.ShapeDtypeStruct((n_rows, LANES), F32)],
        compiler_params=_params("parallel"), name="norm_route",
    )(x_all, norm_w.reshape(1, d), mod3, mod3, r_hi, r_lo)


def _final_norm_kernel(x_ref, nw_ref, o_ref):
    x = x_ref[...]
    o_ref[...] = x * lax.rsqrt(jnp.mean(x * x, axis=-1, keepdims=True) + EPS) * nw_ref[...]


def _final_norm(x_all, norm_w, n_rows, tr=256):
    d = x_all.shape[1]
    return pl.pallas_call(
        _final_norm_kernel, grid=(n_rows // tr,),
        in_specs=[pl.BlockSpec((tr, d), lambda i: (i, 0)), pl.BlockSpec((1, d), lambda i: (0, 0))],
        out_specs=pl.BlockSpec((tr, d), lambda i: (i, 0)),
        out_shape=jax.ShapeDtypeStruct((n_rows, d), F32),
        compiler_params=_params("parallel"), name="final_norm",
    )(x_all, norm_w.reshape(1, d))


def _mm_kernel(a_ref, w_ref, o_ref):
    o_ref[...] = _dot(a_ref[...], w_ref[...]).astype(o_ref.dtype)


def _matmul(a, w, out_dtype, tm, tn):
    m, k = a.shape
    n = w.shape[1]
    return pl.pallas_call(
        _mm_kernel, grid=(m // tm, n // tn),
        in_specs=[pl.BlockSpec((tm, k), lambda i, j: (i, 0)),
                  pl.BlockSpec((k, tn), lambda i, j: (0, j))],
        out_specs=pl.BlockSpec((tm, tn), lambda i, j: (i, j)),
        out_shape=jax.ShapeDtypeStruct((m, n), out_dtype),
        compiler_params=_params("parallel", "parallel"), name="matmul",
    )(a, w)


def _mm_res_kernel(a_ref, w_ref, x_ref, g_ref, o_ref, acc_ref, *, nk):
    k = pl.program_id(2)

    @pl.when(k == 0)
    def _():
        acc_ref[...] = jnp.zeros_like(acc_ref)

    acc_ref[...] += _dot(a_ref[...], w_ref[...])

    @pl.when(k == nk - 1)
    def _():
        o_ref[...] = x_ref[...] + g_ref[...] * acc_ref[...]


def _matmul_residual(a, w, x_all, mod3, which_gate, n_rows, seq, n_batch, tm, tn, tk):
    k = a.shape[1]
    d = w.shape[1]
    nk = k // tk
    return pl.pallas_call(
        functools.partial(_mm_res_kernel, nk=nk), grid=(n_rows // tm, d // tn, nk),
        in_specs=[pl.BlockSpec((tm, tk), lambda i, j, kk: (i, kk)),
                  pl.BlockSpec((tk, tn), lambda i, j, kk: (kk, j)),
                  pl.BlockSpec((tm, tn), lambda i, j, kk: (i, j)),
                  _mod_spec(which_gate, tm, seq, n_batch, tn, col_axis=1)],
        out_specs=pl.BlockSpec((tm, tn), lambda i, j, kk: (i, j)),
        out_shape=jax.ShapeDtypeStruct((n_rows, d), F32),
        scratch_shapes=[pltpu.VMEM((tm, tn), F32)],
        compiler_params=_params("parallel", "parallel", "arbitrary"), name="matmul_residual",
    )(a, w, x_all, mod3)


def _out_proj_kernel(*refs, n_groups, widths, bounds, n_x):
    n_src = len(widths)
    w_ref = refs[n_groups * n_src]
    x_refs = refs[n_groups * n_src + 1:n_groups * n_src + 1 + n_x]
    g_ref, o_ref = refs[-2:]
    i = pl.program_id(0)
    for g in range(n_groups):
        @pl.when(_in_group(i, bounds, g))
        def _(g=g):
            acc, k0 = None, 0
            for s in range(n_src):
                part = _dot(refs[g * n_src + s][...], w_ref[k0:k0 + widths[s], :])
                acc = part if acc is None else acc + part
                k0 += widths[s]
            x_ref = x_refs[0] if n_x == 1 else x_refs[g]
            o_ref[...] = x_ref[...] + g_ref[...] * acc


def _out_proj(sources, w, streams, mod3, which_gate, seq, n_batch, tm, tn):
    k, d = w.shape
    group_rows = [grp[0].shape[0] for grp in sources]
    widths = [a.shape[1] for a in sources[0]]
    assert sum(widths) == k
    n_rows = sum(group_rows)
    a_specs, args = [], []
    for g, grp in enumerate(sources):
        for a in grp:
            specs, bounds = _row_group_specs(group_rows, tm, a.shape[1], lambda j: 0)
            a_specs.append(specs[g])
            args.append(a)
    if len(streams) == 1:
        x_specs = [pl.BlockSpec((tm, tn), lambda i, j: (i, j))]
    else:
        assert [s.shape[0] for s in streams] == group_rows
        x_specs, _ = _row_group_specs(group_rows, tm, tn, lambda j: j)
    return pl.pallas_call(
        functools.partial(_out_proj_kernel, n_groups=len(sources), widths=widths, bounds=bounds,
                          n_x=len(streams)),
        grid=(n_rows // tm, d // tn),
        in_specs=a_specs + [pl.BlockSpec((k, tn), lambda i, j: (0, j))] + x_specs
        + [_mod_spec(which_gate, tm, seq, n_batch, tn, col_axis=1)],
        out_specs=pl.BlockSpec((tm, tn), lambda i, j: (i, j)),
        out_shape=jax.ShapeDtypeStruct((n_rows, d), F32),
        compiler_params=_params("parallel", "parallel"), name="out_proj",
    )(*args, w, *streams, mod3)


def _swiglu_kernel(a_ref, w1_ref, w3_ref, o_ref):
    a = a_ref[...]
    o_ref[...] = (_silu(_dot(a, w1_ref[...])) * _dot(a, w3_ref[...])).astype(o_ref.dtype)


def _swiglu_in(a, w1, w3, tm, tn):
    m, k = a.shape
    f = w1.shape[1]
    return pl.pallas_call(
        _swiglu_kernel, grid=(m // tm, f // tn),
        in_specs=[pl.BlockSpec((tm, k), lambda i, j: (i, 0)),
                  pl.BlockSpec((k, tn), lambda i, j: (0, j)),
                  pl.BlockSpec((k, tn), lambda i, j: (0, j))],
        out_specs=pl.BlockSpec((tm, tn), lambda i, j: (i, j)),
        out_shape=jax.ShapeDtypeStruct((m, f), BF16),
        compiler_params=_params("parallel", "parallel"), name="swiglu_in",
    )(a, w1, w3)


MOE_TILE = 512


def _moe_dispatch(route, n_experts, tile):
    n = route.shape[0]
    e = jnp.concatenate([route[:, 2], route[:, 3]]).astype(jnp.int32)
    onehot = (e[:, None] == jnp.arange(n_experts, dtype=jnp.int32)[None, :]).astype(jnp.int32)
    csum = jnp.cumsum(onehot, axis=0)
    counts = csum[-1]
    rank = jnp.sum((csum - onehot) * onehot, axis=1)
    padded = (counts + tile - 1) // tile * tile
    ends = jnp.cumsum(padded)
    pos = jnp.sum(onehot * (ends - padded)[None, :], axis=1) + rank
    n_tiles = (TOP_K * n) // tile + n_experts
    tok = jnp.tile(jnp.arange(n, dtype=jnp.int32), TOP_K)
    src = jnp.zeros((n_tiles * tile,), jnp.int32).at[pos].set(tok, unique_indices=True)
    tile_id = jnp.arange(n_tiles, dtype=jnp.int32)
    tile_expert = jnp.sum((tile_id[:, None] * tile >= ends[None, :]).astype(jnp.int32), axis=1)
    n_used = ends[-1] // tile
    tile_expert = jnp.minimum(tile_expert, jnp.max(jnp.where(tile_id < n_used, tile_expert, 0)))
    return src.reshape(n_tiles, 1, tile), pos, tile_expert.astype(jnp.int32), n_used.reshape(1).astype(jnp.int32)


def _moe_in_kernel(te_ref, nu_ref, src_ref, nxt_ref, h_hbm, w1_ref, w3_ref, o_ref, rows_ref, a_ref, sem, *, tile):
    i = pl.program_id(0)
    j = pl.program_id(1)
    used = i < nu_ref[0]
    slot = i % 2

    def gather(idx_ref, s):
        def issue(p, carry):
            pltpu.make_async_copy(h_hbm.at[pl.ds(idx_ref[0, p], 1), :], rows_ref.at[s, pl.ds(p, 1), :],
                                  sem.at[s]).start()
            return carry
        lax.fori_loop(0, tile, issue, 0, unroll=8)

    @pl.when(jnp.logical_and(i == 0, j == 0))
    def _():
        gather(src_ref, 0)

    @pl.when(jnp.logical_and(i + 1 < nu_ref[0], j == 0))
    def _():
        gather(nxt_ref, 1 - slot)

    @pl.when(jnp.logical_and(used, j == 0))
    def _():
        pltpu.make_async_copy(h_hbm.at[pl.ds(0, tile), :], rows_ref.at[slot], sem.at[slot]).wait()
        a_ref[...] = rows_ref[slot].astype(BF16)

    @pl.when(used)
    def _():
        a = a_ref[...]
        o_ref[...] = (_silu(_dot(a, w1_ref[...])) * _dot(a, w3_ref[...])).astype(o_ref.dtype)

    @pl.when(jnp.logical_not(used))
    def _():
        o_ref[...] = jnp.zeros_like(o_ref)


def _moe_out_kernel(te_ref, nu_ref, a_ref, w_ref, o_ref):
    used = pl.program_id(0) < nu_ref[0]

    @pl.when(used)
    def _():
        o_ref[...] = _dot(a_ref[...], w_ref[...])

    @pl.when(jnp.logical_not(used))
    def _():
        o_ref[...] = jnp.zeros_like(o_ref)


def _moe_combine_kernel(pos_ref, y_hbm, x_ref, g_ref, route_ref, *rest, tc, normed):
    o_ref, rows_ref, sem = rest[-3:]

    def issue(t, carry):
        for s in range(TOP_K):
            pltpu.make_async_copy(y_hbm.at[pl.ds(pos_ref[0, s * tc + t], 1), :],
                                  rows_ref.at[s, pl.ds(t, 1), :], sem).start()
        return carry
    lax.fori_loop(0, tc, issue, 0, unroll=8)
    for s in range(TOP_K):
        pltpu.make_async_copy(y_hbm.at[pl.ds(0, tc), :], rows_ref.at[s], sem).wait()
    route = route_ref[...]
    moe = route[:, 0:1] * rows_ref[0] + route[:, 1:2] * rows_ref[1]
    o = x_ref[...] + g_ref[...] * moe
    if normed:
        o = o * lax.rsqrt(jnp.mean(o * o, axis=-1, keepdims=True) + EPS) * rest[0][...]
    o_ref[...] = o


def _moe(h, route, w1, w3, w2, x_all, mod3, which_gate, n_rows, seq, n_batch, tn=512, tc=256, out_norm_w=None):
    n, d = h.shape
    n_experts, _, f = w1.shape
    tile = MOE_TILE
    src, pos, tile_expert, n_used = _moe_dispatch(route, n_experts, tile)
    n_tiles = src.shape[0]
    smem = functools.partial(pl.BlockSpec, memory_space=pltpu.SMEM)
    act = pl.pallas_call(
        functools.partial(_moe_in_kernel, tile=tile),
        grid_spec=pltpu.PrefetchScalarGridSpec(
            num_scalar_prefetch=2, grid=(n_tiles, f // tn),
            in_specs=[smem((None, 1, tile), lambda i, j, te, nu: (i, 0, 0)),
                      smem((None, 1, tile), lambda i, j, te, nu: (jnp.minimum(i + 1, n_tiles - 1), 0, 0)),
                      pl.BlockSpec(memory_space=pl.ANY),
                      pl.BlockSpec((None, d, tn), lambda i, j, te, nu: (te[i], 0, j)),
                      pl.BlockSpec((None, d, tn), lambda i, j, te, nu: (te[i], 0, j))],
            out_specs=pl.BlockSpec((tile, tn), lambda i, j, te, nu: (i, j)),
            scratch_shapes=[pltpu.VMEM((2, tile, d), F32), pltpu.VMEM((tile, d), BF16),
                            pltpu.SemaphoreType.DMA((2,))]),
        out_shape=jax.ShapeDtypeStruct((n_tiles * tile, f), BF16),
        compiler_params=_params("arbitrary", "arbitrary"), name="moe_in",
    )(tile_expert, n_used, src, src, h, w1, w3)
    tn2 = _row_tile(d, 1024)
    y = pl.pallas_call(
        _moe_out_kernel,
        grid_spec=pltpu.PrefetchScalarGridSpec(
            num_scalar_prefetch=2, grid=(n_tiles, d // tn2),
            in_specs=[pl.BlockSpec((tile, f), lambda i, j, te, nu: (i, 0)),
                      pl.BlockSpec((None, f, tn2), lambda i, j, te, nu: (te[i], 0, j))],
            out_specs=pl.BlockSpec((tile, tn2), lambda i, j, te, nu: (i, j))),
        out_shape=jax.ShapeDtypeStruct((n_tiles * tile, d), F32),
        compiler_params=_params("arbitrary", "arbitrary"), name="moe_out",
    )(tile_expert, n_used, act, w2)
    pos2 = pos.reshape(TOP_K, n // tc, tc).transpose(1, 0, 2).reshape(n // tc, 1, TOP_K * tc)
    normed = out_norm_w is not None
    norm_specs = [pl.BlockSpec((1, d), lambda i: (0, 0))] if normed else []
    norm_args = [out_norm_w.reshape(1, d)] if normed else []
    return pl.pallas_call(
        functools.partial(_moe_combine_kernel, tc=tc, normed=normed),
        grid=(n_rows // tc,),
        in_specs=[smem((None, 1, TOP_K * tc), lambda i: (i, 0, 0)),
                  pl.BlockSpec(memory_space=pl.ANY),
                  pl.BlockSpec((tc, d), lambda i: (i, 0)),
                  _mod_spec(which_gate, tc, seq, n_batch, d),
                  pl.BlockSpec((tc, LANES), lambda i: (i, 0))] + norm_specs,
        out_specs=pl.BlockSpec((tc, d), lambda i: (i, 0)),
        out_shape=jax.ShapeDtypeStruct((n_rows, d), F32),
        scratch_shapes=[pltpu.VMEM((TOP_K, tc, d), F32), pltpu.SemaphoreType.DMA(())],
        compiler_params=_params("arbitrary"), name="moe_combine",
    )(pos2, y, x_all, mod3, route, *norm_args)


def _na_bias_table(rpb):
    n_heads = rpb.shape[0]
    n_dr = 2 * NA_WIN_ROWS - 1
    n_dc = 2 * NA_WIN_COLS - 1
    cols = jnp.arange(GRID_W)
    col_start = jnp.clip(cols - NA_WIN_COLS // 2, 0, GRID_W - NA_WIN_COLS)
    in_win = (cols[None, :] >= col_start[:, None]) & (cols[None, :] < col_start[:, None] + NA_WIN_COLS)
    dc = cols[None, :] - cols[:, None] + NA_WIN_COLS - 1
    expand = ((dc[None] == jnp.arange(n_dc)[:, None, None]) & in_win[None]).astype(F32)
    t = jnp.dot(rpb.reshape(-1, n_dc), expand.reshape(n_dc, GRID_W * GRID_W), precision=lax.Precision.HIGHEST)
    t = jnp.where(in_win.reshape(1, 1, -1), t.reshape(n_heads, n_dr, -1), NEG_BIG)
    t = t.reshape(n_heads, n_dr, GRID_W, GRID_W)
    masked = jnp.full((n_heads, GRID_W, GRID_W), NEG_BIG, F32)
    kinds = []
    for kind in range(3):
        q_rows = []
        for i in range(NA_QROWS):
            lo, dr0 = ((0, NA_WIN_ROWS - 1 - i), (i, NA_WIN_ROWS // 2 - 1),
                       (NA_QROWS, NA_WIN_ROWS - 1 - NA_QROWS - i))[kind]
            pieces = [t[:, dr0 + j - lo] if lo <= j < lo + NA_WIN_ROWS else masked for j in range(NA_KROWS)]
            q_rows.append(jnp.concatenate(pieces, axis=-1))
        kinds.append(jnp.concatenate(q_rows, axis=1))
    return jnp.stack(kinds, axis=1)


NA_QROWS = 4
NA_KROWS = NA_WIN_ROWS + NA_QROWS


def _na_kernel(q_ref, k_ref, v_ref, qc_ref, kc_ref, vc_ref, bias_ref, o_ref, oc_ref, *, rows, scale):
    kc = kc_ref[...]
    vc = vc_ref[...]
    nq = NA_QROWS * GRID_W
    nk = NA_KROWS * GRID_W
    n_blocks = rows // NA_QROWS

    def block(i, carry):
        base = jnp.clip(i * NA_QROWS - NA_WIN_ROWS // 2, 0, rows - NA_KROWS)
        kind = jnp.where(i == 0, 0, jnp.where(i == n_blocks - 1, 2, 1))
        q0 = pl.multiple_of(i * nq, nq)
        k0 = pl.multiple_of(base * GRID_W, nq)
        q = q_ref[pl.ds(q0, nq), :]
        kw = k_ref[pl.ds(k0, nk), :]
        vw = v_ref[pl.ds(k0, nk), :]
        s_w = lax.dot_general(q, kw, _NT, preferred_element_type=F32) * scale + bias_ref[kind]
        s_c = lax.dot_general(q, kc, _NT, preferred_element_type=F32) * scale
        m = jnp.maximum(jnp.max(s_w, axis=-1, keepdims=True), jnp.max(s_c, axis=-1, keepdims=True))
        p_w = jnp.exp(s_w - m)
        p_c = jnp.exp(s_c - m)
        den = jnp.sum(p_w, axis=-1, keepdims=True) + jnp.sum(p_c, axis=-1, keepdims=True)
        o = _dot(p_w.astype(BF16), vw) + _dot(p_c.astype(BF16), vc)
        o_ref[pl.ds(q0, nq), :] = (o / den).astype(o_ref.dtype)
        return carry

    lax.fori_loop(0, n_blocks, block, 0, unroll=2)

    s = lax.dot_general(qc_ref[...], kc, _NT, preferred_element_type=F32) * scale
    p = jnp.exp(s - jnp.max(s, axis=-1, keepdims=True))
    o = _dot(p.astype(BF16), vc) / jnp.sum(p, axis=-1, keepdims=True)
    oc_ref[...] = o.astype(oc_ref.dtype)


def _neighbourhood_attention(proj, bias_tab, n_batch, seq, ctx_len, n_heads, col0):
    dh = LANES
    rows = seq // GRID_W
    cb = n_batch * seq // ctx_len
    lat = lambda part: pl.BlockSpec((seq, dh), lambda b, h: (b, col0 + part * n_heads + h))
    ctx = lambda part: pl.BlockSpec((ctx_len, dh), lambda b, h: (cb + b, col0 + part * n_heads + h))
    return pl.pallas_call(
        functools.partial(_na_kernel, rows=rows, scale=dh ** -0.5),
        grid=(n_batch, n_heads),
        in_specs=[lat(0), lat(1), lat(2), ctx(0), ctx(1), ctx(2),
                  pl.BlockSpec((None, 3, NA_QROWS * GRID_W, NA_KROWS * GRID_W), lambda b, h: (h, 0, 0, 0))],
        out_specs=[pl.BlockSpec((seq, dh), lambda b, h: (b, h)),
                   pl.BlockSpec((ctx_len, dh), lambda b, h: (b, h))],
        out_shape=[jax.ShapeDtypeStruct((n_batch * seq, n_heads * dh), BF16),
                   jax.ShapeDtypeStruct((n_batch * ctx_len, n_heads * dh), BF16)],
        compiler_params=_params("parallel", "parallel"), name="neighbourhood_attention",
    )(proj, proj, proj, proj, proj, proj, bias_tab)


def _rope_tables(seq, dk):
    quarter = dk // 4
    inv_freq = ROPE_BASE ** (-jnp.arange(quarter, dtype=F32) / quarter)
    pos = jnp.arange(seq, dtype=jnp.int32)
    row = (pos // GRID_W).astype(F32)[:, None] * inv_freq
    col = (pos % GRID_W).astype(F32)[:, None] * inv_freq
    cos = jnp.concatenate([jnp.cos(row), jnp.cos(row), jnp.cos(col), jnp.cos(col)], axis=-1)
    sin = jnp.concatenate([-jnp.sin(row), jnp.sin(row), -jnp.sin(col), jnp.sin(col)], axis=-1)
    return cos, sin


GLA_BLOCK_CHUNKS = 4


def _gla_kernel(q_ref, k_ref, v_ref, g_ref, lr_ref, qc_ref, kc_ref, vc_ref, gc_ref, lrc_ref,
                wg_ref, bg_ref, nw_ref, cos_ref, sin_ref, tri_ref, keep_ref, vmask_ref, o_ref, oc_ref,
                s_ref, acc_ref, accc_ref, *, n_lat, n_ctx):
    dk = q_ref.shape[1]
    dv = v_ref.shape[1]
    c = GLA_CHUNK
    nb_lat = min(GLA_BLOCK_CHUNKS, n_lat)
    nb_ctx = min(GLA_BLOCK_CHUNKS, n_ctx)

    def swap_quarters(t):
        lane = lax.broadcasted_iota(jnp.int32, t.shape, 1)
        first_quarter = (lane % (dk // 2)) < (dk // 4)
        return jnp.where(first_quarter, pltpu.roll(t, dk - dk // 4, 1), pltpu.roll(t, dk // 4, 1))

    def block(refs, blk, nb, d, rope):
        qr, kr, vr, lrr = refs
        n = nb * c
        rows = pl.ds(pl.multiple_of(blk * n, n), n)
        q = qr[rows, :].astype(F32) * (dk ** -0.5)
        k = kr[rows, :].astype(F32)
        if rope:
            cos = cos_ref[rows, :]
            sin = sin_ref[rows, :]
            q = q * cos + swap_quarters(q) * sin
            k = k * cos + swap_quarters(k) * sin
        v = vr[rows, :]
        z = _dot(lrr[rows, :].astype(BF16), wg_ref[d]) + bg_ref[d:d + 1, :]
        logg = -_softplus(-z) * (1.0 / GLA_NORMALIZER)
        g1 = logg.astype(BF16)
        rem = logg - g1.astype(F32)
        g2 = rem.astype(BF16)
        g3 = (rem - g2.astype(F32)).astype(BF16)
        b3 = _dot(tri_ref[d, :n, :n], jnp.concatenate([g1, g2, g3], axis=1))
        b = b3[:, :dk] + b3[:, dk:2 * dk] + b3[:, 2 * dk:]
        end_row = c - 1 if d == 0 else 0
        ends = [b[m * c + end_row:m * c + end_row + 1, :] for m in range(nb)]
        b_end = jnp.concatenate([jnp.broadcast_to(e, (c, dk)) for e in ends], axis=0)
        q_dec = (q * jnp.exp(b)).astype(BF16)
        k_inv = (k * jnp.exp(-b)).astype(BF16)
        k_end_t = (k * jnp.exp(b_end - b)).T.astype(BF16)
        att = lax.dot_general(q_dec, k_inv, _NT, preferred_element_type=F32)
        att = jnp.where(keep_ref[d, :n, :n] != 0.0, att, 0.0).astype(BF16)
        o_intra = _dot(att, v)
        v_bd = jnp.concatenate([v] * nb, axis=1) * vmask_ref[:n, :nb * dv]
        d_all = _dot(k_end_t, v_bd)
        ends_t = jnp.concatenate(ends + [jnp.zeros((dk - nb, dk), F32)], axis=0).T
        state = s_ref[...]
        outs = [None] * nb
        for m in (range(nb) if d == 0 else range(nb - 1, -1, -1)):
            sl = slice(m * c, (m + 1) * c)
            outs[m] = o_intra[sl, :] + _dot(q_dec[sl, :], state.astype(BF16))
            decay = jnp.broadcast_to(jnp.exp(ends_t[:, m:m + 1]), state.shape)
            state = decay * state + d_all[:, m * dv:(m + 1) * dv]
        s_ref[...] = state
        return jnp.concatenate(outs, axis=0)

    def finish(o, g):
        y = o * lax.rsqrt(jnp.mean(o * o, axis=-1, keepdims=True) + EPS) * nw_ref[...]
        return (y * _silu(g.astype(F32))).astype(o_ref.dtype)

    lat = (q_ref, k_ref, v_ref, lr_ref)
    ctx = (qc_ref, kc_ref, vc_ref, lrc_ref)

    def fwd_ctx(i, carry):
        n = nb_ctx * c
        accc_ref[pl.ds(pl.multiple_of(i * n, n), n), :] = block(ctx, i, nb_ctx, 0, False)
        return carry

    def fwd_lat(i, carry):
        n = nb_lat * c
        acc_ref[pl.ds(pl.multiple_of(i * n, n), n), :] = block(lat, i, nb_lat, 0, True)
        return carry

    def bwd_ctx(j, carry):
        i = n_ctx // nb_ctx - 1 - j
        n = nb_ctx * c
        rows = pl.ds(pl.multiple_of(i * n, n), n)
        oc_ref[rows, :] = finish(accc_ref[rows, :] + block(ctx, i, nb_ctx, 1, False), gc_ref[rows, :])
        return carry

    def bwd_lat(j, carry):
        i = n_lat // nb_lat - 1 - j
        n = nb_lat * c
        rows = pl.ds(pl.multiple_of(i * n, n), n)
        o_ref[rows, :] = finish(acc_ref[rows, :] + block(lat, i, nb_lat, 1, True), g_ref[rows, :])
        return carry

    s_ref[...] = jnp.zeros_like(s_ref)
    lax.fori_loop(0, n_ctx // nb_ctx, fwd_ctx, 0)
    lax.fori_loop(0, n_lat // nb_lat, fwd_lat, 0, unroll=2)
    s_ref[...] = jnp.zeros_like(s_ref)
    lax.fori_loop(0, n_ctx // nb_ctx, bwd_ctx, 0)
    lax.fori_loop(0, n_lat // nb_lat, bwd_lat, 0, unroll=2)


def _gla(proj, lr, wg_pad, bg, norm_w, n_batch, seq, ctx_len, n_heads, dk, dv, col_q):
    n_lat = seq // GLA_CHUNK
    n_ctx = ctx_len // GLA_CHUNK
    assert dk == LANES and dv % dk == 0
    assert n_lat % min(GLA_BLOCK_CHUNKS, n_lat) == 0 and n_ctx % min(GLA_BLOCK_CHUNKS, n_ctx) == 0
    cb = n_batch * seq // ctx_len
    vq = dv // dk
    col_v = (col_q + 2 * n_heads) // vq
    col_g = col_v + n_heads
    cos, sin = _rope_tables(seq, dk)
    nblk = GLA_BLOCK_CHUNKS * GLA_CHUNK
    r = jnp.arange(nblk)
    same_chunk = (r[:, None] // GLA_CHUNK) == (r[None, :] // GLA_CHUNK)
    keep = jnp.stack([same_chunk & (r[:, None] >= r[None, :]), same_chunk & (r[:, None] <= r[None, :])])
    keep = keep.astype(F32)
    vmask = ((r[:, None] // GLA_CHUNK) == (jnp.arange(GLA_BLOCK_CHUNKS * dv)[None, :] // dv)).astype(BF16)

    def spec(rows, width, row_block, col):
        return pl.BlockSpec((rows, width), lambda b, h: (row_block(b), col(h)))
    lat_rb = lambda b: b
    ctx_rb = lambda b: cb + b
    in_specs = []
    for rows, rb in ((seq, lat_rb), (ctx_len, ctx_rb)):
        in_specs += [spec(rows, dk, rb, lambda h: col_q + h),
                     spec(rows, dk, rb, lambda h: col_q + n_heads + h),
                     spec(rows, dv, rb, lambda h: col_v + h),
                     spec(rows, dv, rb, lambda h: col_g + h),
                     spec(rows, LANES, rb, lambda h: 0)]
    in_specs += [pl.BlockSpec((2, LANES, dk), lambda b, h: (0, 0, h)),
                 pl.BlockSpec((2, dk), lambda b, h: (0, h)),
                 pl.BlockSpec((1, dv), lambda b, h: (0, 0)),
                 pl.BlockSpec((seq, dk), lambda b, h: (0, 0)),
                 pl.BlockSpec((seq, dk), lambda b, h: (0, 0)),
                 pl.BlockSpec((2, nblk, nblk), lambda b, h: (0, 0, 0)),
                 pl.BlockSpec((2, nblk, nblk), lambda b, h: (0, 0, 0)),
                 pl.BlockSpec((nblk, GLA_BLOCK_CHUNKS * dv), lambda b, h: (0, 0))]
    return pl.pallas_call(
        functools.partial(_gla_kernel, n_lat=n_lat, n_ctx=n_ctx),
        grid=(n_batch, n_heads), in_specs=in_specs,
        out_specs=[pl.BlockSpec((seq, dv), lambda b, h: (b, h)),
                   pl.BlockSpec((ctx_len, dv), lambda b, h: (b, h))],
        out_shape=[jax.ShapeDtypeStruct((n_batch * seq, n_heads * dv), BF16),
                   jax.ShapeDtypeStruct((n_batch * ctx_len, n_heads * dv), BF16)],
        scratch_shapes=[pltpu.VMEM((dk, dv), F32), pltpu.VMEM((seq, dv), F32),
                        pltpu.VMEM((ctx_len, dv), F32)],
        compiler_params=_params("parallel", "parallel"), name="gla",
    )(proj, proj, proj, proj, lr, proj, proj, proj, proj, lr,
      wg_pad, bg, norm_w.reshape(1, dv), cos, sin, keep.astype(BF16), keep, vmask)


def _lru_pitch(n):
    seg = n // SUBLANES
    assert seg % SUBLANES == 0
    return seg + 4


def _lru_kernel(x_ref, g_ref, xc_ref, gc_ref, cw_ref, cb_ref, w4_ref, b4_ref, lam_ref, o_ref, oc_ref,
                xpad_ref, af_ref, cf_ref, ab_ref, cbk_ref, hf_ref, pf_ref, hb_ref, pb_ref, *, seq, ctx_len):
    bw = x_ref.shape[1]
    pad = SUBLANES
    sp = _softplus(-lam_ref[...])
    cw = cw_ref[...]
    w4 = w4_ref[...]
    b4 = b4_ref[...]

    def put(ref, n, t0, val):
        seg, pitch = n // SUBLANES, _lru_pitch(n)
        for s in range(SUBLANES):
            lo, hi = max(t0, s * seg), min(t0 + val.shape[0], (s + 1) * seg)
            if lo < hi:
                ref[s * pitch + lo - s * seg:s * pitch + hi - s * seg, :] = val[lo - t0:hi - t0, :]

    def coeffs(src_ref, n):
        xpad_ref[0:pad, :] = jnp.zeros((pad, bw), F32)
        xpad_ref[pad + n:pad + n + pad, :] = jnp.zeros((pad, bw), F32)
        tile = min(n, 512)
        for t0 in range(0, n, tile):
            xpad_ref[pad + t0:pad + t0 + tile, :] = src_ref[t0:t0 + tile, :].astype(F32)
        for t0 in range(0, n, tile):
            xc = cb_ref[...]
            for j in range(LRU_CONV):
                lo = pad + t0 + j - LRU_CONV // 2
                xc = xc + cw[j:j + 1, :] * xpad_ref[lo:lo + tile, :]
            z = _dot(xc.astype(BF16), w4) + b4
            for d, (a_ref, c_ref) in enumerate(((af_ref, cf_ref), (ab_ref, cbk_ref))):
                r = jax.nn.sigmoid(z[:, (2 * d) * bw:(2 * d + 1) * bw])
                i = jax.nn.sigmoid(z[:, (2 * d + 1) * bw:(2 * d + 2) * bw])
                log_a = (-LRU_C) * r * sp[d:d + 1, :]
                a = jnp.exp(log_a)
                put(a_ref, n, t0, a)
                put(c_ref, n, t0, jnp.sqrt(1.0 - a * a) * (i * xc))

    def scan(n, h0_f, h0_b):
        seg, pitch = n // SUBLANES, _lru_pitch(n)

        def step(g, carry):
            hf, pf, hb, pb = carry
            fwd = pl.ds(g, SUBLANES, stride=pitch)
            bwd = pl.ds(seg - 1 - g, SUBLANES, stride=pitch)
            a = af_ref[fwd, :]
            hf = a * hf + cf_ref[fwd, :]
            pf = a * pf
            hf_ref[fwd, :] = hf
            pf_ref[fwd, :] = pf
            a = ab_ref[bwd, :]
            hb = a * hb + cbk_ref[bwd, :]
            pb = a * pb
            hb_ref[bwd, :] = hb
            pb_ref[bwd, :] = pb
            return hf, pf, hb, pb

        zero = jnp.zeros((SUBLANES, bw), F32)
        one = jnp.ones((SUBLANES, bw), F32)
        hf, pf, hb, pb = lax.fori_loop(0, seg, step, (zero, one, zero, one), unroll=8)
        carry_f = [h0_f]
        for s in range(SUBLANES):
            carry_f.append(pf[s:s + 1, :] * carry_f[s] + hf[s:s + 1, :])
        carry_b = [h0_b]
        for s in range(SUBLANES - 1, -1, -1):
            carry_b.append(pb[s:s + 1, :] * carry_b[-1] + hb[s:s + 1, :])
        carry_b = carry_b[::-1]
        return carry_f, carry_b

    def emit(n, carry_f, carry_b, gate_ref, out_ref):
        seg, pitch = n // SUBLANES, _lru_pitch(n)
        for s in range(SUBLANES):
            rows = slice(s * seg, (s + 1) * seg)
            held = slice(s * pitch, s * pitch + seg)
            h = (hf_ref[held, :] + pf_ref[held, :] * carry_f[s]
                 + hb_ref[held, :] + pb_ref[held, :] * carry_b[s + 1])
            out_ref[rows, :] = (h * _gelu_tanh(gate_ref[rows, :].astype(F32))).astype(out_ref.dtype)

    zero_h = jnp.zeros((1, bw), F32)
    coeffs(xc_ref, ctx_len)
    cf, cbw = scan(ctx_len, zero_h, zero_h)
    emit(ctx_len, cf, cbw, gc_ref, oc_ref)
    coeffs(x_ref, seq)
    lf, lb = scan(seq, cf[SUBLANES], cbw[0])
    emit(seq, lf, lb, g_ref, o_ref)


def _rglru(proj, conv_w, conv_b, w4, b4, lam, n_batch, seq, ctx_len, col_x):
    n_blk = w4.shape[0]
    bw = LANES
    cb = n_batch * seq // ctx_len
    width = n_blk * bw
    return pl.pallas_call(
        functools.partial(_lru_kernel, seq=seq, ctx_len=ctx_len),
        grid=(n_batch, n_blk),
        in_specs=[pl.BlockSpec((seq, bw), lambda b, j: (b, col_x + j)),
                  pl.BlockSpec((seq, bw), lambda b, j: (b, col_x + n_blk + j)),
                  pl.BlockSpec((ctx_len, bw), lambda b, j: (cb + b, col_x + j)),
                  pl.BlockSpec((ctx_len, bw), lambda b, j: (cb + b, col_x + n_blk + j)),
                  pl.BlockSpec((LRU_CONV, bw), lambda b, j: (0, j)),
                  pl.BlockSpec((1, bw), lambda b, j: (0, j)),
                  pl.BlockSpec((None, bw, 4 * bw), lambda b, j: (j, 0, 0)),
                  pl.BlockSpec((None, 1, 4 * bw), lambda b, j: (j, 0, 0)),
                  pl.BlockSpec((2, bw), lambda b, j: (0, j))],
        out_specs=[pl.BlockSpec((seq, bw), lambda b, j: (b, j)),
                   pl.BlockSpec((ctx_len, bw), lambda b, j: (b, j))],
        out_shape=[jax.ShapeDtypeStruct((n_batch * seq, width), BF16),
                   jax.ShapeDtypeStruct((n_batch * ctx_len, width), BF16)],
        scratch_shapes=([pltpu.VMEM((seq + 2 * SUBLANES, bw), F32)]
                        + [pltpu.VMEM((SUBLANES * _lru_pitch(seq), bw), F32)] * 8),
        compiler_params=_params("parallel", "parallel"), name="rglru",
    )(proj, proj, proj, proj, conv_w, conv_b.reshape(1, width), w4, b4, lam)


def _row_tile(n_rows, cap):
    t = cap
    while n_rows % t:
        t //= 2
    return t


def kernel(x, c, ctx, c_ctx, w_mod, b_mod, norm_mix, norm_ffn, w_in, na_rpb, gla_wg, gla_bg, gla_norm,
           conv_w, conv_b, lru_wa, lru_ba, lru_wx, lru_bx, lru_lam, w_out, ffd_w1, ffd_w3, ffd_w2,
           router, moe_w1, moe_w3, moe_w2, final_norm):
    n_batch, seq, d = x.shape
    ctx_len = ctx.shape[1]
    depth = w_in.shape[0]
    n_lat = n_batch * seq
    n_tot = n_lat + n_batch * ctx_len
    na_heads = na_rpb.shape[1]
    na_w = na_heads * LANES
    gla_kw = gla_wg.shape[3]
    gla_heads = gla_kw // LANES
    gla_dv = gla_norm.shape[1]
    gla_vw = gla_heads * gla_dv
    lru_w = conv_w.shape[2]
    n_blk = lru_wa.shape[2]
    assert n_batch < SUBLANES and lru_w == n_blk * LANES and seq % (SUBLANES * GRID_W) == 0

    p_lr = 3 * na_w + 2 * gla_kw + 2 * gla_vw
    col_gla = 3 * na_w // LANES
    col_lru = p_lr // LANES

    tm = _row_tile(n_tot, 1024)
    tm_lat = _row_tile(n_lat, 1024)
    tr = _row_tile(n_tot, 256)

    streams = [x.reshape(n_lat, d), ctx.reshape(n_batch * ctx_len, d)]
    c8 = jnp.zeros((SUBLANES, d), F32).at[:n_batch].set(c).at[n_batch].set(c_ctx)

    for l in range(depth):
        last = l == depth - 1
        rows_out = n_lat if last else n_tot
        tmo = tm_lat if last else tm
        mod3 = _modulation(c8, w_mod, b_mod, l).reshape(SUBLANES * N_MOD, 1, d)

        h = _norm_mod(streams, norm_mix[l], mod3, 1, 0, n_tot, seq, n_batch, tr)
        w_l = w_in[l]
        w_main = jnp.concatenate([w_l[:, :p_lr], w_l[:, p_lr + 2 * GLA_LOWRANK:]], axis=1).astype(BF16)
        w_lr = jnp.pad(w_l[:, p_lr:p_lr + 2 * GLA_LOWRANK], ((0, 0), (0, LANES - 2 * GLA_LOWRANK))).astype(BF16)
        proj = _matmul(h, w_main, BF16, tm, _row_tile(w_main.shape[1], 512))
        lr = _matmul(h, w_lr, F32, tm, LANES)

        na_l, na_c = _neighbourhood_attention(proj, _na_bias_table(na_rpb[l]), n_batch, seq, ctx_len,
                                              na_heads, 0)
        wg = gla_wg[l]
        wg_pad = jnp.zeros((2, LANES, gla_kw), F32)
        wg_pad = wg_pad.at[0, :GLA_LOWRANK].set(wg[0]).at[1, GLA_LOWRANK:2 * GLA_LOWRANK].set(wg[1])
        gla_l, gla_c = _gla(proj, lr, wg_pad.astype(BF16), gla_bg[l], gla_norm[l], n_batch, seq, ctx_len,
                            gla_heads, LANES, gla_dv, col_gla)
        w4 = jnp.concatenate([lru_wa[l, 0], lru_wx[l, 0], lru_wa[l, 1], lru_wx[l, 1]], axis=-1).astype(BF16)
        b4 = jnp.concatenate([lru_ba[l, 0].reshape(n_blk, 1, LANES), lru_bx[l, 0].reshape(n_blk, 1, LANES),
                              lru_ba[l, 1].reshape(n_blk, 1, LANES), lru_bx[l, 1].reshape(n_blk, 1, LANES)],
                             axis=-1)
        lru_l, lru_c = _rglru(proj, conv_w[l], conv_b[l], w4, b4, lru_lam[l], n_batch, seq, ctx_len, col_lru)

        mixed = [[na_l, gla_l, lru_l]] + ([] if last else [[na_c, gla_c, lru_c]])
        x_all = _out_proj(mixed, w_out[l].astype(BF16), streams, mod3, 2, seq, n_batch, _row_tile(tm, 512),
                          _row_tile(d, 1024))

        j = l // 2
        if l % 2 == 0:
            h = _norm_mod([x_all], norm_ffn[l], mod3, 4, 3, rows_out, seq, n_batch, tr)
            act = _swiglu_in(h, ffd_w1[j].astype(BF16), ffd_w3[j].astype(BF16), tmo,
                             _row_tile(ffd_w1.shape[2], 512))
            x_all = _matmul_residual(act, ffd_w2[j].astype(BF16), x_all, mod3, 5, rows_out, seq, n_batch, tmo,
                                     _row_tile(d, 1024), _row_tile(act.shape[1], 2048))
        else:
            h, route = _norm_mod([x_all], norm_ffn[l], mod3, 4, 3, rows_out, seq, n_batch, tr, router=router[j])
            x_all = _moe(h, route, moe_w1[j].astype(BF16), moe_w3[j].astype(BF16), moe_w2[j].astype(BF16),
                         x_all, mod3, 5, rows_out, seq, n_batch, tn=_row_tile(moe_w1.shape[3], 512),
                         tc=_row_tile(rows_out, 256), out_norm_w=final_norm if last else None)
        streams = [x_all]

    out = x_all if depth % 2 == 0 else _final_norm(x_all, final_norm, n_lat, tr)
    return out.reshape(n_batch, seq, d)
```

```python
import functools

import jax
import jax.numpy as jnp
from jax import lax
from jax.experimental import pallas as pl
from jax.experimental.pallas import tpu as pltpu

F32 = jnp.float32
BF16 = jnp.bfloat16

EPS = 1e-6
ROPE_BASE = 10000.0
GRID_W = 64
NA_WIN_ROWS = 8
NA_WIN_COLS = 16
GLA_CHUNK = 64
GLA_LOWRANK = 16
GLA_NORMALIZER = 16.0
LRU_C = 8.0
LRU_CONV = 4
N_MOD = 6
TOP_K = 2
LANES = 128
SUBLANES = 8
VMEM_LIMIT = 56 * 1024 * 1024
NEG_BIG = -1e30

_NT = (((1,), (1,)), ((), ()))


def _params(*sem):
    return pltpu.CompilerParams(dimension_semantics=sem, vmem_limit_bytes=VMEM_LIMIT)


def _dot(a, b):
    return jnp.dot(a, b, preferred_element_type=F32)


def _split_bf16(x):
    hi = x.astype(BF16)
    lo = (x - hi.astype(F32)).astype(BF16)
    return hi, lo


def _softplus(x):
    return jnp.maximum(x, 0.0) + jnp.log1p(jnp.exp(-jnp.abs(x)))


def _silu(x):
    return x * jax.nn.sigmoid(x)


def _gelu_tanh(x):
    return 0.5 * x * (1.0 + jnp.tanh(0.7978845608028654 * (x + 0.044715 * (x * x * x))))


def _mod_kernel(c_ref, w_ref, b_ref, o_ref):
    c = c_ref[...]
    a_hi, a_lo = _split_bf16(_silu(c))
    w_hi, w_lo = _split_bf16(w_ref[...])
    o_ref[...] = _dot(a_hi, w_hi) + _dot(a_lo, w_hi) + _dot(a_hi, w_lo) + b_ref[...]


def _modulation(c8, w_mod, b_mod, layer, tn=512):
    d = c8.shape[1]
    n = w_mod.shape[2]
    return pl.pallas_call(
        _mod_kernel,
        grid=(n // tn,),
        in_specs=[pl.BlockSpec((SUBLANES, d), lambda j: (0, 0)),
                  pl.BlockSpec((None, d, tn), lambda j: (layer, 0, j)),
                  pl.BlockSpec((None, 1, tn), lambda j: (layer, 0, j))],
        out_specs=pl.BlockSpec((SUBLANES, tn), lambda j: (0, j)),
        out_shape=jax.ShapeDtypeStruct((SUBLANES, n), F32),
        compiler_params=_params("parallel"),
        name="modulation",
    )(c8, w_mod, b_mod.reshape(b_mod.shape[0], 1, n))


def _mod_spec(which, tile_rows, seq, n_batch, d_block, col_axis=None):
    def row(i):
        return jnp.minimum((i * tile_rows) // seq, n_batch) * N_MOD + which
    if col_axis is None:
        return pl.BlockSpec((None, 1, d_block), lambda i, *_: (row(i), 0, 0))
    return pl.BlockSpec((None, 1, d_block), lambda *g: (row(g[0]), 0, g[col_axis]))


def _norm_mod_body(x_ref, nw_ref, sc_ref, sh_ref):
    x = x_ref[...]
    y = x * lax.rsqrt(jnp.mean(x * x, axis=-1, keepdims=True) + EPS) * nw_ref[...]
    return y * (1.0 + sc_ref[...]) + sh_ref[...]


def _row_group_specs(group_rows, tile, cols, col_index):
    specs, bounds = [], [0]
    for rows in group_rows:
        first, n_tiles = bounds[-1], rows // tile
        specs.append(pl.BlockSpec(
            (tile, cols), lambda i, *g, first=first, n_tiles=n_tiles: (jnp.clip(i - first, 0, n_tiles - 1),
                                                                        col_index(*g))))
        bounds.append(first + n_tiles)
    return specs, bounds


def _in_group(i, bounds, g):
    return jnp.logical_and(i >= bounds[g], i < bounds[g + 1])


def _norm_mod_kernel(*refs, bounds):
    n_groups = len(bounds) - 1
    nw_ref, sc_ref, sh_ref, o_ref = refs[n_groups:]
    for g in range(n_groups):
        @pl.when(_in_group(pl.program_id(0), bounds, g))
        def _(g=g):
            o_ref[...] = _norm_mod_body(refs[g], nw_ref, sc_ref, sh_ref).astype(o_ref.dtype)


def _norm_route_kernel(x_ref, nw_ref, sc_ref, sh_ref, rhi_ref, rlo_ref, o_ref, g_ref, *, n_experts):
    h = _norm_mod_body(x_ref, nw_ref, sc_ref, sh_ref)
    o_ref[...] = h.astype(o_ref.dtype)
    h_hi, h_lo = _split_bf16(h)
    logits = _dot(h_hi, rhi_ref[...]) + _dot(h_lo, rhi_ref[...]) + _dot(h_hi, rlo_ref[...])
    lane = lax.broadcasted_iota(jnp.int32, logits.shape, 1)
    valid = lane < n_experts
    logits = jnp.where(valid, logits, NEG_BIG)
    e = jnp.exp(logits - jnp.max(logits, axis=-1, keepdims=True))
    probs = jnp.where(valid, e / jnp.sum(e, axis=-1, keepdims=True), -1.0)
    p1 = jnp.max(probs, axis=-1, keepdims=True)
    i1 = jnp.min(jnp.where(probs == p1, lane, LANES), axis=-1, keepdims=True)
    rest = jnp.where(lane == i1, -1.0, probs)
    p2 = jnp.max(rest, axis=-1, keepdims=True)
    i2 = jnp.min(jnp.where(rest == p2, lane, LANES), axis=-1, keepdims=True)
    den = p1 + p2
    g_ref[...] = (jnp.where(lane == 0, p1 / den, 0.0) + jnp.where(lane == 1, p2 / den, 0.0)
                  + jnp.where(lane == 2, i1.astype(F32), 0.0) + jnp.where(lane == 3, i2.astype(F32), 0.0))


def _norm_mod(streams, norm_w, mod3, which_scale, which_shift, n_rows, seq, n_batch, tr=256,
              router=None):
    d = streams[0].shape[1]
    group_rows = [n_rows] if len(streams) == 1 else [s.shape[0] for s in streams]
    assert sum(group_rows) == n_rows
    x_specs, bounds = _row_group_specs(group_rows, tr, d, lambda: 0)
    in_specs = x_specs + [pl.BlockSpec((1, d), lambda i: (0, 0)),
                          _mod_spec(which_scale, tr, seq, n_batch, d),
                          _mod_spec(which_shift, tr, seq, n_batch, d)]
    h_spec = pl.BlockSpec((tr, d), lambda i: (i, 0))
    h_shape = jax.ShapeDtypeStruct((n_rows, d), BF16)
    if router is None:
        return pl.pallas_call(
            functools.partial(_norm_mod_kernel, bounds=bounds), grid=(n_rows // tr,), in_specs=in_specs,
            out_specs=h_spec, out_shape=h_shape, compiler_params=_params("parallel"), name="norm_mod",
        )(*streams, norm_w.reshape(1, d), mod3, mod3)
    (x_all,) = streams
    n_experts = router.shape[1]
    r_pad = jnp.pad(router, ((0, 0), (0, LANES - n_experts)))
    r_hi = r_pad.astype(BF16)
    r_lo = (r_pad - r_hi.astype(F32)).astype(BF16)
    w_spec = pl.BlockSpec((d, LANES), lambda i: (0, 0))
    return pl.pallas_call(
        functools.partial(_norm_route_kernel, n_experts=n_experts),
        grid=(n_rows // tr,), in_specs=in_specs + [w_spec, w_spec],
        out_specs=[h_spec, pl.BlockSpec((tr, LANES), lambda i: (i, 0))],
        out_shape=[jax.ShapeDtypeStruct((n_rows, d), F32), jax.ShapeDtypeStruct((n_rows, LANES), F32)],
        compiler_params=_params("parallel"), name="norm_route",
    )(x_all, norm_w.reshape(1, d), mod3, mod3, r_hi, r_lo)


def _final_norm_kernel(x_ref, nw_ref, o_ref):
    x = x_ref[...]
    o_ref[...] = x * lax.rsqrt(jnp.mean(x * x, axis=-1, keepdims=True) + EPS) * nw_ref[...]


def _final_norm(x_all, norm_w, n_rows, tr=256):
    d = x_all.shape[1]
    return pl.pallas_call(
        _final_norm_kernel, grid=(n_rows // tr,),
        in_specs=[pl.BlockSpec((tr, d), lambda i: (i, 0)), pl.BlockSpec((1, d), lambda i: (0, 0))],
        out_specs=pl.BlockSpec((tr, d), lambda i: (i, 0)),
        out_shape=jax.ShapeDtypeStruct((n_rows, d), F32),
        compiler_params=_params("parallel"), name="final_norm",
    )(x_all, norm_w.reshape(1, d))


def _in_proj_kernel(a_ref, w_ref, ws_ref, o_ref, os_ref):
    a = a_ref[...]
    o_ref[...] = _dot(a, w_ref[...]).astype(o_ref.dtype)

    @pl.when(pl.program_id(1) == 0)
    def _():
        os_ref[...] = _dot(a, ws_ref[...])


def _in_proj(a, w, w_side, tm, tn):
    m, k = a.shape
    n = w.shape[1]
    return pl.pallas_call(
        _in_proj_kernel, grid=(m // tm, n // tn),
        in_specs=[pl.BlockSpec((tm, k), lambda i, j: (i, 0)),
                  pl.BlockSpec((k, tn), lambda i, j: (0, j)),
                  pl.BlockSpec((k, LANES), lambda i, j: (0, 0))],
        out_specs=[pl.BlockSpec((tm, tn), lambda i, j: (i, j)),
                   pl.BlockSpec((tm, LANES), lambda i, j: (i, 0))],
        out_shape=[jax.ShapeDtypeStruct((m, n), BF16), jax.ShapeDtypeStruct((m, LANES), F32)],
        compiler_params=_params("parallel", "arbitrary"), name="in_proj",
    )(a, w, w_side)


def _mm_res_kernel(a_ref, w_ref, x_ref, g_ref, o_ref, acc_ref, *, nk):
    k = pl.program_id(2)

    @pl.when(k == 0)
    def _():
        acc_ref[...] = jnp.zeros_like(acc_ref)

    acc_ref[...] += _dot(a_ref[...], w_ref[...])

    @pl.when(k == nk - 1)
    def _():
        o_ref[...] = x_ref[...] + g_ref[...] * acc_ref[...]


def _matmul_residual(a, w, x_all, mod3, which_gate, n_rows, seq, n_batch, tm, tn, tk):
    k = a.shape[1]
    d = w.shape[1]
    nk = k // tk
    return pl.pallas_call(
        functools.partial(_mm_res_kernel, nk=nk), grid=(n_rows // tm, d // tn, nk),
        in_specs=[pl.BlockSpec((tm, tk), lambda i, j, kk: (i, kk)),
                  pl.BlockSpec((tk, tn), lambda i, j, kk: (kk, j)),
                  pl.BlockSpec((tm, tn), lambda i, j, kk: (i, j)),
                  _mod_spec(which_gate, tm, seq, n_batch, tn, col_axis=1)],
        out_specs=pl.BlockSpec((tm, tn), lambda i, j, kk: (i, j)),
        out_shape=jax.ShapeDtypeStruct((n_rows, d), F32),
        scratch_shapes=[pltpu.VMEM((tm, tn), F32)],
        compiler_params=_params("parallel", "parallel", "arbitrary"), name="matmul_residual",
    )(a, w, x_all, mod3)


def _out_proj_kernel(*refs, n_groups, widths, bounds, n_x):
    n_src = len(widths)
    w_ref = refs[n_groups * n_src]
    x_refs = refs[n_groups * n_src + 1:n_groups * n_src + 1 + n_x]
    g_ref, o_ref = refs[-2:]
    i = pl.program_id(0)
    for g in range(n_groups):
        @pl.when(_in_group(i, bounds, g))
        def _(g=g):
            acc, k0 = None, 0
            for s in range(n_src):
                part = _dot(refs[g * n_src + s][...], w_ref[k0:k0 + widths[s], :])
                acc = part if acc is None else acc + part
                k0 += widths[s]
            x_ref = x_refs[0] if n_x == 1 else x_refs[g]
            o_ref[...] = x_ref[...] + g_ref[...] * acc


def _out_proj(sources, w, streams, mod3, which_gate, seq, n_batch, tm, tn):
    k, d = w.shape
    group_rows = [grp[0].shape[0] for grp in sources]
    widths = [a.shape[1] for a in sources[0]]
    assert sum(widths) == k
    n_rows = sum(group_rows)
    a_specs, args = [], []
    for g, grp in enumerate(sources):
        for a in grp:
            specs, bounds = _row_group_specs(group_rows, tm, a.shape[1], lambda j: 0)
            a_specs.append(specs[g])
            args.append(a)
    if len(streams) == 1:
        x_specs = [pl.BlockSpec((tm, tn), lambda i, j: (i, j))]
    else:
        assert [s.shape[0] for s in streams] == group_rows
        x_specs, _ = _row_group_specs(group_rows, tm, tn, lambda j: j)
    return pl.pallas_call(
        functools.partial(_out_proj_kernel, n_groups=len(sources), widths=widths, bounds=bounds,
                          n_x=len(streams)),
        grid=(n_rows // tm, d // tn),
        in_specs=a_specs + [pl.BlockSpec((k, tn), lambda i, j: (0, j))] + x_specs
        + [_mod_spec(which_gate, tm, seq, n_batch, tn, col_axis=1)],
        out_specs=pl.BlockSpec((tm, tn), lambda i, j: (i, j)),
        out_shape=jax.ShapeDtypeStruct((n_rows, d), F32),
        compiler_params=_params("parallel", "parallel"), name="out_proj",
    )(*args, w, *streams, mod3)


def _swiglu_kernel(a_ref, w1_ref, w3_ref, o_ref):
    a = a_ref[...]
    o_ref[...] = (_silu(_dot(a, w1_ref[...])) * _dot(a, w3_ref[...])).astype(o_ref.dtype)


def _swiglu_in(a, w1, w3, tm, tn):
    m, k = a.shape
    f = w1.shape[1]
    return pl.pallas_call(
        _swiglu_kernel, grid=(m // tm, f // tn),
        in_specs=[pl.BlockSpec((tm, k), lambda i, j: (i, 0)),
                  pl.BlockSpec((k, tn), lambda i, j: (0, j)),
                  pl.BlockSpec((k, tn), lambda i, j: (0, j))],
        out_specs=pl.BlockSpec((tm, tn), lambda i, j: (i, j)),
        out_shape=jax.ShapeDtypeStruct((m, f), BF16),
        compiler_params=_params("parallel", "parallel"), name="swiglu_in",
    )(a, w1, w3)


MOE_TILE = 512


def _moe_dispatch(route, n_experts, tile):
    n = route.shape[0]
    e = jnp.concatenate([route[:, 2], route[:, 3]]).astype(jnp.int32)
    onehot = (e[:, None] == jnp.arange(n_experts, dtype=jnp.int32)[None, :]).astype(jnp.int32)
    blk = _row_tile(TOP_K * n, 512)
    oh3 = onehot.astype(BF16).reshape(-1, blk, n_experts)
    inside = jnp.einsum('ij,bjk->bik', jnp.tril(jnp.ones((blk, blk), BF16)), oh3, preferred_element_type=F32)
    before = jnp.cumsum(inside[:, -1, :], axis=0) - inside[:, -1, :]
    csum = (inside + before[:, None, :]).astype(jnp.int32).reshape(TOP_K * n, n_experts)
    counts = csum[-1]
    rank = jnp.sum((csum - onehot) * onehot, axis=1)
    padded = (counts + tile - 1) // tile * tile
    ends = jnp.cumsum(padded)
    pos = jnp.sum(onehot * (ends - padded)[None, :], axis=1) + rank
    n_tiles = (TOP_K * n) // tile + n_experts
    tok = jnp.tile(jnp.arange(n, dtype=jnp.int32), TOP_K)
    src = jnp.zeros((n_tiles * tile,), jnp.int32).at[pos].set(tok, unique_indices=True)
    tile_id = jnp.arange(n_tiles, dtype=jnp.int32)
    tile_expert = jnp.sum((tile_id[:, None] * tile >= ends[None, :]).astype(jnp.int32), axis=1)
    n_used = ends[-1] // tile
    tile_expert = jnp.minimum(tile_expert, jnp.max(jnp.where(tile_id < n_used, tile_expert, 0)))
    return src.reshape(n_tiles, 1, tile), pos, tile_expert.astype(jnp.int32), n_used.reshape(1).astype(jnp.int32)


def _moe_in_kernel(te_ref, nu_ref, src_ref, nxt_ref, h_hbm, w1_ref, w3_ref, o_ref, rows_ref, a_ref, sem, *, tile):
    i = pl.program_id(0)
    j = pl.program_id(1)
    used = i < nu_ref[0]
    slot = i % 2

    def gather(idx_ref, s):
        def issue(p, carry):
            pltpu.make_async_copy(h_hbm.at[pl.ds(idx_ref[0, p], 1), :], rows_ref.at[s, pl.ds(p, 1), :],
                                  sem.at[s]).start()
            return carry
        lax.fori_loop(0, tile, issue, 0, unroll=8)

    @pl.when(jnp.logical_and(i == 0, j == 0))
    def _():
        gather(src_ref, 0)

    @pl.when(jnp.logical_and(i + 1 < nu_ref[0], j == 0))
    def _():
        gather(nxt_ref, 1 - slot)

    @pl.when(jnp.logical_and(used, j == 0))
    def _():
        pltpu.make_async_copy(h_hbm.at[pl.ds(0, tile), :], rows_ref.at[slot], sem.at[slot]).wait()
        a_ref[...] = rows_ref[slot].astype(BF16)

    @pl.when(used)
    def _():
        a = a_ref[...]
        o_ref[...] = (_silu(_dot(a, w1_ref[...])) * _dot(a, w3_ref[...])).astype(o_ref.dtype)

    @pl.when(jnp.logical_not(used))
    def _():
        o_ref[...] = jnp.zeros_like(o_ref)


def _moe_out_kernel(te_ref, nu_ref, a_ref, w_ref, o_ref):
    used = pl.program_id(0) < nu_ref[0]

    @pl.when(used)
    def _():
        o_ref[...] = _dot(a_ref[...], w_ref[...])

    @pl.when(jnp.logical_not(used))
    def _():
        o_ref[...] = jnp.zeros_like(o_ref)


def _moe_combine_kernel(pos_ref, y_hbm, x_ref, g_ref, route_ref, *rest, tc, normed):
    o_ref, rows_ref, sem = rest[-3:]

    def issue(t, carry):
        for s in range(TOP_K):
            pltpu.make_async_copy(y_hbm.at[pl.ds(pos_ref[0, s * tc + t], 1), :],
                                  rows_ref.at[s, pl.ds(t, 1), :], sem).start()
        return carry
    lax.fori_loop(0, tc, issue, 0, unroll=8)
    for s in range(TOP_K):
        pltpu.make_async_copy(y_hbm.at[pl.ds(0, tc), :], rows_ref.at[s], sem).wait()
    route = route_ref[...]
    moe = route[:, 0:1] * rows_ref[0] + route[:, 1:2] * rows_ref[1]
    o = x_ref[...] + g_ref[...] * moe
    if normed:
        o = o * lax.rsqrt(jnp.mean(o * o, axis=-1, keepdims=True) + EPS) * rest[0][...]
    o_ref[...] = o


def _moe(h, route, w1, w3, w2, x_all, mod3, which_gate, n_rows, seq, n_batch, tn=512, tc=256, out_norm_w=None):
    n, d = h.shape
    n_experts, _, f = w1.shape
    tile = MOE_TILE
    src, pos, tile_expert, n_used = _moe_dispatch(route, n_experts, tile)
    n_tiles = src.shape[0]
    smem = functools.partial(pl.BlockSpec, memory_space=pltpu.SMEM)
    act = pl.pallas_call(
        functools.partial(_moe_in_kernel, tile=tile),
        grid_spec=pltpu.PrefetchScalarGridSpec(
            num_scalar_prefetch=2, grid=(n_tiles, f // tn),
            in_specs=[smem((None, 1, tile), lambda i, j, te, nu: (i, 0, 0)),
                      smem((None, 1, tile), lambda i, j, te, nu: (jnp.minimum(i + 1, n_tiles - 1), 0, 0)),
                      pl.BlockSpec(memory_space=pl.ANY),
                      pl.BlockSpec((None, d, tn), lambda i, j, te, nu: (te[i], 0, j)),
                      pl.BlockSpec((None, d, tn), lambda i, j, te, nu: (te[i], 0, j))],
            out_specs=pl.BlockSpec((tile, tn), lambda i, j, te, nu: (i, j)),
            scratch_shapes=[pltpu.VMEM((2, tile, d), F32), pltpu.VMEM((tile, d), BF16),
                            pltpu.SemaphoreType.DMA((2,))]),
        out_shape=jax.ShapeDtypeStruct((n_tiles * tile, f), BF16),
        compiler_params=_params("arbitrary", "arbitrary"), name="moe_in",
    )(tile_expert, n_used, src, src, h, w1, w3)
    tn2 = _row_tile(d, 1024)
    y = pl.pallas_call(
        _moe_out_kernel,
        grid_spec=pltpu.PrefetchScalarGridSpec(
            num_scalar_prefetch=2, grid=(n_tiles, d // tn2),
            in_specs=[pl.BlockSpec((tile, f), lambda i, j, te, nu: (i, 0)),
                      pl.BlockSpec((None, f, tn2), lambda i, j, te, nu: (te[i], 0, j))],
            out_specs=pl.BlockSpec((tile, tn2), lambda i, j, te, nu: (i, j))),
        out_shape=jax.ShapeDtypeStruct((n_tiles * tile, d), F32),
        compiler_params=_params("arbitrary", "arbitrary"), name="moe_out",
    )(tile_expert, n_used, act, w2)
    pos2 = pos.reshape(TOP_K, n // tc, tc).transpose(1, 0, 2).reshape(n // tc, 1, TOP_K * tc)
    normed = out_norm_w is not None
    norm_specs = [pl.BlockSpec((1, d), lambda i: (0, 0))] if normed else []
    norm_args = [out_norm_w.reshape(1, d)] if normed else []
    return pl.pallas_call(
        functools.partial(_moe_combine_kernel, tc=tc, normed=normed),
        grid=(n_rows // tc,),
        in_specs=[smem((None, 1, TOP_K * tc), lambda i: (i, 0, 0)),
                  pl.BlockSpec(memory_space=pl.ANY),
                  pl.BlockSpec((tc, d), lambda i: (i, 0)),
                  _mod_spec(which_gate, tc, seq, n_batch, d),
                  pl.BlockSpec((tc, LANES), lambda i: (i, 0))] + norm_specs,
        out_specs=pl.BlockSpec((tc, d), lambda i: (i, 0)),
        out_shape=jax.ShapeDtypeStruct((n_rows, d), F32),
        scratch_shapes=[pltpu.VMEM((TOP_K, tc, d), F32), pltpu.SemaphoreType.DMA(())],
        compiler_params=_params("arbitrary"), name="moe_combine",
    )(pos2, y, x_all, mod3, route, *norm_args)


def _na_bias_table(rpb):
    n_heads = rpb.shape[0]
    n_dr = 2 * NA_WIN_ROWS - 1
    n_dc = 2 * NA_WIN_COLS - 1
    cols = jnp.arange(GRID_W)
    col_start = jnp.clip(cols - NA_WIN_COLS // 2, 0, GRID_W - NA_WIN_COLS)
    in_win = (cols[None, :] >= col_start[:, None]) & (cols[None, :] < col_start[:, None] + NA_WIN_COLS)
    dc = cols[None, :] - cols[:, None] + NA_WIN_COLS - 1
    expand = ((dc[None] == jnp.arange(n_dc)[:, None, None]) & in_win[None]).astype(F32)
    t = jnp.dot(rpb.reshape(-1, n_dc), expand.reshape(n_dc, GRID_W * GRID_W), precision=lax.Precision.HIGHEST)
    t = jnp.where(in_win.reshape(1, 1, -1), t.reshape(n_heads, n_dr, -1), NEG_BIG)
    t = t.reshape(n_heads, n_dr, GRID_W, GRID_W)
    masked = jnp.full((n_heads, GRID_W, GRID_W), NEG_BIG, F32)
    kinds = []
    for kind in range(3):
        q_rows = []
        for i in range(NA_QROWS):
            lo, dr0 = ((0, NA_WIN_ROWS - 1 - i), (i, NA_WIN_ROWS // 2 - 1),
                       (NA_QROWS, NA_WIN_ROWS - 1 - NA_QROWS - i))[kind]
            pieces = [t[:, dr0 + j - lo] if lo <= j < lo + NA_WIN_ROWS else masked for j in range(NA_KROWS)]
            q_rows.append(jnp.concatenate(pieces, axis=-1))
        kinds.append(jnp.concatenate(q_rows, axis=1))
    return jnp.stack(kinds, axis=1)


NA_QROWS = 4
NA_KROWS = NA_WIN_ROWS + NA_QROWS


def _na_kernel(q_ref, k_ref, v_ref, qc_ref, kc_ref, vc_ref, bias_ref, o_ref, oc_ref, *, rows, scale):
    kc = kc_ref[...]
    vc = vc_ref[...]
    nq = NA_QROWS * GRID_W
    nk = NA_KROWS * GRID_W
    n_blocks = rows // NA_QROWS

    def block(i, carry):
        base = jnp.clip(i * NA_QROWS - NA_WIN_ROWS // 2, 0, rows - NA_KROWS)
        kind = jnp.where(i == 0, 0, jnp.where(i == n_blocks - 1, 2, 1))
        q0 = pl.multiple_of(i * nq, nq)
        k0 = pl.multiple_of(base * GRID_W, nq)
        q = q_ref[pl.ds(q0, nq), :]
        kw = k_ref[pl.ds(k0, nk), :]
        vw = v_ref[pl.ds(k0, nk), :]
        s_w = lax.dot_general(q, kw, _NT, preferred_element_type=F32) * scale + bias_ref[kind]
        s_c = lax.dot_general(q, kc, _NT, preferred_element_type=F32) * scale
        m = jnp.maximum(jnp.max(s_w, axis=-1, keepdims=True), jnp.max(s_c, axis=-1, keepdims=True))
        p_w = jnp.exp(s_w - m)
        p_c = jnp.exp(s_c - m)
        den = jnp.sum(p_w, axis=-1, keepdims=True) + jnp.sum(p_c, axis=-1, keepdims=True)
        o = _dot(p_w.astype(BF16), vw) + _dot(p_c.astype(BF16), vc)
        o_ref[pl.ds(q0, nq), :] = (o / den).astype(o_ref.dtype)
        return carry

    lax.fori_loop(0, n_blocks, block, 0, unroll=2)

    s = lax.dot_general(qc_ref[...], kc, _NT, preferred_element_type=F32) * scale
    p = jnp.exp(s - jnp.max(s, axis=-1, keepdims=True))
    o = _dot(p.astype(BF16), vc) / jnp.sum(p, axis=-1, keepdims=True)
    oc_ref[...] = o.astype(oc_ref.dtype)


def _neighbourhood_attention(proj, bias_tab, n_batch, seq, ctx_len, n_heads, col0):
    dh = LANES
    rows = seq // GRID_W
    cb = n_batch * seq // ctx_len
    lat = lambda part: pl.BlockSpec((seq, dh), lambda b, h: (b, col0 + part * n_heads + h))
    ctx = lambda part: pl.BlockSpec((ctx_len, dh), lambda b, h: (cb + b, col0 + part * n_heads + h))
    return pl.pallas_call(
        functools.partial(_na_kernel, rows=rows, scale=dh ** -0.5),
        grid=(n_batch, n_heads),
        in_specs=[lat(0), lat(1), lat(2), ctx(0), ctx(1), ctx(2),
                  pl.BlockSpec((None, 3, NA_QROWS * GRID_W, NA_KROWS * GRID_W), lambda b, h: (h, 0, 0, 0))],
        out_specs=[pl.BlockSpec((seq, dh), lambda b, h: (b, h)),
                   pl.BlockSpec((ctx_len, dh), lambda b, h: (b, h))],
        out_shape=[jax.ShapeDtypeStruct((n_batch * seq, n_heads * dh), BF16),
                   jax.ShapeDtypeStruct((n_batch * ctx_len, n_heads * dh), BF16)],
        compiler_params=_params("parallel", "parallel"), name="neighbourhood_attention",
    )(proj, proj, proj, proj, proj, proj, bias_tab)


def _rope_tables(seq, dk):
    quarter = dk // 4
    inv_freq = ROPE_BASE ** (-jnp.arange(quarter, dtype=F32) / quarter)
    pos = jnp.arange(seq, dtype=jnp.int32)
    row = (pos // GRID_W).astype(F32)[:, None] * inv_freq
    col = (pos % GRID_W).astype(F32)[:, None] * inv_freq
    cos = jnp.concatenate([jnp.cos(row), jnp.cos(row), jnp.cos(col), jnp.cos(col)], axis=-1)
    sin = jnp.concatenate([-jnp.sin(row), jnp.sin(row), -jnp.sin(col), jnp.sin(col)], axis=-1)
    return cos, sin


GLA_BLOCK_CHUNKS = 4


def _gla_kernel(q_ref, k_ref, v_ref, g_ref, lr_ref, qc_ref, kc_ref, vc_ref, gc_ref, lrc_ref,
                wg_ref, bg_ref, nw_ref, cos_ref, sin_ref, tri_ref, keep_ref, vmask_ref, o_ref, oc_ref,
                s_ref, acc_ref, accc_ref, *, n_lat, n_ctx):
    dk = q_ref.shape[1]
    dv = v_ref.shape[1]
    c = GLA_CHUNK
    nb_lat = min(GLA_BLOCK_CHUNKS, n_lat)
    nb_ctx = min(GLA_BLOCK_CHUNKS, n_ctx)

    def swap_quarters(t):
        lane = lax.broadcasted_iota(jnp.int32, t.shape, 1)
        first_quarter = (lane % (dk // 2)) < (dk // 4)
        return jnp.where(first_quarter, pltpu.roll(t, dk - dk // 4, 1), pltpu.roll(t, dk // 4, 1))

    def block(refs, blk, nb, d, rope):
        qr, kr, vr, lrr = refs
        n = nb * c
        rows = pl.ds(pl.multiple_of(blk * n, n), n)
        q = qr[rows, :].astype(F32) * (dk ** -0.5)
        k = kr[rows, :].astype(F32)
        if rope:
            cos = cos_ref[rows, :]
            sin = sin_ref[rows, :]
            q = q * cos + swap_quarters(q) * sin
            k = k * cos + swap_quarters(k) * sin
        v = vr[rows, :]
        z = _dot(lrr[rows, :].astype(BF16), wg_ref[d]) + bg_ref[d:d + 1, :]
        logg = -_softplus(-z) * (1.0 / GLA_NORMALIZER)
        g1 = logg.astype(BF16)
        rem = logg - g1.astype(F32)
        g2 = rem.astype(BF16)
        g3 = (rem - g2.astype(F32)).astype(BF16)
        b3 = _dot(tri_ref[d, :n, :n], jnp.concatenate([g1, g2, g3], axis=1))
        b = b3[:, :dk] + b3[:, dk:2 * dk] + b3[:, 2 * dk:]
        end_row = c - 1 if d == 0 else 0
        ends = [b[m * c + end_row:m * c + end_row + 1, :] for m in range(nb)]
        b_end = jnp.concatenate([jnp.broadcast_to(e, (c, dk)) for e in ends], axis=0)
        q_dec = (q * jnp.exp(b)).astype(BF16)
        k_inv = (k * jnp.exp(-b)).astype(BF16)
        k_end_t = (k * jnp.exp(b_end - b)).T.astype(BF16)
        att = lax.dot_general(q_dec, k_inv, _NT, preferred_element_type=F32)
        att = jnp.where(keep_ref[d, :n, :n] != 0.0, att, 0.0).astype(BF16)
        o_intra = _dot(att, v)
        v_bd = jnp.concatenate([v] * nb, axis=1) * vmask_ref[:n, :nb * dv]
        d_all = _dot(k_end_t, v_bd)
        ends_t = jnp.concatenate(ends + [jnp.zeros((dk - nb, dk), F32)], axis=0).T
        state = s_ref[...]
        outs = [None] * nb
        for m in (range(nb) if d == 0 else range(nb - 1, -1, -1)):
            sl = slice(m * c, (m + 1) * c)
            outs[m] = o_intra[sl, :] + _dot(q_dec[sl, :], state.astype(BF16))
            decay = jnp.broadcast_to(jnp.exp(ends_t[:, m:m + 1]), state.shape)
            state = decay * state + d_all[:, m * dv:(m + 1) * dv]
        s_ref[...] = state
        return jnp.concatenate(outs, axis=0)

    def finish(o, g):
        y = o * lax.rsqrt(jnp.mean(o * o, axis=-1, keepdims=True) + EPS) * nw_ref[...]
        return (y * _silu(g.astype(F32))).astype(o_ref.dtype)

    lat = (q_ref, k_ref, v_ref, lr_ref)
    ctx = (qc_ref, kc_ref, vc_ref, lrc_ref)

    def fwd_ctx(i, carry):
        n = nb_ctx * c
        accc_ref[pl.ds(pl.multiple_of(i * n, n), n), :] = block(ctx, i, nb_ctx, 0, False)
        return carry

    def fwd_lat(i, carry):
        n = nb_lat * c
        acc_ref[pl.ds(pl.multiple_of(i * n, n), n), :] = block(lat, i, nb_lat, 0, True)
        return carry

    def bwd_ctx(j, carry):
        i = n_ctx // nb_ctx - 1 - j
        n = nb_ctx * c
        rows = pl.ds(pl.multiple_of(i * n, n), n)
        oc_ref[rows, :] = finish(accc_ref[rows, :] + block(ctx, i, nb_ctx, 1, False), gc_ref[rows, :])
        return carry

    def bwd_lat(j, carry):
        i = n_lat // nb_lat - 1 - j
        n = nb_lat * c
        rows = pl.ds(pl.multiple_of(i * n, n), n)
        o_ref[rows, :] = finish(acc_ref[rows, :] + block(lat, i, nb_lat, 1, True), g_ref[rows, :])
        return carry

    s_ref[...] = jnp.zeros_like(s_ref)
    lax.fori_loop(0, n_ctx // nb_ctx, fwd_ctx, 0)
    lax.fori_loop(0, n_lat // nb_lat, fwd_lat, 0, unroll=2)
    s_ref[...] = jnp.zeros_like(s_ref)
    lax.fori_loop(0, n_ctx // nb_ctx, bwd_ctx, 0)
    lax.fori_loop(0, n_lat // nb_lat, bwd_lat, 0, unroll=2)


def _gla(proj, lr, wg_pad, bg, norm_w, n_batch, seq, ctx_len, n_heads, dk, dv, col_q):
    n_lat = seq // GLA_CHUNK
    n_ctx = ctx_len // GLA_CHUNK
    assert dk == LANES and dv % dk == 0
    assert n_lat % min(GLA_BLOCK_CHUNKS, n_lat) == 0 and n_ctx % min(GLA_BLOCK_CHUNKS, n_ctx) == 0
    cb = n_batch * seq // ctx_len
    vq = dv // dk
    col_v = (col_q + 2 * n_heads) // vq
    col_g = col_v + n_heads
    cos, sin = _rope_tables(seq, dk)
    nblk = GLA_BLOCK_CHUNKS * GLA_CHUNK
    r = jnp.arange(nblk)
    same_chunk = (r[:, None] // GLA_CHUNK) == (r[None, :] // GLA_CHUNK)
    keep = jnp.stack([same_chunk & (r[:, None] >= r[None, :]), same_chunk & (r[:, None] <= r[None, :])])
    keep = keep.astype(F32)
    vmask = ((r[:, None] // GLA_CHUNK) == (jnp.arange(GLA_BLOCK_CHUNKS * dv)[None, :] // dv)).astype(BF16)

    def spec(rows, width, row_block, col):
        return pl.BlockSpec((rows, width), lambda b, h: (row_block(b), col(h)))
    lat_rb = lambda b: b
    ctx_rb = lambda b: cb + b
    in_specs = []
    for rows, rb in ((seq, lat_rb), (ctx_len, ctx_rb)):
        in_specs += [spec(rows, dk, rb, lambda h: col_q + h),
                     spec(rows, dk, rb, lambda h: col_q + n_heads + h),
                     spec(rows, dv, rb, lambda h: col_v + h),
                     spec(rows, dv, rb, lambda h: col_g + h),
                     spec(rows, LANES, rb, lambda h: 0)]
    in_specs += [pl.BlockSpec((2, LANES, dk), lambda b, h: (0, 0, h)),
                 pl.BlockSpec((2, dk), lambda b, h: (0, h)),
                 pl.BlockSpec((1, dv), lambda b, h: (0, 0)),
                 pl.BlockSpec((seq, dk), lambda b, h: (0, 0)),
                 pl.BlockSpec((seq, dk), lambda b, h: (0, 0)),
                 pl.BlockSpec((2, nblk, nblk), lambda b, h: (0, 0, 0)),
                 pl.BlockSpec((2, nblk, nblk), lambda b, h: (0, 0, 0)),
                 pl.BlockSpec((nblk, GLA_BLOCK_CHUNKS * dv), lambda b, h: (0, 0))]
    return pl.pallas_call(
        functools.partial(_gla_kernel, n_lat=n_lat, n_ctx=n_ctx),
        grid=(n_batch, n_heads), in_specs=in_specs,
        out_specs=[pl.BlockSpec((seq, dv), lambda b, h: (b, h)),
                   pl.BlockSpec((ctx_len, dv), lambda b, h: (b, h))],
        out_shape=[jax.ShapeDtypeStruct((n_batch * seq, n_heads * dv), BF16),
                   jax.ShapeDtypeStruct((n_batch * ctx_len, n_heads * dv), BF16)],
        scratch_shapes=[pltpu.VMEM((dk, dv), F32), pltpu.VMEM((seq, dv), F32),
                        pltpu.VMEM((ctx_len, dv), F32)],
        compiler_params=_params("parallel", "parallel"), name="gla",
    )(proj, proj, proj, proj, lr, proj, proj, proj, proj, lr,
      wg_pad, bg, norm_w.reshape(1, dv), cos, sin, keep.astype(BF16), keep, vmask)


def _lru_pitch(n):
    seg = n // SUBLANES
    assert seg % SUBLANES == 0
    return seg + 4


def _lru_kernel(x_ref, g_ref, xc_ref, gc_ref, cw_ref, cb_ref, w4_ref, b4_ref, lam_ref, o_ref, oc_ref,
                xpad_ref, af_ref, cf_ref, ab_ref, cbk_ref, hf_ref, pf_ref, hb_ref, pb_ref, *, seq, ctx_len):
    bw = x_ref.shape[1]
    pad = SUBLANES
    sp = _softplus(-lam_ref[...])
    cw = cw_ref[...]
    w4 = w4_ref[...]
    b4 = b4_ref[...]

    def put(ref, n, t0, val):
        seg, pitch = n // SUBLANES, _lru_pitch(n)
        for s in range(SUBLANES):
            lo, hi = max(t0, s * seg), min(t0 + val.shape[0], (s + 1) * seg)
            if lo < hi:
                ref[s * pitch + lo - s * seg:s * pitch + hi - s * seg, :] = val[lo - t0:hi - t0, :]

    def coeffs(src_ref, n):
        xpad_ref[0:pad, :] = jnp.zeros((pad, bw), F32)
        xpad_ref[pad + n:pad + n + pad, :] = jnp.zeros((pad, bw), F32)
        tile = min(n, 512)
        for t0 in range(0, n, tile):
            xpad_ref[pad + t0:pad + t0 + tile, :] = src_ref[t0:t0 + tile, :].astype(F32)
        for t0 in range(0, n, tile):
            xc = cb_ref[...]
            for j in range(LRU_CONV):
                lo = pad + t0 + j - LRU_CONV // 2
                xc = xc + cw[j:j + 1, :] * xpad_ref[lo:lo + tile, :]
            z = _dot(xc.astype(BF16), w4) + b4
            for d, (a_ref, c_ref) in enumerate(((af_ref, cf_ref), (ab_ref, cbk_ref))):
                r = jax.nn.sigmoid(z[:, (2 * d) * bw:(2 * d + 1) * bw])
                i = jax.nn.sigmoid(z[:, (2 * d + 1) * bw:(2 * d + 2) * bw])
                log_a = (-LRU_C) * r * sp[d:d + 1, :]
                a = jnp.exp(log_a)
                put(a_ref, n, t0, a)
                put(c_ref, n, t0, jnp.sqrt(1.0 - a * a) * (i * xc))

    def scan(n, h0_f, h0_b):
        seg, pitch = n // SUBLANES, _lru_pitch(n)

        def step(g, carry):
            hf, pf, hb, pb = carry
            fwd = pl.ds(g, SUBLANES, stride=pitch)
            bwd = pl.ds(seg - 1 - g, SUBLANES, stride=pitch)
            a = af_ref[fwd, :]
            hf = a * hf + cf_ref[fwd, :]
            pf = a * pf
            hf_ref[fwd, :] = hf
            pf_ref[fwd, :] = pf
            a = ab_ref[bwd, :]
            hb = a * hb + cbk_ref[bwd, :]
            pb = a * pb
            hb_ref[bwd, :] = hb
            pb_ref[bwd, :] = pb
            return hf, pf, hb, pb

        zero = jnp.zeros((SUBLANES, bw), F32)
        one = jnp.ones((SUBLANES, bw), F32)
        hf, pf, hb, pb = lax.fori_loop(0, seg, step, (zero, one, zero, one), unroll=8)
        carry_f = [h0_f]
        for s in range(SUBLANES):
            carry_f.append(pf[s:s + 1, :] * carry_f[s] + hf[s:s + 1, :])
        carry_b = [h0_b]
        for s in range(SUBLANES - 1, -1, -1):
            carry_b.append(pb[s:s + 1, :] * carry_b[-1] + hb[s:s + 1, :])
        carry_b = carry_b[::-1]
        return carry_f, carry_b

    def emit(n, carry_f, carry_b, gate_ref, out_ref):
        seg, pitch = n // SUBLANES, _lru_pitch(n)
        for s in range(SUBLANES):
            rows = slice(s * seg, (s + 1) * seg)
            held = slice(s * pitch, s * pitch + seg)
            h = (hf_ref[held, :] + pf_ref[held, :] * carry_f[s]
                 + hb_ref[held, :] + pb_ref[held, :] * carry_b[s + 1])
            out_ref[rows, :] = (h * _gelu_tanh(gate_ref[rows, :].astype(F32))).astype(out_ref.dtype)

    zero_h = jnp.zeros((1, bw), F32)
    coeffs(xc_ref, ctx_len)
    cf, cbw = scan(ctx_len, zero_h, zero_h)
    emit(ctx_len, cf, cbw, gc_ref, oc_ref)
    coeffs(x_ref, seq)
    lf, lb = scan(seq, cf[SUBLANES], cbw[0])
    emit(seq, lf, lb, g_ref, o_ref)


def _rglru(proj, conv_w, conv_b, w4, b4, lam, n_batch, seq, ctx_len, col_x):
    n_blk = w4.shape[0]
    bw = LANES
    cb = n_batch * seq // ctx_len
    width = n_blk * bw
    return pl.pallas_call(
        functools.partial(_lru_kernel, seq=seq, ctx_len=ctx_len),
        grid=(n_batch, n_blk),
        in_specs=[pl.BlockSpec((seq, bw), lambda b, j: (b, col_x + j)),
                  pl.BlockSpec((seq, bw), lambda b, j: (b, col_x + n_blk + j)),
                  pl.BlockSpec((ctx_len, bw), lambda b, j: (cb + b, col_x + j)),
                  pl.BlockSpec((ctx_len, bw), lambda b, j: (cb + b, col_x + n_blk + j)),
                  pl.BlockSpec((LRU_CONV, bw), lambda b, j: (0, j)),
                  pl.BlockSpec((1, bw), lambda b, j: (0, j)),
                  pl.BlockSpec((None, bw, 4 * bw), lambda b, j: (j, 0, 0)),
                  pl.BlockSpec((None, 1, 4 * bw), lambda b, j: (j, 0, 0)),
                  pl.BlockSpec((2, bw), lambda b, j: (0, j))],
        out_specs=[pl.BlockSpec((seq, bw), lambda b, j: (b, j)),
                   pl.BlockSpec((ctx_len, bw), lambda b, j: (b, j))],
        out_shape=[jax.ShapeDtypeStruct((n_batch * seq, width), BF16),
                   jax.ShapeDtypeStruct((n_batch * ctx_len, width), BF16)],
        scratch_shapes=([pltpu.VMEM((seq + 2 * SUBLANES, bw), F32)]
                        + [pltpu.VMEM((SUBLANES * _lru_pitch(seq), bw), F32)] * 8),
        compiler_params=_params("parallel", "parallel"), name="rglru",
    )(proj, proj, proj, proj, conv_w, conv_b.reshape(1, width), w4, b4, lam)


def _row_tile(n_rows, cap):
    t = cap
    while n_rows % t:
        t //= 2
    return t


def kernel(x, c, ctx, c_ctx, w_mod, b_mod, norm_mix, norm_ffn, w_in, na_rpb, gla_wg, gla_bg, gla_norm,
           conv_w, conv_b, lru_wa, lru_ba, lru_wx, lru_bx, lru_lam, w_out, ffd_w1, ffd_w3, ffd_w2,
           router, moe_w1, moe_w3, moe_w2, final_norm):
    n_batch, seq, d = x.shape
    ctx_len = ctx.shape[1]
    depth = w_in.shape[0]
    n_lat = n_batch * seq
    n_tot = n_lat + n_batch * ctx_len
    na_heads = na_rpb.shape[1]
    na_w = na_heads * LANES
    gla_kw = gla_wg.shape[3]
    gla_heads = gla_kw // LANES
    gla_dv = gla_norm.shape[1]
    gla_vw = gla_heads * gla_dv
    lru_w = conv_w.shape[2]
    n_blk = lru_wa.shape[2]
    assert n_batch < SUBLANES and lru_w == n_blk * LANES and seq % (SUBLANES * GRID_W) == 0

    p_lr = 3 * na_w + 2 * gla_kw + 2 * gla_vw
    col_gla = 3 * na_w // LANES
    col_lru = p_lr // LANES

    tm = _row_tile(n_tot, 1024)
    tm_lat = _row_tile(n_lat, 1024)
    tr = _row_tile(n_tot, 256)

    streams = [x.reshape(n_lat, d), ctx.reshape(n_batch * ctx_len, d)]
    c8 = jnp.zeros((SUBLANES, d), F32).at[:n_batch].set(c).at[n_batch].set(c_ctx)

    for l in range(depth):
        last = l == depth - 1
        rows_out = n_lat if last else n_tot
        tmo = tm_lat if last else tm
        mod3 = _modulation(c8, w_mod, b_mod, l).reshape(SUBLANES * N_MOD, 1, d)

        h = _norm_mod(streams, norm_mix[l], mod3, 1, 0, n_tot, seq, n_batch, tr)
        w_l = w_in[l]
        w_main = jnp.concatenate([w_l[:, :p_lr], w_l[:, p_lr + 2 * GLA_LOWRANK:]], axis=1).astype(BF16)
        w_lr = jnp.pad(w_l[:, p_lr:p_lr + 2 * GLA_LOWRANK], ((0, 0), (0, LANES - 2 * GLA_LOWRANK))).astype(BF16)
        proj, lr = _in_proj(h, w_main, w_lr, tm, _row_tile(w_main.shape[1], 512))

        na_l, na_c = _neighbourhood_attention(proj, _na_bias_table(na_rpb[l]), n_batch, seq, ctx_len,
                                              na_heads, 0)
        wg = gla_wg[l]
        wg_pad = jnp.zeros((2, LANES, gla_kw), F32)
        wg_pad = wg_pad.at[0, :GLA_LOWRANK].set(wg[0]).at[1, GLA_LOWRANK:2 * GLA_LOWRANK].set(wg[1])
        gla_l, gla_c = _gla(proj, lr, wg_pad.astype(BF16), gla_bg[l], gla_norm[l], n_batch, seq, ctx_len,
                            gla_heads, LANES, gla_dv, col_gla)
        w4 = jnp.concatenate([lru_wa[l, 0], lru_wx[l, 0], lru_wa[l, 1], lru_wx[l, 1]], axis=-1).astype(BF16)
        b4 = jnp.concatenate([lru_ba[l, 0].reshape(n_blk, 1, LANES), lru_bx[l, 0].reshape(n_blk, 1, LANES),
                              lru_ba[l, 1].reshape(n_blk, 1, LANES), lru_bx[l, 1].reshape(n_blk, 1, LANES)],
                             axis=-1)
        lru_l, lru_c = _rglru(proj, conv_w[l], conv_b[l], w4, b4, lru_lam[l], n_batch, seq, ctx_len, col_lru)

        mixed = [[na_l, gla_l, lru_l]] + ([] if last else [[na_c, gla_c, lru_c]])
        x_all = _out_proj(mixed, w_out[l].astype(BF16), streams, mod3, 2, seq, n_batch, _row_tile(tm, 512),
                          _row_tile(d, 1024))

        j = l // 2
        if l % 2 == 0:
            h = _norm_mod([x_all], norm_ffn[l], mod3, 4, 3, rows_out, seq, n_batch, tr)
            act = _swiglu_in(h, ffd_w1[j].astype(BF16), ffd_w3[j].astype(BF16), tmo,
                             _row_tile(ffd_w1.shape[2], 512))
            x_all = _matmul_residual(act, ffd_w2[j].astype(BF16), x_all, mod3, 5, rows_out, seq, n_batch,
                                     _row_tile(tmo, 512), _row_tile(d, 512), act.shape[1])
        else:
            h, route = _norm_mod([x_all], norm_ffn[l], mod3, 4, 3, rows_out, seq, n_batch, tr, router=router[j])
            x_all = _moe(h, route, moe_w1[j].astype(BF16), moe_w3[j].astype(BF16), moe_w2[j].astype(BF16),
                         x_all, mod3, 5, rows_out, seq, n_batch, tn=_row_tile(moe_w1.shape[3], 512),
                         tc=_row_tile(rows_out, 256), out_norm_w=final_norm if last else None)
        streams = [x_all]

    out = x_all if depth % 2 == 0 else _final_norm(x_all, final_norm, n_lat, tr)
    return out.reshape(n_batch, seq, d)
```

```python
import functools

import jax
import jax.numpy as jnp
from jax import lax
from jax.experimental import pallas as pl
from jax.experimental.pallas import tpu as pltpu

F32 = jnp.float32
BF16 = jnp.bfloat16

EPS = 1e-6
ROPE_BASE = 10000.0
GRID_W = 64
NA_WIN_ROWS = 8
NA_WIN_COLS = 16
GLA_CHUNK = 64
GLA_LOWRANK = 16
GLA_NORMALIZER = 16.0
LRU_C = 8.0
LRU_CONV = 4
N_MOD = 6
TOP_K = 2
LANES = 128
SUBLANES = 8
VMEM_LIMIT = 56 * 1024 * 1024
NEG_BIG = -1e30

_NT = (((1,), (1,)), ((), ()))


def _params(*sem):
    return pltpu.CompilerParams(dimension_semantics=sem, vmem_limit_bytes=VMEM_LIMIT)


def _dot(a, b):
    return jnp.dot(a, b, preferred_element_type=F32)


def _split_bf16(x):
    hi = x.astype(BF16)
    lo = (x - hi.astype(F32)).astype(BF16)
    return hi, lo


def _softplus(x):
    return jnp.maximum(x, 0.0) + jnp.log1p(jnp.exp(-jnp.abs(x)))


def _silu(x):
    return x * jax.nn.sigmoid(x)


def _gelu_tanh(x):
    return 0.5 * x * (1.0 + jnp.tanh(0.7978845608028654 * (x + 0.044715 * (x * x * x))))


def _mod_kernel(c_ref, w_ref, b_ref, o_ref):
    c = c_ref[...]
    a_hi, a_lo = _split_bf16(_silu(c))
    w_hi, w_lo = _split_bf16(w_ref[...])
    o_ref[...] = _dot(a_hi, w_hi) + _dot(a_lo, w_hi) + _dot(a_hi, w_lo) + b_ref[...]


def _modulation(c8, w_mod, b_mod, layer, tn=512):
    d = c8.shape[1]
    n = w_mod.shape[2]
    return pl.pallas_call(
        _mod_kernel,
        grid=(n // tn,),
        in_specs=[pl.BlockSpec((SUBLANES, d), lambda j: (0, 0)),
                  pl.BlockSpec((None, d, tn), lambda j: (layer, 0, j)),
                  pl.BlockSpec((None, 1, tn), lambda j: (layer, 0, j))],
        out_specs=pl.BlockSpec((SUBLANES, tn), lambda j: (0, j)),
        out_shape=jax.ShapeDtypeStruct((SUBLANES, n), F32),
        compiler_params=_params("parallel"),
        name="modulation",
    )(c8, w_mod, b_mod.reshape(b_mod.shape[0], 1, n))


def _mod_spec(which, tile_rows, seq, n_batch, d_block, col_axis=None):
    def row(i):
        return jnp.minimum((i * tile_rows) // seq, n_batch) * N_MOD + which
    if col_axis is None:
        return pl.BlockSpec((None, 1, d_block), lambda i, *_: (row(i), 0, 0))
    return pl.BlockSpec((None, 1, d_block), lambda *g: (row(g[0]), 0, g[col_axis]))


def _norm_mod_body(x_ref, nw_ref, sc_ref, sh_ref):
    x = x_ref[...]
    y = x * lax.rsqrt(jnp.mean(x * x, axis=-1, keepdims=True) + EPS) * nw_ref[...]
    return y * (1.0 + sc_ref[...]) + sh_ref[...]


def _row_group_specs(group_rows, tile, cols, col_index):
    specs, bounds = [], [0]
    for rows in group_rows:
        first, n_tiles = bounds[-1], rows // tile
        specs.append(pl.BlockSpec(
            (tile, cols), lambda i, *g, first=first, n_tiles=n_tiles: (jnp.clip(i - first, 0, n_tiles - 1),
                                                                        col_index(*g))))
        bounds.append(first + n_tiles)
    return specs, bounds


def _in_group(i, bounds, g):
    return jnp.logical_and(i >= bounds[g], i < bounds[g + 1])


def _norm_mod_kernel(*refs, bounds):
    n_groups = len(bounds) - 1
    nw_ref, sc_ref, sh_ref, o_ref = refs[n_groups:]
    for g in range(n_groups):
        @pl.when(_in_group(pl.program_id(0), bounds, g))
        def _(g=g):
            o_ref[...] = _norm_mod_body(refs[g], nw_ref, sc_ref, sh_ref).astype(o_ref.dtype)


def _norm_route_kernel(x_ref, nw_ref, sc_ref, sh_ref, rhi_ref, rlo_ref, o_ref, g_ref, *, n_experts):
    h = _norm_mod_body(x_ref, nw_ref, sc_ref, sh_ref)
    o_ref[...] = h.astype(o_ref.dtype)
    h_hi, h_lo = _split_bf16(h)
    logits = _dot(h_hi, rhi_ref[...]) + _dot(h_lo, rhi_ref[...]) + _dot(h_hi, rlo_ref[...])
    lane = lax.broadcasted_iota(jnp.int32, logits.shape, 1)
    valid = lane < n_experts
    logits = jnp.where(valid, logits, NEG_BIG)
    e = jnp.exp(logits - jnp.max(logits, axis=-1, keepdims=True))
    probs = jnp.where(valid, e / jnp.sum(e, axis=-1, keepdims=True), -1.0)
    p1 = jnp.max(probs, axis=-1, keepdims=True)
    i1 = jnp.min(jnp.where(probs == p1, lane, LANES), axis=-1, keepdims=True)
    rest = jnp.where(lane == i1, -1.0, probs)
    p2 = jnp.max(rest, axis=-1, keepdims=True)
    i2 = jnp.min(jnp.where(rest == p2, lane, LANES), axis=-1, keepdims=True)
    den = p1 + p2
    g_ref[...] = (jnp.where(lane == 0, p1 / den, 0.0) + jnp.where(lane == 1, p2 / den, 0.0)
                  + jnp.where(lane == 2, i1.astype(F32), 0.0) + jnp.where(lane == 3, i2.astype(F32), 0.0))


def _norm_mod(streams, norm_w, mod3, which_scale, which_shift, n_rows, seq, n_batch, tr=256,
              router=None):
    d = streams[0].shape[1]
    group_rows = [n_rows] if len(streams) == 1 else [s.shape[0] for s in streams]
    assert sum(group_rows) == n_rows
    x_specs, bounds = _row_group_specs(group_rows, tr, d, lambda: 0)
    in_specs = x_specs + [pl.BlockSpec((1, d), lambda i: (0, 0)),
                          _mod_spec(which_scale, tr, seq, n_batch, d),
                          _mod_spec(which_shift, tr, seq, n_batch, d)]
    h_spec = pl.BlockSpec((tr, d), lambda i: (i, 0))
    h_shape = jax.ShapeDtypeStruct((n_rows, d), BF16)
    if router is None:
        return pl.pallas_call(
            functools.partial(_norm_mod_kernel, bounds=bounds), grid=(n_rows // tr,), in_specs=in_specs,
            out_specs=h_spec, out_shape=h_shape, compiler_params=_params("parallel"), name="norm_mod",
        )(*streams, norm_w.reshape(1, d), mod3, mod3)
    (x_all,) = streams
    n_experts = router.shape[1]
    r_pad = jnp.pad(router, ((0, 0), (0, LANES - n_experts)))
    r_hi = r_pad.astype(BF16)
    r_lo = (r_pad - r_hi.astype(F32)).astype(BF16)
    w_spec = pl.BlockSpec((d, LANES), lambda i: (0, 0))
    return pl.pallas_call(
        functools.partial(_norm_route_kernel, n_experts=n_experts),
        grid=(n_rows // tr,), in_specs=in_specs + [w_spec, w_spec],
        out_specs=[h_spec, pl.BlockSpec((tr, LANES), lambda i: (i, 0))],
        out_shape=[jax.ShapeDtypeStruct((n_rows, d), F32), jax.ShapeDtypeStruct((n_rows, LANES), F32)],
        compiler_params=_params("parallel"), name="norm_route",
    )(x_all, norm_w.reshape(1, d), mod3, mod3, r_hi, r_lo)


def _final_norm_kernel(x_ref, nw_ref, o_ref):
    x = x_ref[...]
    o_ref[...] = x * lax.rsqrt(jnp.mean(x * x, axis=-1, keepdims=True) + EPS) * nw_ref[...]


def _final_norm(x_all, norm_w, n_rows, tr=256):
    d = x_all.shape[1]
    return pl.pallas_call(
        _final_norm_kernel, grid=(n_rows // tr,),
        in_specs=[pl.BlockSpec((tr, d), lambda i: (i, 0)), pl.BlockSpec((1, d), lambda i: (0, 0))],
        out_specs=pl.BlockSpec((tr, d), lambda i: (i, 0)),
        out_shape=jax.ShapeDtypeStruct((n_rows, d), F32),
        compiler_params=_params("parallel"), name="final_norm",
    )(x_all, norm_w.reshape(1, d))


def _mm_kernel(a_ref, w_ref, o_ref):
    o_ref[...] = _dot(a_ref[...], w_ref[...]).astype(o_ref.dtype)


def _matmul(a, w, out_dtype, tm, tn):
    m, k = a.shape
    n = w.shape[1]
    return pl.pallas_call(
        _mm_kernel, grid=(m // tm, n // tn),
        in_specs=[pl.BlockSpec((tm, k), lambda i, j: (i, 0)),
                  pl.BlockSpec((k, tn), lambda i, j: (0, j))],
        out_specs=pl.BlockSpec((tm, tn), lambda i, j: (i, j)),
        out_shape=jax.ShapeDtypeStruct((m, n), out_dtype),
        compiler_params=_params("parallel", "parallel"), name="matmul",
    )(a, w)


def _mm_res_kernel(a_ref, w_ref, x_ref, g_ref, o_ref, acc_ref, *, nk):
    k = pl.program_id(2)

    @pl.when(k == 0)
    def _():
        acc_ref[...] = jnp.zeros_like(acc_ref)

    acc_ref[...] += _dot(a_ref[...], w_ref[...])

    @pl.when(k == nk - 1)
    def _():
        o_ref[...] = x_ref[...] + g_ref[...] * acc_ref[...]


def _matmul_residual(a, w, x_all, mod3, which_gate, n_rows, seq, n_batch, tm, tn, tk):
    k = a.shape[1]
    d = w.shape[1]
    nk = k // tk
    return pl.pallas_call(
        functools.partial(_mm_res_kernel, nk=nk), grid=(n_rows // tm, d // tn, nk),
        in_specs=[pl.BlockSpec((tm, tk), lambda i, j, kk: (i, kk)),
                  pl.BlockSpec((tk, tn), lambda i, j, kk: (kk, j)),
                  pl.BlockSpec((tm, tn), lambda i, j, kk: (i, j)),
                  _mod_spec(which_gate, tm, seq, n_batch, tn, col_axis=1)],
        out_specs=pl.BlockSpec((tm, tn), lambda i, j, kk: (i, j)),
        out_shape=jax.ShapeDtypeStruct((n_rows, d), F32),
        scratch_shapes=[pltpu.VMEM((tm, tn), F32)],
        compiler_params=_params("parallel", "parallel", "arbitrary"), name="matmul_residual",
    )(a, w, x_all, mod3)


def _out_proj_kernel(*refs, n_groups, widths, bounds, n_x):
    n_src = len(widths)
    w_ref = refs[n_groups * n_src]
    x_refs = refs[n_groups * n_src + 1:n_groups * n_src + 1 + n_x]
    g_ref, o_ref = refs[-2:]
    i = pl.program_id(0)
    for g in range(n_groups):
        @pl.when(_in_group(i, bounds, g))
        def _(g=g):
            acc, k0 = None, 0
            for s in range(n_src):
                part = _dot(refs[g * n_src + s][...], w_ref[k0:k0 + widths[s], :])
                acc = part if acc is None else acc + part
                k0 += widths[s]
            x_ref = x_refs[0] if n_x == 1 else x_refs[g]
            o_ref[...] = x_ref[...] + g_ref[...] * acc


def _out_proj(sources, w, streams, mod3, which_gate, seq, n_batch, tm, tn):
    k, d = w.shape
    group_rows = [grp[0].shape[0] for grp in sources]
    widths = [a.shape[1] for a in sources[0]]
    assert sum(widths) == k
    n_rows = sum(group_rows)
    a_specs, args = [], []
    for g, grp in enumerate(sources):
        for a in grp:
            specs, bounds = _row_group_specs(group_rows, tm, a.shape[1], lambda j: 0)
            a_specs.append(specs[g])
            args.append(a)
    if len(streams) == 1:
        x_specs = [pl.BlockSpec((tm, tn), lambda i, j: (i, j))]
    else:
        assert [s.shape[0] for s in streams] == group_rows
        x_specs, _ = _row_group_specs(group_rows, tm, tn, lambda j: j)
    return pl.pallas_call(
        functools.partial(_out_proj_kernel, n_groups=len(sources), widths=widths, bounds=bounds,
                          n_x=len(streams)),
        grid=(n_rows // tm, d // tn),
        in_specs=a_specs + [pl.BlockSpec((k, tn), lambda i, j: (0, j))] + x_specs
        + [_mod_spec(which_gate, tm, seq, n_batch, tn, col_axis=1)],
        out_specs=pl.BlockSpec((tm, tn), lambda i, j: (i, j)),
        out_shape=jax.ShapeDtypeStruct((n_rows, d), F32),
        compiler_params=_params("parallel", "parallel"), name="out_proj",
    )(*args, w, *streams, mod3)


def _swiglu_kernel(a_ref, w1_ref, w3_ref, o_ref):
    a = a_ref[...]
    o_ref[...] = (_silu(_dot(a, w1_ref[...])) * _dot(a, w3_ref[...])).astype(o_ref.dtype)


def _swiglu_in(a, w1, w3, tm, tn):
    m, k = a.shape
    f = w1.shape[1]
    return pl.pallas_call(
        _swiglu_kernel, grid=(m // tm, f // tn),
        in_specs=[pl.BlockSpec((tm, k), lambda i, j: (i, 0)),
                  pl.BlockSpec((k, tn), lambda i, j: (0, j)),
                  pl.BlockSpec((k, tn), lambda i, j: (0, j))],
        out_specs=pl.BlockSpec((tm, tn), lambda i, j: (i, j)),
        out_shape=jax.ShapeDtypeStruct((m, f), BF16),
        compiler_params=_params("parallel", "parallel"), name="swiglu_in",
    )(a, w1, w3)


MOE_TILE = 512


def _moe_dispatch(route, n_experts, tile):
    n = route.shape[0]
    e = jnp.concatenate([route[:, 2], route[:, 3]]).astype(jnp.int32)
    onehot = (e[:, None] == jnp.arange(n_experts, dtype=jnp.int32)[None, :]).astype(jnp.int32)
    csum = jnp.cumsum(onehot, axis=0)
    counts = csum[-1]
    rank = jnp.sum((csum - onehot) * onehot, axis=1)
    padded = (counts + tile - 1) // tile * tile
    ends = jnp.cumsum(padded)
    pos = jnp.sum(onehot * (ends - padded)[None, :], axis=1) + rank
    n_tiles = (TOP_K * n) // tile + n_experts
    tok = jnp.tile(jnp.arange(n, dtype=jnp.int32), TOP_K)
    src = jnp.zeros((n_tiles * tile,), jnp.int32).at[pos].set(tok, unique_indices=True)
    tile_id = jnp.arange(n_tiles, dtype=jnp.int32)
    tile_expert = jnp.sum((tile_id[:, None] * tile >= ends[None, :]).astype(jnp.int32), axis=1)
    n_used = ends[-1] // tile
    tile_expert = jnp.minimum(tile_expert, jnp.max(jnp.where(tile_id < n_used, tile_expert, 0)))
    return src.reshape(n_tiles, 1, tile), pos, tile_expert.astype(jnp.int32), n_used.reshape(1).astype(jnp.int32)


def _moe_in_kernel(te_ref, nu_ref, src_ref, nxt_ref, h_hbm, w1_ref, w3_ref, o_ref, rows_ref, a_ref, sem, *, tile):
    i = pl.program_id(0)
    j = pl.program_id(1)
    used = i < nu_ref[0]
    slot = i % 2

    def gather(idx_ref, s):
        def issue(p, carry):
            pltpu.make_async_copy(h_hbm.at[pl.ds(idx_ref[0, p], 1), :], rows_ref.at[s, pl.ds(p, 1), :],
                                  sem.at[s]).start()
            return carry
        lax.fori_loop(0, tile, issue, 0, unroll=8)

    @pl.when(jnp.logical_and(i == 0, j == 0))
    def _():
        gather(src_ref, 0)

    @pl.when(jnp.logical_and(i + 1 < nu_ref[0], j == 0))
    def _():
        gather(nxt_ref, 1 - slot)

    @pl.when(jnp.logical_and(used, j == 0))
    def _():
        pltpu.make_async_copy(h_hbm.at[pl.ds(0, tile), :], rows_ref.at[slot], sem.at[slot]).wait()
        a_ref[...] = rows_ref[slot].astype(BF16)

    @pl.when(used)
    def _():
        a = a_ref[...]
        o_ref[...] = (_silu(_dot(a, w1_ref[...])) * _dot(a, w3_ref[...])).astype(o_ref.dtype)

    @pl.when(jnp.logical_not(used))
    def _():
        o_ref[...] = jnp.zeros_like(o_ref)


def _moe_out_kernel(te_ref, nu_ref, a_ref, w_ref, o_ref):
    used = pl.program_id(0) < nu_ref[0]

    @pl.when(used)
    def _():
        o_ref[...] = _dot(a_ref[...], w_ref[...])

    @pl.when(jnp.logical_not(used))
    def _():
        o_ref[...] = jnp.zeros_like(o_ref)


def _moe_combine_kernel(pos_ref, nxt_ref, y_hbm, x_ref, g_ref, route_ref, *rest, tc, n_tiles, normed):
    o_ref, rows_ref, sem = rest[-3:]
    i = pl.program_id(0)
    slot = i % 2

    def gather(idx_ref, s):
        def issue(t, carry):
            for c in range(TOP_K):
                pltpu.make_async_copy(y_hbm.at[pl.ds(idx_ref[0, c * tc + t], 1), :],
                                      rows_ref.at[s, c, pl.ds(t, 1), :], sem.at[s]).start()
            return carry
        lax.fori_loop(0, tc, issue, 0, unroll=8)

    @pl.when(i == 0)
    def _():
        gather(pos_ref, 0)

    @pl.when(i + 1 < n_tiles)
    def _():
        gather(nxt_ref, 1 - slot)

    for c in range(TOP_K):
        pltpu.make_async_copy(y_hbm.at[pl.ds(0, tc), :], rows_ref.at[slot, c], sem.at[slot]).wait()
    route = route_ref[...]
    moe = route[:, 0:1] * rows_ref[slot, 0] + route[:, 1:2] * rows_ref[slot, 1]
    o = x_ref[...] + g_ref[...] * moe
    if normed:
        o = o * lax.rsqrt(jnp.mean(o * o, axis=-1, keepdims=True) + EPS) * rest[0][...]
    o_ref[...] = o


def _moe(h, route, w1, w3, w2, x_all, mod3, which_gate, n_rows, seq, n_batch, tn=512, tc=256, out_norm_w=None):
    n, d = h.shape
    n_experts, _, f = w1.shape
    tile = MOE_TILE
    src, pos, tile_expert, n_used = _moe_dispatch(route, n_experts, tile)
    n_tiles = src.shape[0]
    smem = functools.partial(pl.BlockSpec, memory_space=pltpu.SMEM)
    act = pl.pallas_call(
        functools.partial(_moe_in_kernel, tile=tile),
        grid_spec=pltpu.PrefetchScalarGridSpec(
            num_scalar_prefetch=2, grid=(n_tiles, f // tn),
            in_specs=[smem((None, 1, tile), lambda i, j, te, nu: (i, 0, 0)),
                      smem((None, 1, tile), lambda i, j, te, nu: (jnp.minimum(i + 1, n_tiles - 1), 0, 0)),
                      pl.BlockSpec(memory_space=pl.ANY),
                      pl.BlockSpec((None, d, tn), lambda i, j, te, nu: (te[i], 0, j)),
                      pl.BlockSpec((None, d, tn), lambda i, j, te, nu: (te[i], 0, j))],
            out_specs=pl.BlockSpec((tile, tn), lambda i, j, te, nu: (i, j)),
            scratch_shapes=[pltpu.VMEM((2, tile, d), F32), pltpu.VMEM((tile, d), BF16),
                            pltpu.SemaphoreType.DMA((2,))]),
        out_shape=jax.ShapeDtypeStruct((n_tiles * tile, f), BF16),
        compiler_params=_params("arbitrary", "arbitrary"), name="moe_in",
    )(tile_expert, n_used, src, src, h, w1, w3)
    tn2 = _row_tile(d, 1024)
    y = pl.pallas_call(
        _moe_out_kernel,
        grid_spec=pltpu.PrefetchScalarGridSpec(
            num_scalar_prefetch=2, grid=(n_tiles, d // tn2),
            in_specs=[pl.BlockSpec((tile, f), lambda i, j, te, nu: (i, 0)),
                      pl.BlockSpec((None, f, tn2), lambda i, j, te, nu: (te[i], 0, j))],
            out_specs=pl.BlockSpec((tile, tn2), lambda i, j, te, nu: (i, j))),
        out_shape=jax.ShapeDtypeStruct((n_tiles * tile, d), F32),
        compiler_params=_params("arbitrary", "arbitrary"), name="moe_out",
    )(tile_expert, n_used, act, w2)
    pos2 = pos.reshape(TOP_K, n // tc, tc).transpose(1, 0, 2).reshape(n // tc, 1, TOP_K * tc)
    normed = out_norm_w is not None
    norm_specs = [pl.BlockSpec((1, d), lambda i: (0, 0))] if normed else []
    norm_args = [out_norm_w.reshape(1, d)] if normed else []
    n_ctiles = n_rows // tc
    return pl.pallas_call(
        functools.partial(_moe_combine_kernel, tc=tc, n_tiles=n_ctiles, normed=normed),
        grid=(n_ctiles,),
        in_specs=[smem((None, 1, TOP_K * tc), lambda i: (i, 0, 0)),
                  smem((None, 1, TOP_K * tc), lambda i: (jnp.minimum(i + 1, n_ctiles - 1), 0, 0)),
                  pl.BlockSpec(memory_space=pl.ANY),
                  pl.BlockSpec((tc, d), lambda i: (i, 0)),
                  _mod_spec(which_gate, tc, seq, n_batch, d),
                  pl.BlockSpec((tc, LANES), lambda i: (i, 0))] + norm_specs,
        out_specs=pl.BlockSpec((tc, d), lambda i: (i, 0)),
        out_shape=jax.ShapeDtypeStruct((n_rows, d), F32),
        scratch_shapes=[pltpu.VMEM((2, TOP_K, tc, d), F32), pltpu.SemaphoreType.DMA((2,))],
        compiler_params=_params("arbitrary"), name="moe_combine",
    )(pos2, pos2, y, x_all, mod3, route, *norm_args)


def _na_bias_table(rpb):
    n_heads = rpb.shape[0]
    n_dr = 2 * NA_WIN_ROWS - 1
    n_dc = 2 * NA_WIN_COLS - 1
    cols = jnp.arange(GRID_W)
    col_start = jnp.clip(cols - NA_WIN_COLS // 2, 0, GRID_W - NA_WIN_COLS)
    in_win = (cols[None, :] >= col_start[:, None]) & (cols[None, :] < col_start[:, None] + NA_WIN_COLS)
    dc = cols[None, :] - cols[:, None] + NA_WIN_COLS - 1
    expand = ((dc[None] == jnp.arange(n_dc)[:, None, None]) & in_win[None]).astype(F32)
    t = jnp.dot(rpb.reshape(-1, n_dc), expand.reshape(n_dc, GRID_W * GRID_W), precision=lax.Precision.HIGHEST)
    t = jnp.where(in_win.reshape(1, 1, -1), t.reshape(n_heads, n_dr, -1), NEG_BIG)
    t = t.reshape(n_heads, n_dr, GRID_W, GRID_W)
    masked = jnp.full((n_heads, GRID_W, GRID_W), NEG_BIG, F32)
    kinds = []
    for kind in range(3):
        q_rows = []
        for i in range(NA_QROWS):
            lo, dr0 = ((0, NA_WIN_ROWS - 1 - i), (i, NA_WIN_ROWS // 2 - 1),
                       (NA_QROWS, NA_WIN_ROWS - 1 - NA_QROWS - i))[kind]
            pieces = [t[:, dr0 + j - lo] if lo <= j < lo + NA_WIN_ROWS else masked for j in range(NA_KROWS)]
            q_rows.append(jnp.concatenate(pieces, axis=-1))
        kinds.append(jnp.concatenate(q_rows, axis=1))
    return jnp.stack(kinds, axis=1)


NA_QROWS = 4
NA_KROWS = NA_WIN_ROWS + NA_QROWS


def _na_kernel(q_ref, k_ref, v_ref, qc_ref, kc_ref, vc_ref, bias_ref, o_ref, oc_ref, *, rows, scale):
    kc = kc_ref[...]
    vc = vc_ref[...]
    nq = NA_QROWS * GRID_W
    nk = NA_KROWS * GRID_W
    n_blocks = rows // NA_QROWS

    def block(i, carry):
        base = jnp.clip(i * NA_QROWS - NA_WIN_ROWS // 2, 0, rows - NA_KROWS)
        kind = jnp.where(i == 0, 0, jnp.where(i == n_blocks - 1, 2, 1))
        q0 = pl.multiple_of(i * nq, nq)
        k0 = pl.multiple_of(base * GRID_W, nq)
        q = q_ref[pl.ds(q0, nq), :]
        kw = k_ref[pl.ds(k0, nk), :]
        vw = v_ref[pl.ds(k0, nk), :]
        s_w = lax.dot_general(q, kw, _NT, preferred_element_type=F32) * scale + bias_ref[kind]
        s_c = lax.dot_general(q, kc, _NT, preferred_element_type=F32) * scale
        m = jnp.maximum(jnp.max(s_w, axis=-1, keepdims=True), jnp.max(s_c, axis=-1, keepdims=True))
        p_w = jnp.exp(s_w - m)
        p_c = jnp.exp(s_c - m)
        den = jnp.sum(p_w, axis=-1, keepdims=True) + jnp.sum(p_c, axis=-1, keepdims=True)
        o = _dot(p_w.astype(BF16), vw) + _dot(p_c.astype(BF16), vc)
        o_ref[pl.ds(q0, nq), :] = (o / den).astype(o_ref.dtype)
        return carry

    lax.fori_loop(0, n_blocks, block, 0, unroll=2)

    s = lax.dot_general(qc_ref[...], kc, _NT, preferred_element_type=F32) * scale
    p = jnp.exp(s - jnp.max(s, axis=-1, keepdims=True))
    o = _dot(p.astype(BF16), vc) / jnp.sum(p, axis=-1, keepdims=True)
    oc_ref[...] = o.astype(oc_ref.dtype)


def _neighbourhood_attention(proj, bias_tab, n_batch, seq, ctx_len, n_heads, col0):
    dh = LANES
    rows = seq // GRID_W
    cb = n_batch * seq // ctx_len
    lat = lambda part: pl.BlockSpec((seq, dh), lambda b, h: (b, col0 + part * n_heads + h))
    ctx = lambda part: pl.BlockSpec((ctx_len, dh), lambda b, h: (cb + b, col0 + part * n_heads + h))
    return pl.pallas_call(
        functools.partial(_na_kernel, rows=rows, scale=dh ** -0.5),
        grid=(n_batch, n_heads),
        in_specs=[lat(0), lat(1), lat(2), ctx(0), ctx(1), ctx(2),
                  pl.BlockSpec((None, 3, NA_QROWS * GRID_W, NA_KROWS * GRID_W), lambda b, h: (h, 0, 0, 0))],
        out_specs=[pl.BlockSpec((seq, dh), lambda b, h: (b, h)),
                   pl.BlockSpec((ctx_len, dh), lambda b, h: (b, h))],
        out_shape=[jax.ShapeDtypeStruct((n_batch * seq, n_heads * dh), BF16),
                   jax.ShapeDtypeStruct((n_batch * ctx_len, n_heads * dh), BF16)],
        compiler_params=_params("parallel", "parallel"), name="neighbourhood_attention",
    )(proj, proj, proj, proj, proj, proj, bias_tab)


def _rope_tables(seq, dk):
    quarter = dk // 4
    inv_freq = ROPE_BASE ** (-jnp.arange(quarter, dtype=F32) / quarter)
    pos = jnp.arange(seq, dtype=jnp.int32)
    row = (pos // GRID_W).astype(F32)[:, None] * inv_freq
    col = (pos % GRID_W).astype(F32)[:, None] * inv_freq
    cos = jnp.concatenate([jnp.cos(row), jnp.cos(row), jnp.cos(col), jnp.cos(col)], axis=-1)
    sin = jnp.concatenate([-jnp.sin(row), jnp.sin(row), -jnp.sin(col), jnp.sin(col)], axis=-1)
    return cos, sin


GLA_BLOCK_CHUNKS = 4


def _gla_kernel(q_ref, k_ref, v_ref, g_ref, lr_ref, qc_ref, kc_ref, vc_ref, gc_ref, lrc_ref,
                wg_ref, bg_ref, nw_ref, cos_ref, sin_ref, tri_ref, keep_ref, vmask_ref, o_ref, oc_ref,
                s_ref, acc_ref, accc_ref, *, n_lat, n_ctx):
    dk = q_ref.shape[1]
    dv = v_ref.shape[1]
    c = GLA_CHUNK
    nb_lat = min(GLA_BLOCK_CHUNKS, n_lat)
    nb_ctx = min(GLA_BLOCK_CHUNKS, n_ctx)

    def swap_quarters(t):
        lane = lax.broadcasted_iota(jnp.int32, t.shape, 1)
        first_quarter = (lane % (dk // 2)) < (dk // 4)
        return jnp.where(first_quarter, pltpu.roll(t, dk - dk // 4, 1), pltpu.roll(t, dk // 4, 1))

    def block(refs, blk, nb, d, rope):
        qr, kr, vr, lrr = refs
        n = nb * c
        rows = pl.ds(pl.multiple_of(blk * n, n), n)
        q = qr[rows, :].astype(F32) * (dk ** -0.5)
        k = kr[rows, :].astype(F32)
        if rope:
            cos = cos_ref[rows, :]
            sin = sin_ref[rows, :]
            q = q * cos + swap_quarters(q) * sin
            k = k * cos + swap_quarters(k) * sin
        v = vr[rows, :]
        z = _dot(lrr[rows, :].astype(BF16), wg_ref[d]) + bg_ref[d:d + 1, :]
        logg = -_softplus(-z) * (1.0 / GLA_NORMALIZER)
        g1 = logg.astype(BF16)
        rem = logg - g1.astype(F32)
        g2 = rem.astype(BF16)
        g3 = (rem - g2.astype(F32)).astype(BF16)
        b3 = _dot(tri_ref[d, :n, :n], jnp.concatenate([g1, g2, g3], axis=1))
        b = b3[:, :dk] + b3[:, dk:2 * dk] + b3[:, 2 * dk:]
        end_row = c - 1 if d == 0 else 0
        ends = [b[m * c + end_row:m * c + end_row + 1, :] for m in range(nb)]
        b_end = jnp.concatenate([jnp.broadcast_to(e, (c, dk)) for e in ends], axis=0)
        q_dec = (q * jnp.exp(b)).astype(BF16)
        k_inv = (k * jnp.exp(-b)).astype(BF16)
        k_end_t = (k * jnp.exp(b_end - b)).T.astype(BF16)
        att = lax.dot_general(q_dec, k_inv, _NT, preferred_element_type=F32)
        att = jnp.where(keep_ref[d, :n, :n] != 0.0, att, 0.0).astype(BF16)
        o_intra = _dot(att, v)
        v_bd = jnp.concatenate([v] * nb, axis=1) * vmask_ref[:n, :nb * dv]
        d_all = _dot(k_end_t, v_bd)
        ends_t = jnp.concatenate(ends + [jnp.zeros((dk - nb, dk), F32)], axis=0).T
        state = s_ref[...]
        outs = [None] * nb
        for m in (range(nb) if d == 0 else range(nb - 1, -1, -1)):
            sl = slice(m * c, (m + 1) * c)
            outs[m] = o_intra[sl, :] + _dot(q_dec[sl, :], state.astype(BF16))
            decay = jnp.broadcast_to(jnp.exp(ends_t[:, m:m + 1]), state.shape)
            state = decay * state + d_all[:, m * dv:(m + 1) * dv]
        s_ref[...] = state
        return jnp.concatenate(outs, axis=0)

    def finish(o, g):
        y = o * lax.rsqrt(jnp.mean(o * o, axis=-1, keepdims=True) + EPS) * nw_ref[...]
        return (y * _silu(g.astype(F32))).astype(o_ref.dtype)

    lat = (q_ref, k_ref, v_ref, lr_ref)
    ctx = (qc_ref, kc_ref, vc_ref, lrc_ref)

    def fwd_ctx(i, carry):
        n = nb_ctx * c
        accc_ref[pl.ds(pl.multiple_of(i * n, n), n), :] = block(ctx, i, nb_ctx, 0, False)
        return carry

    def fwd_lat(i, carry):
        n = nb_lat * c
        acc_ref[pl.ds(pl.multiple_of(i * n, n), n), :] = block(lat, i, nb_lat, 0, True)
        return carry

    def bwd_ctx(j, carry):
        i = n_ctx // nb_ctx - 1 - j
        n = nb_ctx * c
        rows = pl.ds(pl.multiple_of(i * n, n), n)
        oc_ref[rows, :] = finish(accc_ref[rows, :] + block(ctx, i, nb_ctx, 1, False), gc_ref[rows, :])
        return carry

    def bwd_lat(j, carry):
        i = n_lat // nb_lat - 1 - j
        n = nb_lat * c
        rows = pl.ds(pl.multiple_of(i * n, n), n)
        o_ref[rows, :] = finish(acc_ref[rows, :] + block(lat, i, nb_lat, 1, True), g_ref[rows, :])
        return carry

    s_ref[...] = jnp.zeros_like(s_ref)
    lax.fori_loop(0, n_ctx // nb_ctx, fwd_ctx, 0)
    lax.fori_loop(0, n_lat // nb_lat, fwd_lat, 0, unroll=2)
    s_ref[...] = jnp.zeros_like(s_ref)
    lax.fori_loop(0, n_ctx // nb_ctx, bwd_ctx, 0)
    lax.fori_loop(0, n_lat // nb_lat, bwd_lat, 0, unroll=2)


def _gla(proj, lr, wg_pad, bg, norm_w, n_batch, seq, ctx_len, n_heads, dk, dv, col_q):
    n_lat = seq // GLA_CHUNK
    n_ctx = ctx_len // GLA_CHUNK
    assert dk == LANES and dv % dk == 0
    assert n_lat % min(GLA_BLOCK_CHUNKS, n_lat) == 0 and n_ctx % min(GLA_BLOCK_CHUNKS, n_ctx) == 0
    cb = n_batch * seq // ctx_len
    vq = dv // dk
    col_v = (col_q + 2 * n_heads) // vq
    col_g = col_v + n_heads
    cos, sin = _rope_tables(seq, dk)
    nblk = GLA_BLOCK_CHUNKS * GLA_CHUNK
    r = jnp.arange(nblk)
    same_chunk = (r[:, None] // GLA_CHUNK) == (r[None, :] // GLA_CHUNK)
    keep = jnp.stack([same_chunk & (r[:, None] >= r[None, :]), same_chunk & (r[:, None] <= r[None, :])])
    keep = keep.astype(F32)
    vmask = ((r[:, None] // GLA_CHUNK) == (jnp.arange(GLA_BLOCK_CHUNKS * dv)[None, :] // dv)).astype(BF16)

    def spec(rows, width, row_block, col):
        return pl.BlockSpec((rows, width), lambda b, h: (row_block(b), col(h)))
    lat_rb = lambda b: b
    ctx_rb = lambda b: cb + b
    in_specs = []
    for rows, rb in ((seq, lat_rb), (ctx_len, ctx_rb)):
        in_specs += [spec(rows, dk, rb, lambda h: col_q + h),
                     spec(rows, dk, rb, lambda h: col_q + n_heads + h),
                     spec(rows, dv, rb, lambda h: col_v + h),
                     spec(rows, dv, rb, lambda h: col_g + h),
                     spec(rows, LANES, rb, lambda h: 0)]
    in_specs += [pl.BlockSpec((2, LANES, dk), lambda b, h: (0, 0, h)),
                 pl.BlockSpec((2, dk), lambda b, h: (0, h)),
                 pl.BlockSpec((1, dv), lambda b, h: (0, 0)),
                 pl.BlockSpec((seq, dk), lambda b, h: (0, 0)),
                 pl.BlockSpec((seq, dk), lambda b, h: (0, 0)),
                 pl.BlockSpec((2, nblk, nblk), lambda b, h: (0, 0, 0)),
                 pl.BlockSpec((2, nblk, nblk), lambda b, h: (0, 0, 0)),
                 pl.BlockSpec((nblk, GLA_BLOCK_CHUNKS * dv), lambda b, h: (0, 0))]
    return pl.pallas_call(
        functools.partial(_gla_kernel, n_lat=n_lat, n_ctx=n_ctx),
        grid=(n_batch, n_heads), in_specs=in_specs,
        out_specs=[pl.BlockSpec((seq, dv), lambda b, h: (b, h)),
                   pl.BlockSpec((ctx_len, dv), lambda b, h: (b, h))],
        out_shape=[jax.ShapeDtypeStruct((n_batch * seq, n_heads * dv), BF16),
                   jax.ShapeDtypeStruct((n_batch * ctx_len, n_heads * dv), BF16)],
        scratch_shapes=[pltpu.VMEM((dk, dv), F32), pltpu.VMEM((seq, dv), F32),
                        pltpu.VMEM((ctx_len, dv), F32)],
        compiler_params=_params("parallel", "parallel"), name="gla",
    )(proj, proj, proj, proj, lr, proj, proj, proj, proj, lr,
      wg_pad, bg, norm_w.reshape(1, dv), cos, sin, keep.astype(BF16), keep, vmask)


def _lru_pitch(n):
    seg = n // SUBLANES
    assert seg % SUBLANES == 0
    return seg + 4


def _lru_kernel(x_ref, g_ref, xc_ref, gc_ref, cw_ref, cb_ref, w4_ref, b4_ref, lam_ref, o_ref, oc_ref,
                xpad_ref, af_ref, cf_ref, ab_ref, cbk_ref, hf_ref, pf_ref, hb_ref, pb_ref, *, seq, ctx_len):
    bw = x_ref.shape[1]
    pad = SUBLANES
    sp = _softplus(-lam_ref[...])
    cw = cw_ref[...]
    w4 = w4_ref[...]
    b4 = b4_ref[...]

    def put(ref, n, t0, val):
        seg, pitch = n // SUBLANES, _lru_pitch(n)
        for s in range(SUBLANES):
            lo, hi = max(t0, s * seg), min(t0 + val.shape[0], (s + 1) * seg)
            if lo < hi:
                ref[s * pitch + lo - s * seg:s * pitch + hi - s * seg, :] = val[lo - t0:hi - t0, :]

    def coeffs(src_ref, n):
        xpad_ref[0:pad, :] = jnp.zeros((pad, bw), F32)
        xpad_ref[pad + n:pad + n + pad, :] = jnp.zeros((pad, bw), F32)
        tile = min(n, 512)
        for t0 in range(0, n, tile):
            xpad_ref[pad + t0:pad + t0 + tile, :] = src_ref[t0:t0 + tile, :].astype(F32)
        for t0 in range(0, n, tile):
            xc = cb_ref[...]
            for j in range(LRU_CONV):
                lo = pad + t0 + j - LRU_CONV // 2
                xc = xc + cw[j:j + 1, :] * xpad_ref[lo:lo + tile, :]
            z = _dot(xc.astype(BF16), w4) + b4
            for d, (a_ref, c_ref) in enumerate(((af_ref, cf_ref), (ab_ref, cbk_ref))):
                r = jax.nn.sigmoid(z[:, (2 * d) * bw:(2 * d + 1) * bw])
                i = jax.nn.sigmoid(z[:, (2 * d + 1) * bw:(2 * d + 2) * bw])
                log_a = (-LRU_C) * r * sp[d:d + 1, :]
                a = jnp.exp(log_a)
                put(a_ref, n, t0, a)
                put(c_ref, n, t0, jnp.sqrt(1.0 - a * a) * (i * xc))

    def scan(n, h0_f, h0_b):
        seg, pitch = n // SUBLANES, _lru_pitch(n)

        def step(g, carry):
            hf, pf, hb, pb = carry
            fwd = pl.ds(g, SUBLANES, stride=pitch)
            bwd = pl.ds(seg - 1 - g, SUBLANES, stride=pitch)
            a = af_ref[fwd, :]
            hf = a * hf + cf_ref[fwd, :]
            pf = a * pf
            hf_ref[fwd, :] = hf
            pf_ref[fwd, :] = pf
            a = ab_ref[bwd, :]
            hb = a * hb + cbk_ref[bwd, :]
            pb = a * pb
            hb_ref[bwd, :] = hb
            pb_ref[bwd, :] = pb
            return hf, pf, hb, pb

        zero = jnp.zeros((SUBLANES, bw), F32)
        one = jnp.ones((SUBLANES, bw), F32)
        hf, pf, hb, pb = lax.fori_loop(0, seg, step, (zero, one, zero, one), unroll=8)
        carry_f = [h0_f]
        for s in range(SUBLANES):
            carry_f.append(pf[s:s + 1, :] * carry_f[s] + hf[s:s + 1, :])
        carry_b = [h0_b]
        for s in range(SUBLANES - 1, -1, -1):
            carry_b.append(pb[s:s + 1, :] * carry_b[-1] + hb[s:s + 1, :])
        carry_b = carry_b[::-1]
        return carry_f, carry_b

    def emit(n, carry_f, carry_b, gate_ref, out_ref):
        seg, pitch = n // SUBLANES, _lru_pitch(n)
        for s in range(SUBLANES):
            rows = slice(s * seg, (s + 1) * seg)
            held = slice(s * pitch, s * pitch + seg)
            h = (hf_ref[held, :] + pf_ref[held, :] * carry_f[s]
                 + hb_ref[held, :] + pb_ref[held, :] * carry_b[s + 1])
            out_ref[rows, :] = (h * _gelu_tanh(gate_ref[rows, :].astype(F32))).astype(out_ref.dtype)

    zero_h = jnp.zeros((1, bw), F32)
    coeffs(xc_ref, ctx_len)
    cf, cbw = scan(ctx_len, zero_h, zero_h)
    emit(ctx_len, cf, cbw, gc_ref, oc_ref)
    coeffs(x_ref, seq)
    lf, lb = scan(seq, cf[SUBLANES], cbw[0])
    emit(seq, lf, lb, g_ref, o_ref)


def _rglru(proj, conv_w, conv_b, w4, b4, lam, n_batch, seq, ctx_len, col_x):
    n_blk = w4.shape[0]
    bw = LANES
    cb = n_batch * seq // ctx_len
    width = n_blk * bw
    return pl.pallas_call(
        functools.partial(_lru_kernel, seq=seq, ctx_len=ctx_len),
        grid=(n_batch, n_blk),
        in_specs=[pl.BlockSpec((seq, bw), lambda b, j: (b, col_x + j)),
                  pl.BlockSpec((seq, bw), lambda b, j: (b, col_x + n_blk + j)),
                  pl.BlockSpec((ctx_len, bw), lambda b, j: (cb + b, col_x + j)),
                  pl.BlockSpec((ctx_len, bw), lambda b, j: (cb + b, col_x + n_blk + j)),
                  pl.BlockSpec((LRU_CONV, bw), lambda b, j: (0, j)),
                  pl.BlockSpec((1, bw), lambda b, j: (0, j)),
                  pl.BlockSpec((None, bw, 4 * bw), lambda b, j: (j, 0, 0)),
                  pl.BlockSpec((None, 1, 4 * bw), lambda b, j: (j, 0, 0)),
                  pl.BlockSpec((2, bw), lambda b, j: (0, j))],
        out_specs=[pl.BlockSpec((seq, bw), lambda b, j: (b, j)),
                   pl.BlockSpec((ctx_len, bw), lambda b, j: (b, j))],
        out_shape=[jax.ShapeDtypeStruct((n_batch * seq, width), BF16),
                   jax.ShapeDtypeStruct((n_batch * ctx_len, width), BF16)],
        scratch_shapes=([pltpu.VMEM((seq + 2 * SUBLANES, bw), F32)]
                        + [pltpu.VMEM((SUBLANES * _lru_pitch(seq), bw), F32)] * 8),
        compiler_params=_params("parallel", "parallel"), name="rglru",
    )(proj, proj, proj, proj, conv_w, conv_b.reshape(1, width), w4, b4, lam)


def _row_tile(n_rows, cap):
    t = cap
    while n_rows % t:
        t //= 2
    return t


def kernel(x, c, ctx, c_ctx, w_mod, b_mod, norm_mix, norm_ffn, w_in, na_rpb, gla_wg, gla_bg, gla_norm,
           conv_w, conv_b, lru_wa, lru_ba, lru_wx, lru_bx, lru_lam, w_out, ffd_w1, ffd_w3, ffd_w2,
           router, moe_w1, moe_w3, moe_w2, final_norm):
    n_batch, seq, d = x.shape
    ctx_len = ctx.shape[1]
    depth = w_in.shape[0]
    n_lat = n_batch * seq
    n_tot = n_lat + n_batch * ctx_len
    na_heads = na_rpb.shape[1]
    na_w = na_heads * LANES
    gla_kw = gla_wg.shape[3]
    gla_heads = gla_kw // LANES
    gla_dv = gla_norm.shape[1]
    gla_vw = gla_heads * gla_dv
    lru_w = conv_w.shape[2]
    n_blk = lru_wa.shape[2]
    assert n_batch < SUBLANES and lru_w == n_blk * LANES and seq % (SUBLANES * GRID_W) == 0

    p_lr = 3 * na_w + 2 * gla_kw + 2 * gla_vw
    col_gla = 3 * na_w // LANES
    col_lru = p_lr // LANES

    tm = _row_tile(n_tot, 1024)
    tm_lat = _row_tile(n_lat, 1024)
    tr = _row_tile(n_tot, 256)

    streams = [x.reshape(n_lat, d), ctx.reshape(n_batch * ctx_len, d)]
    c8 = jnp.zeros((SUBLANES, d), F32).at[:n_batch].set(c).at[n_batch].set(c_ctx)

    for l in range(depth):
        last = l == depth - 1
        rows_out = n_lat if last else n_tot
        tmo = tm_lat if last else tm
        mod3 = _modulation(c8, w_mod, b_mod, l).reshape(SUBLANES * N_MOD, 1, d)

        h = _norm_mod(streams, norm_mix[l], mod3, 1, 0, n_tot, seq, n_batch, tr)
        w_l = w_in[l]
        w_main = jnp.concatenate([w_l[:, :p_lr], w_l[:, p_lr + 2 * GLA_LOWRANK:]], axis=1).astype(BF16)
        w_lr = jnp.pad(w_l[:, p_lr:p_lr + 2 * GLA_LOWRANK], ((0, 0), (0, LANES - 2 * GLA_LOWRANK))).astype(BF16)
        proj = _matmul(h, w_main, BF16, tm, _row_tile(w_main.shape[1], 512))
        lr = _matmul(h, w_lr, F32, tm, LANES)

        na_l, na_c = _neighbourhood_attention(proj, _na_bias_table(na_rpb[l]), n_batch, seq, ctx_len,
                                              na_heads, 0)
        wg = gla_wg[l]
        wg_pad = jnp.zeros((2, LANES, gla_kw), F32)
        wg_pad = wg_pad.at[0, :GLA_LOWRANK].set(wg[0]).at[1, GLA_LOWRANK:2 * GLA_LOWRANK].set(wg[1])
        gla_l, gla_c = _gla(proj, lr, wg_pad.astype(BF16), gla_bg[l], gla_norm[l], n_batch, seq, ctx_len,
                            gla_heads, LANES, gla_dv, col_gla)
        w4 = jnp.concatenate([lru_wa[l, 0], lru_wx[l, 0], lru_wa[l, 1], lru_wx[l, 1]], axis=-1).astype(BF16)
        b4 = jnp.concatenate([lru_ba[l, 0].reshape(n_blk, 1, LANES), lru_bx[l, 0].reshape(n_blk, 1, LANES),
                              lru_ba[l, 1].reshape(n_blk, 1, LANES), lru_bx[l, 1].reshape(n_blk, 1, LANES)],
                             axis=-1)
        lru_l, lru_c = _rglru(proj, conv_w[l], conv_b[l], w4, b4, lru_lam[l], n_batch, seq, ctx_len, col_lru)

        mixed = [[na_l, gla_l, lru_l]] + ([] if last else [[na_c, gla_c, lru_c]])
        x_all = _out_proj(mixed, w_out[l].astype(BF16), streams, mod3, 2, seq, n_batch, _row_tile(tm, 512),
                          _row_tile(d, 1024))

        j = l // 2
        if l % 2 == 0:
            h = _norm_mod([x_all], norm_ffn[l], mod3, 4, 3, rows_out, seq, n_batch, tr)
            act = _swiglu_in(h, ffd_w1[j].astype(BF16), ffd_w3[j].astype(BF16), tmo,
                             _row_tile(ffd_w1.shape[2], 512))
            x_all = _matmul_residual(act, ffd_w2[j].astype(BF16), x_all, mod3, 5, rows_out, seq, n_batch, tmo,
                                     _row_tile(d, 1024), _row_tile(act.shape[1], 2048))
        else:
            h, route = _norm_mod([x_all], norm_ffn[l], mod3, 4, 3, rows_out, seq, n_batch, tr, router=router[j])
            x_all = _moe(h, route, moe_w1[j].astype(BF16), moe_w3[j].astype(BF16), moe_w2[j].astype(BF16),
                         x_all, mod3, 5, rows_out, seq, n_batch, tn=_row_tile(moe_w1.shape[3], 512),
                         tc=_row_tile(rows_out, 256), out_norm_w=final_norm if last else None)
        streams = [x_all]

    out = x_all if depth % 2 == 0 else _final_norm(x_all, final_norm, n_lat, tr)
    return out.reshape(n_batch, seq, d)
```

```python
import functools

import jax
import jax.numpy as jnp
from jax import lax
from jax.experimental import pallas as pl
from jax.experimental.pallas import tpu as pltpu

F32 = jnp.float32
BF16 = jnp.bfloat16

EPS = 1e-6
ROPE_BASE = 10000.0
GRID_W = 64
NA_WIN_ROWS = 8
NA_WIN_COLS = 16
GLA_CHUNK = 64
GLA_LOWRANK = 16
GLA_NORMALIZER = 16.0
LRU_C = 8.0
LRU_CONV = 4
N_MOD = 6
TOP_K = 2
LANES = 128
SUBLANES = 8
VMEM_LIMIT = 56 * 1024 * 1024
NEG_BIG = -1e30

_NT = (((1,), (1,)), ((), ()))


def _params(*sem):
    return pltpu.CompilerParams(dimension_semantics=sem, vmem_limit_bytes=VMEM_LIMIT)


def _dot(a, b):
    return jnp.dot(a, b, preferred_element_type=F32)


def _split_bf16(x):
    hi = x.astype(BF16)
    lo = (x - hi.astype(F32)).astype(BF16)
    return hi, lo


def _softplus(x):
    return jnp.maximum(x, 0.0) + jnp.log1p(jnp.exp(-jnp.abs(x)))


def _silu(x):
    return x * jax.nn.sigmoid(x)


def _gelu_tanh(x):
    return 0.5 * x * (1.0 + jnp.tanh(0.7978845608028654 * (x + 0.044715 * (x * x * x))))


def _mod_kernel(c_ref, w_ref, b_ref, o_ref):
    c = c_ref[...]
    a_hi, a_lo = _split_bf16(_silu(c))
    w_hi, w_lo = _split_bf16(w_ref[...])
    o_ref[...] = _dot(a_hi, w_hi) + _dot(a_lo, w_hi) + _dot(a_hi, w_lo) + b_ref[...]


def _modulation(c8, w_mod, b_mod, layer, tn=512):
    d = c8.shape[1]
    n = w_mod.shape[2]
    return pl.pallas_call(
        _mod_kernel,
        grid=(n // tn,),
        in_specs=[pl.BlockSpec((SUBLANES, d), lambda j: (0, 0)),
                  pl.BlockSpec((None, d, tn), lambda j: (layer, 0, j)),
                  pl.BlockSpec((None, 1, tn), lambda j: (layer, 0, j))],
        out_specs=pl.BlockSpec((SUBLANES, tn), lambda j: (0, j)),
        out_shape=jax.ShapeDtypeStruct((SUBLANES, n), F32),
        compiler_params=_params("parallel"),
        name="modulation",
    )(c8, w_mod, b_mod.reshape(b_mod.shape[0], 1, n))


def _mod_spec(which, tile_rows, seq, n_batch, d_block, col_axis=None):
    def row(i):
        return jnp.minimum((i * tile_rows) // seq, n_batch) * N_MOD + which
    if col_axis is None:
        return pl.BlockSpec((None, 1, d_block), lambda i, *_: (row(i), 0, 0))
    return pl.BlockSpec((None, 1, d_block), lambda *g: (row(g[0]), 0, g[col_axis]))


def _norm_mod_body(x_ref, nw_ref, sc_ref, sh_ref):
    x = x_ref[...]
    y = x * lax.rsqrt(jnp.mean(x * x, axis=-1, keepdims=True) + EPS) * nw_ref[...]
    return y * (1.0 + sc_ref[...]) + sh_ref[...]


def _row_group_specs(group_rows, tile, cols, col_index):
    specs, bounds = [], [0]
    for rows in group_rows:
        first, n_tiles = bounds[-1], rows // tile
        specs.append(pl.BlockSpec(
            (tile, cols), lambda i, *g, first=first, n_tiles=n_tiles: (jnp.clip(i - first, 0, n_tiles - 1),
                                                                        col_index(*g))))
        bounds.append(first + n_tiles)
    return specs, bounds


def _in_group(i, bounds, g):
    return jnp.logical_and(i >= bounds[g], i < bounds[g + 1])


def _norm_mod_kernel(*refs, bounds):
    n_groups = len(bounds) - 1
    nw_ref, sc_ref, sh_ref, o_ref = refs[n_groups:]
    for g in range(n_groups):
        @pl.when(_in_group(pl.program_id(0), bounds, g))
        def _(g=g):
            o_ref[...] = _norm_mod_body(refs[g], nw_ref, sc_ref, sh_ref).astype(o_ref.dtype)


def _norm_route_kernel(x_ref, nw_ref, sc_ref, sh_ref, rhi_ref, rlo_ref, o_ref, g_ref, *, n_experts):
    h = _norm_mod_body(x_ref, nw_ref, sc_ref, sh_ref)
    o_ref[...] = h.astype(o_ref.dtype)
    h_hi, h_lo = _split_bf16(h)
    logits = _dot(h_hi, rhi_ref[...]) + _dot(h_lo, rhi_ref[...]) + _dot(h_hi, rlo_ref[...])
    lane = lax.broadcasted_iota(jnp.int32, logits.shape, 1)
    valid = lane < n_experts
    logits = jnp.where(valid, logits, NEG_BIG)
    e = jnp.exp(logits - jnp.max(logits, axis=-1, keepdims=True))
    probs = jnp.where(valid, e / jnp.sum(e, axis=-1, keepdims=True), -1.0)
    p1 = jnp.max(probs, axis=-1, keepdims=True)
    i1 = jnp.min(jnp.where(probs == p1, lane, LANES), axis=-1, keepdims=True)
    rest = jnp.where(lane == i1, -1.0, probs)
    p2 = jnp.max(rest, axis=-1, keepdims=True)
    i2 = jnp.min(jnp.where(rest == p2, lane, LANES), axis=-1, keepdims=True)
    den = p1 + p2
    g_ref[...] = (jnp.where(lane == 0, p1 / den, 0.0) + jnp.where(lane == 1, p2 / den, 0.0)
                  + jnp.where(lane == 2, i1.astype(F32), 0.0) + jnp.where(lane == 3, i2.astype(F32), 0.0))


def _norm_mod(streams, norm_w, mod3, which_scale, which_shift, n_rows, seq, n_batch, tr=256,
              router=None):
    d = streams[0].shape[1]
    group_rows = [n_rows] if len(streams) == 1 else [s.shape[0] for s in streams]
    assert sum(group_rows) == n_rows
    x_specs, bounds = _row_group_specs(group_rows, tr, d, lambda: 0)
    in_specs = x_specs + [pl.BlockSpec((1, d), lambda i: (0, 0)),
                          _mod_spec(which_scale, tr, seq, n_batch, d),
                          _mod_spec(which_shift, tr, seq, n_batch, d)]
    h_spec = pl.BlockSpec((tr, d), lambda i: (i, 0))
    h_shape = jax.ShapeDtypeStruct((n_rows, d), BF16)
    if router is None:
        return pl.pallas_call(
            functools.partial(_norm_mod_kernel, bounds=bounds), grid=(n_rows // tr,), in_specs=in_specs,
            out_specs=h_spec, out_shape=h_shape, compiler_params=_params("parallel"), name="norm_mod",
        )(*streams, norm_w.reshape(1, d), mod3, mod3)
    (x_all,) = streams
    n_experts = router.shape[1]
    r_pad = jnp.pad(router, ((0, 0), (0, LANES - n_experts)))
    r_hi = r_pad.astype(BF16)
    r_lo = (r_pad - r_hi.astype(F32)).astype(BF16)
    w_spec = pl.BlockSpec((d, LANES), lambda i: (0, 0))
    return pl.pallas_call(
        functools.partial(_norm_route_kernel, n_experts=n_experts),
        grid=(n_rows // tr,), in_specs=in_specs + [w_spec, w_spec],
        out_specs=[h_spec, pl.BlockSpec((tr, LANES), lambda i: (i, 0))],
        out_shape=[jax.ShapeDtypeStruct((n_rows, d), F32), jax.ShapeDtypeStruct((n_rows, LANES), F32)],
        compiler_params=_params("parallel"), name="norm_route",
    )(x_all, norm_w.reshape(1, d), mod3, mod3, r_hi, r_lo)


def _final_norm_kernel(x_ref, nw_ref, o_ref):
    x = x_ref[...]
    o_ref[...] = x * lax.rsqrt(jnp.mean(x * x, axis=-1, keepdims=True) + EPS) * nw_ref[...]


def _final_norm(x_all, norm_w, n_rows, tr=256):
    d = x_all.shape[1]
    return pl.pallas_call(
        _final_norm_kernel, grid=(n_rows // tr,),
        in_specs=[pl.BlockSpec((tr, d), lambda i: (i, 0)), pl.BlockSpec((1, d), lambda i: (0, 0))],
        out_specs=pl.BlockSpec((tr, d), lambda i: (i, 0)),
        out_shape=jax.ShapeDtypeStruct((n_rows, d), F32),
        compiler_params=_params("parallel"), name="final_norm",
    )(x_all, norm_w.reshape(1, d))


def _mm_kernel(a_ref, w_ref, o_ref):
    o_ref[...] = _dot(a_ref[...], w_ref[...]).astype(o_ref.dtype)


def _matmul(a, w, out_dtype, tm, tn):
    m, k = a.shape
    n = w.shape[1]
    return pl.pallas_call(
        _mm_kernel, grid=(m // tm, n // tn),
        in_specs=[pl.BlockSpec((tm, k), lambda i, j: (i, 0)),
                  pl.BlockSpec((k, tn), lambda i, j: (0, j))],
        out_specs=pl.BlockSpec((tm, tn), lambda i, j: (i, j)),
        out_shape=jax.ShapeDtypeStruct((m, n), out_dtype),
        compiler_params=_params("parallel", "parallel"), name="matmul",
    )(a, w)


def _mm_res_kernel(a_ref, w_ref, x_ref, g_ref, o_ref, acc_ref, *, nk):
    k = pl.program_id(2)

    @pl.when(k == 0)
    def _():
        acc_ref[...] = jnp.zeros_like(acc_ref)

    acc_ref[...] += _dot(a_ref[...], w_ref[...])

    @pl.when(k == nk - 1)
    def _():
        o_ref[...] = x_ref[...] + g_ref[...] * acc_ref[...]


def _matmul_residual(a, w, x_all, mod3, which_gate, n_rows, seq, n_batch, tm, tn, tk):
    k = a.shape[1]
    d = w.shape[1]
    nk = k // tk
    return pl.pallas_call(
        functools.partial(_mm_res_kernel, nk=nk), grid=(n_rows // tm, d // tn, nk),
        in_specs=[pl.BlockSpec((tm, tk), lambda i, j, kk: (i, kk)),
                  pl.BlockSpec((tk, tn), lambda i, j, kk: (kk, j)),
                  pl.BlockSpec((tm, tn), lambda i, j, kk: (i, j)),
                  _mod_spec(which_gate, tm, seq, n_batch, tn, col_axis=1)],
        out_specs=pl.BlockSpec((tm, tn), lambda i, j, kk: (i, j)),
        out_shape=jax.ShapeDtypeStruct((n_rows, d), F32),
        scratch_shapes=[pltpu.VMEM((tm, tn), F32)],
        compiler_params=_params("parallel", "parallel", "arbitrary"), name="matmul_residual",
    )(a, w, x_all, mod3)


def _out_proj_kernel(*refs, n_groups, widths, bounds, n_x):
    n_src = len(widths)
    w_ref = refs[n_groups * n_src]
    x_refs = refs[n_groups * n_src + 1:n_groups * n_src + 1 + n_x]
    g_ref, o_ref = refs[-2:]
    i = pl.program_id(0)
    for g in range(n_groups):
        @pl.when(_in_group(i, bounds, g))
        def _(g=g):
            acc, k0 = None, 0
            for s in range(n_src):
                part = _dot(refs[g * n_src + s][...], w_ref[k0:k0 + widths[s], :])
                acc = part if acc is None else acc + part
                k0 += widths[s]
            x_ref = x_refs[0] if n_x == 1 else x_refs[g]
            o_ref[...] = x_ref[...] + g_ref[...] * acc


def _out_proj(sources, w, streams, mod3, which_gate, seq, n_batch, tm, tn):
    k, d = w.shape
    group_rows = [grp[0].shape[0] for grp in sources]
    widths = [a.shape[1] for a in sources[0]]
    assert sum(widths) == k
    n_rows = sum(group_rows)
    a_specs, args = [], []
    for g, grp in enumerate(sources):
        for a in grp:
            specs, bounds = _row_group_specs(group_rows, tm, a.shape[1], lambda j: 0)
            a_specs.append(specs[g])
            args.append(a)
    if len(streams) == 1:
        x_specs = [pl.BlockSpec((tm, tn), lambda i, j: (i, j))]
    else:
        assert [s.shape[0] for s in streams] == group_rows
        x_specs, _ = _row_group_specs(group_rows, tm, tn, lambda j: j)
    return pl.pallas_call(
        functools.partial(_out_proj_kernel, n_groups=len(sources), widths=widths, bounds=bounds,
                          n_x=len(streams)),
        grid=(n_rows // tm, d // tn),
        in_specs=a_specs + [pl.BlockSpec((k, tn), lambda i, j: (0, j))] + x_specs
        + [_mod_spec(which_gate, tm, seq, n_batch, tn, col_axis=1)],
        out_specs=pl.BlockSpec((tm, tn), lambda i, j: (i, j)),
        out_shape=jax.ShapeDtypeStruct((n_rows, d), F32),
        compiler_params=_params("parallel", "parallel"), name="out_proj",
    )(*args, w, *streams, mod3)


def _swiglu_kernel(a_ref, w1_ref, w3_ref, o_ref):
    a = a_ref[...]
    o_ref[...] = (_silu(_dot(a, w1_ref[...])) * _dot(a, w3_ref[...])).astype(o_ref.dtype)


def _swiglu_in(a, w1, w3, tm, tn):
    m, k = a.shape
    f = w1.shape[1]
    return pl.pallas_call(
        _swiglu_kernel, grid=(m // tm, f // tn),
        in_specs=[pl.BlockSpec((tm, k), lambda i, j: (i, 0)),
                  pl.BlockSpec((k, tn), lambda i, j: (0, j)),
                  pl.BlockSpec((k, tn), lambda i, j: (0, j))],
        out_specs=pl.BlockSpec((tm, tn), lambda i, j: (i, j)),
        out_shape=jax.ShapeDtypeStruct((m, f), BF16),
        compiler_params=_params("parallel", "parallel"), name="swiglu_in",
    )(a, w1, w3)


MOE_TILE = 512


def _moe_dispatch(route, n_experts, tile):
    n = route.shape[0]
    e = jnp.concatenate([route[:, 2], route[:, 3]]).astype(jnp.int32)
    onehot = (e[:, None] == jnp.arange(n_experts, dtype=jnp.int32)[None, :]).astype(jnp.int32)
    csum = jnp.cumsum(onehot, axis=0)
    counts = csum[-1]
    rank = jnp.sum((csum - onehot) * onehot, axis=1)
    padded = (counts + tile - 1) // tile * tile
    ends = jnp.cumsum(padded)
    pos = jnp.sum(onehot * (ends - padded)[None, :], axis=1) + rank
    n_tiles = (TOP_K * n) // tile + n_experts
    tok = jnp.tile(jnp.arange(n, dtype=jnp.int32), TOP_K)
    src = jnp.zeros((n_tiles * tile,), jnp.int32).at[pos].set(tok, unique_indices=True)
    tile_id = jnp.arange(n_tiles, dtype=jnp.int32)
    tile_expert = jnp.sum((tile_id[:, None] * tile >= ends[None, :]).astype(jnp.int32), axis=1)
    n_used = ends[-1] // tile
    tile_expert = jnp.minimum(tile_expert, jnp.max(jnp.where(tile_id < n_used, tile_expert, 0)))
    return src.reshape(n_tiles, 1, tile), pos, tile_expert.astype(jnp.int32), n_used.reshape(1).astype(jnp.int32)


def _moe_in_kernel(te_ref, nu_ref, src_ref, nxt_ref, h_hbm, w1_ref, w3_ref, o_ref, rows_ref, a_ref, sem, *, tile):
    i = pl.program_id(0)
    j = pl.program_id(1)
    used = i < nu_ref[0]
    slot = i % 2

    def gather(idx_ref, s):
        def issue(p, carry):
            pltpu.make_async_copy(h_hbm.at[pl.ds(idx_ref[0, p], 1), :], rows_ref.at[s, pl.ds(p, 1), :],
                                  sem.at[s]).start()
            return carry
        lax.fori_loop(0, tile, issue, 0, unroll=8)

    @pl.when(jnp.logical_and(i == 0, j == 0))
    def _():
        gather(src_ref, 0)

    @pl.when(jnp.logical_and(i + 1 < nu_ref[0], j == 0))
    def _():
        gather(nxt_ref, 1 - slot)

    @pl.when(jnp.logical_and(used, j == 0))
    def _():
        pltpu.make_async_copy(h_hbm.at[pl.ds(0, tile), :], rows_ref.at[slot], sem.at[slot]).wait()
        a_ref[...] = rows_ref[slot].astype(BF16)

    @pl.when(used)
    def _():
        a = a_ref[...]
        o_ref[...] = (_silu(_dot(a, w1_ref[...])) * _dot(a, w3_ref[...])).astype(o_ref.dtype)

    @pl.when(jnp.logical_not(used))
    def _():
        o_ref[...] = jnp.zeros_like(o_ref)


def _moe_out_kernel(te_ref, nu_ref, a_ref, w_ref, o_ref):
    used = pl.program_id(0) < nu_ref[0]

    @pl.when(used)
    def _():
        o_ref[...] = _dot(a_ref[...], w_ref[...])

    @pl.when(jnp.logical_not(used))
    def _():
        o_ref[...] = jnp.zeros_like(o_ref)


def _moe_combine_kernel(pos_ref, nxt_ref, y_hbm, x_ref, g_ref, route_ref, *rest, tc, n_tiles, normed):
    o_ref, rows_ref, sem = rest[-3:]
    i = pl.program_id(0)
    slot = i % 2

    def gather(idx_ref, s):
        def issue(t, carry):
            for c in range(TOP_K):
                pltpu.make_async_copy(y_hbm.at[pl.ds(idx_ref[0, c * tc + t], 1), :],
                                      rows_ref.at[s, c, pl.ds(t, 1), :], sem.at[s]).start()
            return carry
        lax.fori_loop(0, tc, issue, 0, unroll=8)

    @pl.when(i == 0)
    def _():
        gather(pos_ref, 0)

    @pl.when(i + 1 < n_tiles)
    def _():
        gather(nxt_ref, 1 - slot)

    for c in range(TOP_K):
        pltpu.make_async_copy(y_hbm.at[pl.ds(0, tc), :], rows_ref.at[slot, c], sem.at[slot]).wait()
    route = route_ref[...]
    moe = route[:, 0:1] * rows_ref[slot, 0] + route[:, 1:2] * rows_ref[slot, 1]
    o = x_ref[...] + g_ref[...] * moe
    if normed:
        o = o * lax.rsqrt(jnp.mean(o * o, axis=-1, keepdims=True) + EPS) * rest[0][...]
    o_ref[...] = o


def _moe(h, route, w1, w3, w2, x_all, mod3, which_gate, n_rows, seq, n_batch, tn=512, tc=256, out_norm_w=None):
    n, d = h.shape
    n_experts, _, f = w1.shape
    tile = MOE_TILE
    src, pos, tile_expert, n_used = _moe_dispatch(route, n_experts, tile)
    n_tiles = src.shape[0]
    smem = functools.partial(pl.BlockSpec, memory_space=pltpu.SMEM)
    act = pl.pallas_call(
        functools.partial(_moe_in_kernel, tile=tile),
        grid_spec=pltpu.PrefetchScalarGridSpec(
            num_scalar_prefetch=2, grid=(n_tiles, f // tn),
            in_specs=[smem((None, 1, tile), lambda i, j, te, nu: (i, 0, 0)),
                      smem((None, 1, tile), lambda i, j, te, nu: (jnp.minimum(i + 1, n_tiles - 1), 0, 0)),
                      pl.BlockSpec(memory_space=pl.ANY),
                      pl.BlockSpec((None, d, tn), lambda i, j, te, nu: (te[i], 0, j)),
                      pl.BlockSpec((None, d, tn), lambda i, j, te, nu: (te[i], 0, j))],
            out_specs=pl.BlockSpec((tile, tn), lambda i, j, te, nu: (i, j)),
            scratch_shapes=[pltpu.VMEM((2, tile, d), F32), pltpu.VMEM((tile, d), BF16),
                            pltpu.SemaphoreType.DMA((2,))]),
        out_shape=jax.ShapeDtypeStruct((n_tiles * tile, f), BF16),
        compiler_params=_params("arbitrary", "arbitrary"), name="moe_in",
    )(tile_expert, n_used, src, src, h, w1, w3)
    tn2 = _row_tile(d, 2048)
    y = pl.pallas_call(
        _moe_out_kernel,
        grid_spec=pltpu.PrefetchScalarGridSpec(
            num_scalar_prefetch=2, grid=(n_tiles, d // tn2),
            in_specs=[pl.BlockSpec((tile, f), lambda i, j, te, nu: (i, 0)),
                      pl.BlockSpec((None, f, tn2), lambda i, j, te, nu: (te[i], 0, j))],
            out_specs=pl.BlockSpec((tile, tn2), lambda i, j, te, nu: (i, j))),
        out_shape=jax.ShapeDtypeStruct((n_tiles * tile, d), F32),
        compiler_params=_params("arbitrary", "arbitrary"), name="moe_out",
    )(tile_expert, n_used, act, w2)
    pos2 = pos.reshape(TOP_K, n // tc, tc).transpose(1, 0, 2).reshape(n // tc, 1, TOP_K * tc)
    normed = out_norm_w is not None
    norm_specs = [pl.BlockSpec((1, d), lambda i: (0, 0))] if normed else []
    norm_args = [out_norm_w.reshape(1, d)] if normed else []
    n_ctiles = n_rows // tc
    return pl.pallas_call(
        functools.partial(_moe_combine_kernel, tc=tc, n_tiles=n_ctiles, normed=normed),
        grid=(n_ctiles,),
        in_specs=[smem((None, 1, TOP_K * tc), lambda i: (i, 0, 0)),
                  smem((None, 1, TOP_K * tc), lambda i: (jnp.minimum(i + 1, n_ctiles - 1), 0, 0)),
                  pl.BlockSpec(memory_space=pl.ANY),
                  pl.BlockSpec((tc, d), lambda i: (i, 0)),
                  _mod_spec(which_gate, tc, seq, n_batch, d),
                  pl.BlockSpec((tc, LANES), lambda i: (i, 0))] + norm_specs,
        out_specs=pl.BlockSpec((tc, d), lambda i: (i, 0)),
        out_shape=jax.ShapeDtypeStruct((n_rows, d), F32),
        scratch_shapes=[pltpu.VMEM((2, TOP_K, tc, d), F32), pltpu.SemaphoreType.DMA((2,))],
        compiler_params=_params("arbitrary"), name="moe_combine",
    )(pos2, pos2, y, x_all, mod3, route, *norm_args)


def _na_bias_table(rpb):
    n_heads = rpb.shape[0]
    n_dr = 2 * NA_WIN_ROWS - 1
    n_dc = 2 * NA_WIN_COLS - 1
    cols = jnp.arange(GRID_W)
    col_start = jnp.clip(cols - NA_WIN_COLS // 2, 0, GRID_W - NA_WIN_COLS)
    in_win = (cols[None, :] >= col_start[:, None]) & (cols[None, :] < col_start[:, None] + NA_WIN_COLS)
    dc = cols[None, :] - cols[:, None] + NA_WIN_COLS - 1
    expand = ((dc[None] == jnp.arange(n_dc)[:, None, None]) & in_win[None]).astype(F32)
    t = jnp.dot(rpb.reshape(-1, n_dc), expand.reshape(n_dc, GRID_W * GRID_W), precision=lax.Precision.HIGHEST)
    t = jnp.where(in_win.reshape(1, 1, -1), t.reshape(n_heads, n_dr, -1), NEG_BIG)
    t = t.reshape(n_heads, n_dr, GRID_W, GRID_W)
    masked = jnp.full((n_heads, GRID_W, GRID_W), NEG_BIG, F32)
    kinds = []
    for kind in range(3):
        q_rows = []
        for i in range(NA_QROWS):
            lo, dr0 = ((0, NA_WIN_ROWS - 1 - i), (i, NA_WIN_ROWS // 2 - 1),
                       (NA_QROWS, NA_WIN_ROWS - 1 - NA_QROWS - i))[kind]
            pieces = [t[:, dr0 + j - lo] if lo <= j < lo + NA_WIN_ROWS else masked for j in range(NA_KROWS)]
            q_rows.append(jnp.concatenate(pieces, axis=-1))
        kinds.append(jnp.concatenate(q_rows, axis=1))
    return jnp.stack(kinds, axis=1)


NA_QROWS = 4
NA_KROWS = NA_WIN_ROWS + NA_QROWS


def _na_kernel(q_ref, k_ref, v_ref, qc_ref, kc_ref, vc_ref, bias_ref, o_ref, oc_ref, *, rows, scale):
    kc = kc_ref[...]
    vc = vc_ref[...]
    nq = NA_QROWS * GRID_W
    nk = NA_KROWS * GRID_W
    n_blocks = rows // NA_QROWS

    def block(i, carry):
        base = jnp.clip(i * NA_QROWS - NA_WIN_ROWS // 2, 0, rows - NA_KROWS)
        kind = jnp.where(i == 0, 0, jnp.where(i == n_blocks - 1, 2, 1))
        q0 = pl.multiple_of(i * nq, nq)
        k0 = pl.multiple_of(base * GRID_W, nq)
        q = q_ref[pl.ds(q0, nq), :]
        kw = k_ref[pl.ds(k0, nk), :]
        vw = v_ref[pl.ds(k0, nk), :]
        s_w = lax.dot_general(q, kw, _NT, preferred_element_type=F32) * scale + bias_ref[kind]
        s_c = lax.dot_general(q, kc, _NT, preferred_element_type=F32) * scale
        m = jnp.maximum(jnp.max(s_w, axis=-1, keepdims=True), jnp.max(s_c, axis=-1, keepdims=True))
        p_w = jnp.exp(s_w - m)
        p_c = jnp.exp(s_c - m)
        den = jnp.sum(p_w, axis=-1, keepdims=True) + jnp.sum(p_c, axis=-1, keepdims=True)
        o = _dot(p_w.astype(BF16), vw) + _dot(p_c.astype(BF16), vc)
        o_ref[pl.ds(q0, nq), :] = (o / den).astype(o_ref.dtype)
        return carry

    lax.fori_loop(0, n_blocks, block, 0, unroll=2)

    s = lax.dot_general(qc_ref[...], kc, _NT, preferred_element_type=F32) * scale
    p = jnp.exp(s - jnp.max(s, axis=-1, keepdims=True))
    o = _dot(p.astype(BF16), vc) / jnp.sum(p, axis=-1, keepdims=True)
    oc_ref[...] = o.astype(oc_ref.dtype)


def _neighbourhood_attention(proj, bias_tab, n_batch, seq, ctx_len, n_heads, col0):
    dh = LANES
    rows = seq // GRID_W
    cb = n_batch * seq // ctx_len
    lat = lambda part: pl.BlockSpec((seq, dh), lambda b, h: (b, col0 + part * n_heads + h))
    ctx = lambda part: pl.BlockSpec((ctx_len, dh), lambda b, h: (cb + b, col0 + part * n_heads + h))
    return pl.pallas_call(
        functools.partial(_na_kernel, rows=rows, scale=dh ** -0.5),
        grid=(n_batch, n_heads),
        in_specs=[lat(0), lat(1), lat(2), ctx(0), ctx(1), ctx(2),
                  pl.BlockSpec((None, 3, NA_QROWS * GRID_W, NA_KROWS * GRID_W), lambda b, h: (h, 0, 0, 0))],
        out_specs=[pl.BlockSpec((seq, dh), lambda b, h: (b, h)),
                   pl.BlockSpec((ctx_len, dh), lambda b, h: (b, h))],
        out_shape=[jax.ShapeDtypeStruct((n_batch * seq, n_heads * dh), BF16),
                   jax.ShapeDtypeStruct((n_batch * ctx_len, n_heads * dh), BF16)],
        compiler_params=_params("parallel", "parallel"), name="neighbourhood_attention",
    )(proj, proj, proj, proj, proj, proj, bias_tab)


def _rope_tables(seq, dk):
    quarter = dk // 4
    inv_freq = ROPE_BASE ** (-jnp.arange(quarter, dtype=F32) / quarter)
    pos = jnp.arange(seq, dtype=jnp.int32)
    row = (pos // GRID_W).astype(F32)[:, None] * inv_freq
    col = (pos % GRID_W).astype(F32)[:, None] * inv_freq
    cos = jnp.concatenate([jnp.cos(row), jnp.cos(row), jnp.cos(col), jnp.cos(col)], axis=-1)
    sin = jnp.concatenate([-jnp.sin(row), jnp.sin(row), -jnp.sin(col), jnp.sin(col)], axis=-1)
    return cos, sin


GLA_BLOCK_CHUNKS = 4


def _gla_kernel(q_ref, k_ref, v_ref, g_ref, lr_ref, qc_ref, kc_ref, vc_ref, gc_ref, lrc_ref,
                wg_ref, bg_ref, nw_ref, cos_ref, sin_ref, tri_ref, keep_ref, vmask_ref, o_ref, oc_ref,
                s_ref, acc_ref, accc_ref, *, n_lat, n_ctx):
    dk = q_ref.shape[1]
    dv = v_ref.shape[1]
    c = GLA_CHUNK
    nb_lat = min(GLA_BLOCK_CHUNKS, n_lat)
    nb_ctx = min(GLA_BLOCK_CHUNKS, n_ctx)

    def swap_quarters(t):
        lane = lax.broadcasted_iota(jnp.int32, t.shape, 1)
        first_quarter = (lane % (dk // 2)) < (dk // 4)
        return jnp.where(first_quarter, pltpu.roll(t, dk - dk // 4, 1), pltpu.roll(t, dk // 4, 1))

    def block(refs, blk, nb, d, rope):
        qr, kr, vr, lrr = refs
        n = nb * c
        rows = pl.ds(pl.multiple_of(blk * n, n), n)
        q = qr[rows, :].astype(F32) * (dk ** -0.5)
        k = kr[rows, :].astype(F32)
        if rope:
            cos = cos_ref[rows, :]
            sin = sin_ref[rows, :]
            q = q * cos + swap_quarters(q) * sin
            k = k * cos + swap_quarters(k) * sin
        v = vr[rows, :]
        z = _dot(lrr[rows, :].astype(BF16), wg_ref[d]) + bg_ref[d:d + 1, :]
        logg = -_softplus(-z) * (1.0 / GLA_NORMALIZER)
        g1 = logg.astype(BF16)
        rem = logg - g1.astype(F32)
        g2 = rem.astype(BF16)
        g3 = (rem - g2.astype(F32)).astype(BF16)
        b3 = _dot(tri_ref[d, :n, :n], jnp.concatenate([g1, g2, g3], axis=1))
        b = b3[:, :dk] + b3[:, dk:2 * dk] + b3[:, 2 * dk:]
        end_row = c - 1 if d == 0 else 0
        ends = [b[m * c + end_row:m * c + end_row + 1, :] for m in range(nb)]
        b_end = jnp.concatenate([jnp.broadcast_to(e, (c, dk)) for e in ends], axis=0)
        q_dec = (q * jnp.exp(b)).astype(BF16)
        k_inv = (k * jnp.exp(-b)).astype(BF16)
        k_end_t = (k * jnp.exp(b_end - b)).T.astype(BF16)
        att = lax.dot_general(q_dec, k_inv, _NT, preferred_element_type=F32)
        att = jnp.where(keep_ref[d, :n, :n] != 0.0, att, 0.0).astype(BF16)
        o_intra = _dot(att, v)
        v_bd = jnp.concatenate([v] * nb, axis=1) * vmask_ref[:n, :nb * dv]
        d_all = _dot(k_end_t, v_bd)
        ends_t = jnp.concatenate(ends + [jnp.zeros((dk - nb, dk), F32)], axis=0).T
        state = s_ref[...]
        outs = [None] * nb
        for m in (range(nb) if d == 0 else range(nb - 1, -1, -1)):
            sl = slice(m * c, (m + 1) * c)
            outs[m] = o_intra[sl, :] + _dot(q_dec[sl, :], state.astype(BF16))
            decay = jnp.broadcast_to(jnp.exp(ends_t[:, m:m + 1]), state.shape)
            state = decay * state + d_all[:, m * dv:(m + 1) * dv]
        s_ref[...] = state
        return jnp.concatenate(outs, axis=0)

    def finish(o, g):
        y = o * lax.rsqrt(jnp.mean(o * o, axis=-1, keepdims=True) + EPS) * nw_ref[...]
        return (y * _silu(g.astype(F32))).astype(o_ref.dtype)

    lat = (q_ref, k_ref, v_ref, lr_ref)
    ctx = (qc_ref, kc_ref, vc_ref, lrc_ref)

    def fwd_ctx(i, carry):
        n = nb_ctx * c
        accc_ref[pl.ds(pl.multiple_of(i * n, n), n), :] = block(ctx, i, nb_ctx, 0, False)
        return carry

    def fwd_lat(i, carry):
        n = nb_lat * c
        acc_ref[pl.ds(pl.multiple_of(i * n, n), n), :] = block(lat, i, nb_lat, 0, True)
        return carry

    def bwd_ctx(j, carry):
        i = n_ctx // nb_ctx - 1 - j
        n = nb_ctx * c
        rows = pl.ds(pl.multiple_of(i * n, n), n)
        oc_ref[rows, :] = finish(accc_ref[rows, :] + block(ctx, i, nb_ctx, 1, False), gc_ref[rows, :])
        return carry

    def bwd_lat(j, carry):
        i = n_lat // nb_lat - 1 - j
        n = nb_lat * c
        rows = pl.ds(pl.multiple_of(i * n, n), n)
        o_ref[rows, :] = finish(acc_ref[rows, :] + block(lat, i, nb_lat, 1, True), g_ref[rows, :])
        return carry

    s_ref[...] = jnp.zeros_like(s_ref)
    lax.fori_loop(0, n_ctx // nb_ctx, fwd_ctx, 0)
    lax.fori_loop(0, n_lat // nb_lat, fwd_lat, 0, unroll=2)
    s_ref[...] = jnp.zeros_like(s_ref)
    lax.fori_loop(0, n_ctx // nb_ctx, bwd_ctx, 0)
    lax.fori_loop(0, n_lat // nb_lat, bwd_lat, 0, unroll=2)


def _gla(proj, lr, wg_pad, bg, norm_w, n_batch, seq, ctx_len, n_heads, dk, dv, col_q):
    n_lat = seq // GLA_CHUNK
    n_ctx = ctx_len // GLA_CHUNK
    assert dk == LANES and dv % dk == 0
    assert n_lat % min(GLA_BLOCK_CHUNKS, n_lat) == 0 and n_ctx % min(GLA_BLOCK_CHUNKS, n_ctx) == 0
    cb = n_batch * seq // ctx_len
    vq = dv // dk
    col_v = (col_q + 2 * n_heads) // vq
    col_g = col_v + n_heads
    cos, sin = _rope_tables(seq, dk)
    nblk = GLA_BLOCK_CHUNKS * GLA_CHUNK
    r = jnp.arange(nblk)
    same_chunk = (r[:, None] // GLA_CHUNK) == (r[None, :] // GLA_CHUNK)
    keep = jnp.stack([same_chunk & (r[:, None] >= r[None, :]), same_chunk & (r[:, None] <= r[None, :])])
    keep = keep.astype(F32)
    vmask = ((r[:, None] // GLA_CHUNK) == (jnp.arange(GLA_BLOCK_CHUNKS * dv)[None, :] // dv)).astype(BF16)

    def spec(rows, width, row_block, col):
        return pl.BlockSpec((rows, width), lambda b, h: (row_block(b), col(h)))
    lat_rb = lambda b: b
    ctx_rb = lambda b: cb + b
    in_specs = []
    for rows, rb in ((seq, lat_rb), (ctx_len, ctx_rb)):
        in_specs += [spec(rows, dk, rb, lambda h: col_q + h),
                     spec(rows, dk, rb, lambda h: col_q + n_heads + h),
                     spec(rows, dv, rb, lambda h: col_v + h),
                     spec(rows, dv, rb, lambda h: col_g + h),
                     spec(rows, LANES, rb, lambda h: 0)]
    in_specs += [pl.BlockSpec((2, LANES, dk), lambda b, h: (0, 0, h)),
                 pl.BlockSpec((2, dk), lambda b, h: (0, h)),
                 pl.BlockSpec((1, dv), lambda b, h: (0, 0)),
                 pl.BlockSpec((seq, dk), lambda b, h: (0, 0)),
                 pl.BlockSpec((seq, dk), lambda b, h: (0, 0)),
                 pl.BlockSpec((2, nblk, nblk), lambda b, h: (0, 0, 0)),
                 pl.BlockSpec((2, nblk, nblk), lambda b, h: (0, 0, 0)),
                 pl.BlockSpec((nblk, GLA_BLOCK_CHUNKS * dv), lambda b, h: (0, 0))]
    return pl.pallas_call(
        functools.partial(_gla_kernel, n_lat=n_lat, n_ctx=n_ctx),
        grid=(n_batch, n_heads), in_specs=in_specs,
        out_specs=[pl.BlockSpec((seq, dv), lambda b, h: (b, h)),
                   pl.BlockSpec((ctx_len, dv), lambda b, h: (b, h))],
        out_shape=[jax.ShapeDtypeStruct((n_batch * seq, n_heads * dv), BF16),
                   jax.ShapeDtypeStruct((n_batch * ctx_len, n_heads * dv), BF16)],
        scratch_shapes=[pltpu.VMEM((dk, dv), F32), pltpu.VMEM((seq, dv), F32),
                        pltpu.VMEM((ctx_len, dv), F32)],
        compiler_params=_params("parallel", "parallel"), name="gla",
    )(proj, proj, proj, proj, lr, proj, proj, proj, proj, lr,
      wg_pad, bg, norm_w.reshape(1, dv), cos, sin, keep.astype(BF16), keep, vmask)


def _lru_pitch(n):
    seg = n // SUBLANES
    assert seg % SUBLANES == 0
    return seg + 4


def _lru_kernel(x_ref, g_ref, xc_ref, gc_ref, cw_ref, cb_ref, w4_ref, b4_ref, lam_ref, o_ref, oc_ref,
                xpad_ref, af_ref, cf_ref, ab_ref, cbk_ref, hf_ref, pf_ref, hb_ref, pb_ref, *, seq, ctx_len):
    bw = x_ref.shape[1]
    pad = SUBLANES
    sp = _softplus(-lam_ref[...])
    cw = cw_ref[...]
    w4 = w4_ref[...]
    b4 = b4_ref[...]

    def put(ref, n, t0, val):
        seg, pitch = n // SUBLANES, _lru_pitch(n)
        for s in range(SUBLANES):
            lo, hi = max(t0, s * seg), min(t0 + val.shape[0], (s + 1) * seg)
            if lo < hi:
                ref[s * pitch + lo - s * seg:s * pitch + hi - s * seg, :] = val[lo - t0:hi - t0, :]

    def coeffs(src_ref, n):
        xpad_ref[0:pad, :] = jnp.zeros((pad, bw), F32)
        xpad_ref[pad + n:pad + n + pad, :] = jnp.zeros((pad, bw), F32)
        tile = min(n, 512)
        for t0 in range(0, n, tile):
            xpad_ref[pad + t0:pad + t0 + tile, :] = src_ref[t0:t0 + tile, :].astype(F32)
        for t0 in range(0, n, tile):
            xc = cb_ref[...]
            for j in range(LRU_CONV):
                lo = pad + t0 + j - LRU_CONV // 2
                xc = xc + cw[j:j + 1, :] * xpad_ref[lo:lo + tile, :]
            z = _dot(xc.astype(BF16), w4) + b4
            for d, (a_ref, c_ref) in enumerate(((af_ref, cf_ref), (ab_ref, cbk_ref))):
                r = jax.nn.sigmoid(z[:, (2 * d) * bw:(2 * d + 1) * bw])
                i = jax.nn.sigmoid(z[:, (2 * d + 1) * bw:(2 * d + 2) * bw])
                log_a = (-LRU_C) * r * sp[d:d + 1, :]
                a = jnp.exp(log_a)
                put(a_ref, n, t0, a)
                put(c_ref, n, t0, jnp.sqrt(1.0 - a * a) * (i * xc))

    def scan(n, h0_f, h0_b):
        seg, pitch = n // SUBLANES, _lru_pitch(n)

        def step(g, carry):
            hf, pf, hb, pb = carry
            fwd = pl.ds(g, SUBLANES, stride=pitch)
            bwd = pl.ds(seg - 1 - g, SUBLANES, stride=pitch)
            a = af_ref[fwd, :]
            hf = a * hf + cf_ref[fwd, :]
            pf = a * pf
            hf_ref[fwd, :] = hf
            pf_ref[fwd, :] = pf
            a = ab_ref[bwd, :]
            hb = a * hb + cbk_ref[bwd, :]
            pb = a * pb
            hb_ref[bwd, :] = hb
            pb_ref[bwd, :] = pb
            return hf, pf, hb, pb

        zero = jnp.zeros((SUBLANES, bw), F32)
        one = jnp.ones((SUBLANES, bw), F32)
        hf, pf, hb, pb = lax.fori_loop(0, seg, step, (zero, one, zero, one), unroll=8)
        carry_f = [h0_f]
        for s in range(SUBLANES):
            carry_f.append(pf[s:s + 1, :] * carry_f[s] + hf[s:s + 1, :])
        carry_b = [h0_b]
        for s in range(SUBLANES - 1, -1, -1):
            carry_b.append(pb[s:s + 1, :] * carry_b[-1] + hb[s:s + 1, :])
        carry_b = carry_b[::-1]
        return carry_f, carry_b

    def emit(n, carry_f, carry_b, gate_ref, out_ref):
        seg, pitch = n // SUBLANES, _lru_pitch(n)
        for s in range(SUBLANES):
            rows = slice(s * seg, (s + 1) * seg)
            held = slice(s * pitch, s * pitch + seg)
            h = (hf_ref[held, :] + pf_ref[held, :] * carry_f[s]
                 + hb_ref[held, :] + pb_ref[held, :] * carry_b[s + 1])
            out_ref[rows, :] = (h * _gelu_tanh(gate_ref[rows, :].astype(F32))).astype(out_ref.dtype)

    zero_h = jnp.zeros((1, bw), F32)
    coeffs(xc_ref, ctx_len)
    cf, cbw = scan(ctx_len, zero_h, zero_h)
    emit(ctx_len, cf, cbw, gc_ref, oc_ref)
    coeffs(x_ref, seq)
    lf, lb = scan(seq, cf[SUBLANES], cbw[0])
    emit(seq, lf, lb, g_ref, o_ref)


def _rglru(proj, conv_w, conv_b, w4, b4, lam, n_batch, seq, ctx_len, col_x):
    n_blk = w4.shape[0]
    bw = LANES
    cb = n_batch * seq // ctx_len
    width = n_blk * bw
    return pl.pallas_call(
        functools.partial(_lru_kernel, seq=seq, ctx_len=ctx_len),
        grid=(n_batch, n_blk),
        in_specs=[pl.BlockSpec((seq, bw), lambda b, j: (b, col_x + j)),
                  pl.BlockSpec((seq, bw), lambda b, j: (b, col_x + n_blk + j)),
                  pl.BlockSpec((ctx_len, bw), lambda b, j: (cb + b, col_x + j)),
                  pl.BlockSpec((ctx_len, bw), lambda b, j: (cb + b, col_x + n_blk + j)),
                  pl.BlockSpec((LRU_CONV, bw), lambda b, j: (0, j)),
                  pl.BlockSpec((1, bw), lambda b, j: (0, j)),
                  pl.BlockSpec((None, bw, 4 * bw), lambda b, j: (j, 0, 0)),
                  pl.BlockSpec((None, 1, 4 * bw), lambda b, j: (j, 0, 0)),
                  pl.BlockSpec((2, bw), lambda b, j: (0, j))],
        out_specs=[pl.BlockSpec((seq, bw), lambda b, j: (b, j)),
                   pl.BlockSpec((ctx_len, bw), lambda b, j: (b, j))],
        out_shape=[jax.ShapeDtypeStruct((n_batch * seq, width), BF16),
                   jax.ShapeDtypeStruct((n_batch * ctx_len, width), BF16)],
        scratch_shapes=([pltpu.VMEM((seq + 2 * SUBLANES, bw), F32)]
                        + [pltpu.VMEM((SUBLANES * _lru_pitch(seq), bw), F32)] * 8),
        compiler_params=_params("parallel", "parallel"), name="rglru",
    )(proj, proj, proj, proj, conv_w, conv_b.reshape(1, width), w4, b4, lam)


def _row_tile(n_rows, cap):
    t = cap
    while n_rows % t:
        t //= 2
    return t


def kernel(x, c, ctx, c_ctx, w_mod, b_mod, norm_mix, norm_ffn, w_in, na_rpb, gla_wg, gla_bg, gla_norm,
           conv_w, conv_b, lru_wa, lru_ba, lru_wx, lru_bx, lru_lam, w_out, ffd_w1, ffd_w3, ffd_w2,
           router, moe_w1, moe_w3, moe_w2, final_norm):
    n_batch, seq, d = x.shape
    ctx_len = ctx.shape[1]
    depth = w_in.shape[0]
    n_lat = n_batch * seq
    n_tot = n_lat + n_batch * ctx_len
    na_heads = na_rpb.shape[1]
    na_w = na_heads * LANES
    gla_kw = gla_wg.shape[3]
    gla_heads = gla_kw // LANES
    gla_dv = gla_norm.shape[1]
    gla_vw = gla_heads * gla_dv
    lru_w = conv_w.shape[2]
    n_blk = lru_wa.shape[2]
    assert n_batch < SUBLANES and lru_w == n_blk * LANES and seq % (SUBLANES * GRID_W) == 0

    p_lr = 3 * na_w + 2 * gla_kw + 2 * gla_vw
    col_gla = 3 * na_w // LANES
    col_lru = p_lr // LANES

    tm = _row_tile(n_tot, 1024)
    tm_lat = _row_tile(n_lat, 1024)
    tr = _row_tile(n_tot, 512)

    streams = [x.reshape(n_lat, d), ctx.reshape(n_batch * ctx_len, d)]
    c8 = jnp.zeros((SUBLANES, d), F32).at[:n_batch].set(c).at[n_batch].set(c_ctx)

    for l in range(depth):
        last = l == depth - 1
        rows_out = n_lat if last else n_tot
        tmo = tm_lat if last else tm
        mod3 = _modulation(c8, w_mod, b_mod, l).reshape(SUBLANES * N_MOD, 1, d)

        h = _norm_mod(streams, norm_mix[l], mod3, 1, 0, n_tot, seq, n_batch, tr)
        w_l = w_in[l]
        w_main = jnp.concatenate([w_l[:, :p_lr], w_l[:, p_lr + 2 * GLA_LOWRANK:]], axis=1).astype(BF16)
        w_lr = jnp.pad(w_l[:, p_lr:p_lr + 2 * GLA_LOWRANK], ((0, 0), (0, LANES - 2 * GLA_LOWRANK))).astype(BF16)
        proj = _matmul(h, w_main, BF16, tm, _row_tile(w_main.shape[1], 512))
        lr = _matmul(h, w_lr, F32, tm, LANES)

        na_l, na_c = _neighbourhood_attention(proj, _na_bias_table(na_rpb[l]), n_batch, seq, ctx_len,
                                              na_heads, 0)
        wg = gla_wg[l]
        wg_pad = jnp.zeros((2, LANES, gla_kw), F32)
        wg_pad = wg_pad.at[0, :GLA_LOWRANK].set(wg[0]).at[1, GLA_LOWRANK:2 * GLA_LOWRANK].set(wg[1])
        gla_l, gla_c = _gla(proj, lr, wg_pad.astype(BF16), gla_bg[l], gla_norm[l], n_batch, seq, ctx_len,
                            gla_heads, LANES, gla_dv, col_gla)
        w4 = jnp.concatenate([lru_wa[l, 0], lru_wx[l, 0], lru_wa[l, 1], lru_wx[l, 1]], axis=-1).astype(BF16)
        b4 = jnp.concatenate([lru_ba[l, 0].reshape(n_blk, 1, LANES), lru_bx[l, 0].reshape(n_blk, 1, LANES),
                              lru_ba[l, 1].reshape(n_blk, 1, LANES), lru_bx[l, 1].reshape(n_blk, 1, LANES)],
                             axis=-1)
        lru_l, lru_c = _rglru(proj, conv_w[l], conv_b[l], w4, b4, lru_lam[l], n_batch, seq, ctx_len, col_lru)

        mixed = [[na_l, gla_l, lru_l]] + ([] if last else [[na_c, gla_c, lru_c]])
        x_all = _out_proj(mixed, w_out[l].astype(BF16), streams, mod3, 2, seq, n_batch, _row_tile(tm, 512),
                          _row_tile(d, 1024))

        j = l // 2
        if l % 2 == 0:
            h = _norm_mod([x_all], norm_ffn[l], mod3, 4, 3, rows_out, seq, n_batch, tr)
            act = _swiglu_in(h, ffd_w1[j].astype(BF16), ffd_w3[j].astype(BF16), tmo,
                             _row_tile(ffd_w1.shape[2], 512))
            x_all = _matmul_residual(act, ffd_w2[j].astype(BF16), x_all, mod3, 5, rows_out, seq, n_batch, tmo,
                                     _row_tile(d, 1024), _row_tile(act.shape[1], 2048))
        else:
            h, route = _norm_mod([x_all], norm_ffn[l], mod3, 4, 3, rows_out, seq, n_batch, tr, router=router[j])
            x_all = _moe(h, route, moe_w1[j].astype(BF16), moe_w3[j].astype(BF16), moe_w2[j].astype(BF16),
                         x_all, mod3, 5, rows_out, seq, n_batch, tn=_row_tile(moe_w1.shape[3], 512),
                         tc=_row_tile(rows_out, 256), out_norm_w=final_norm if last else None)
        streams = [x_all]

    out = x_all if depth % 2 == 0 else _final_norm(x_all, final_norm, n_lat, tr)
    return out.reshape(n_batch, seq, d)
```
